```python
import math
import jax, jax.numpy as jnp
from jax import lax
import numpy as np

D_MODEL = 2048
BATCH = 4
SEQ = 4096
DEPTH = 4

PLE_DIM = 256
EPS = 1e-6
NEG_INF = -1e30

SWA_HEADS = 16
SWA_KV_HEADS = 2
SWA_HEAD_DIM = 64
SWA_GROUP = SWA_HEADS // SWA_KV_HEADS
WINDOW = 128
SWA_BLOCK = 128

REL_BUCKETS = 32
REL_MAX_DIST = 128

GDN_QK_HEADS = 4
GDN_V_HEADS = 8
GDN_HEAD_DIM = 128
GDN_CONV = 4
GDN_CHUNK = 64

SWA_Q_W = SWA_HEADS * SWA_HEAD_DIM
SWA_KV_W = SWA_KV_HEADS * SWA_HEAD_DIM
GDN_QK_W = GDN_QK_HEADS * GDN_HEAD_DIM
GDN_V_W = GDN_V_HEADS * GDN_HEAD_DIM
GDN_CONV_CH = 2 * GDN_QK_W + GDN_V_W
MIX_WIDTH = SWA_Q_W + GDN_V_W
IN_SPLITS = (SWA_Q_W, SWA_KV_W, SWA_KV_W, GDN_QK_W, GDN_QK_W, GDN_V_W, GDN_V_W, GDN_V_HEADS, GDN_V_HEADS)
IN_WIDTH = sum(IN_SPLITS)

D_FF = 7 * D_MODEL // 2
N_EXPERTS = 8
TOP_K = 2
MOE_BLOCK = 256
N_DENSE = (DEPTH + 1) // 2
N_MOE = DEPTH // 2

kernel_name = "hymba_swa_sink_gdn_moe_ple_trunk"


def rmsnorm(x, gain):
    xf = x.astype(jnp.float32)
    y = xf * lax.rsqrt(jnp.mean(xf * xf, axis=-1, keepdims=True) + EPS)
    return (y * gain.astype(jnp.float32)).astype(x.dtype)


def l2norm(x):
    xf = x.astype(jnp.float32)
    return xf * lax.rsqrt(jnp.sum(xf * xf, axis=-1, keepdims=True) + EPS)


def relative_bucket(dist):
    max_exact = REL_BUCKETS // 2
    d = jnp.maximum(dist, 0)
    log_ratio = jnp.log(jnp.maximum(d, 1).astype(jnp.float32) / max_exact) / math.log(REL_MAX_DIST / max_exact)
    large = jnp.minimum(max_exact + (log_ratio * (REL_BUCKETS - max_exact)).astype(jnp.int32), REL_BUCKETS - 1)
    return jnp.where(d < max_exact, d, large)


def sliding_window_attention(q, k, v, sinks, rel_table):
    b, t = q.shape[0], q.shape[1]
    nb = t // SWA_BLOCK
    qb = q.reshape(b, nb, SWA_BLOCK, SWA_KV_HEADS, SWA_GROUP, SWA_HEAD_DIM)

    def windows(z):
        zp = jnp.pad(z, ((0, 0), (SWA_BLOCK, 0), (0, 0), (0, 0)))
        zp = zp.reshape(b, nb + 1, SWA_BLOCK, SWA_KV_HEADS, SWA_HEAD_DIM)
        return jnp.concatenate([zp[:, :-1], zp[:, 1:]], axis=2)

    kw, vw = windows(k), windows(v)
    qi = jnp.arange(SWA_BLOCK, dtype=jnp.int32)[:, None]
    kj = jnp.arange(2 * SWA_BLOCK, dtype=jnp.int32)[None, :]
    dist = qi + SWA_BLOCK - kj
    band = (dist >= 0) & (dist < WINDOW)
    key_pos = jnp.arange(nb, dtype=jnp.int32)[:, None] * SWA_BLOCK - SWA_BLOCK + kj
    valid = band[None] & (key_pos >= 0)[:, None, :]
    bias = rel_table[relative_bucket(dist)].astype(jnp.float32)
    bias = bias.transpose(2, 0, 1).reshape(SWA_KV_HEADS, SWA_GROUP, SWA_BLOCK, 2 * SWA_BLOCK)
    logits = jnp.einsum('bnqhgd,bnkhd->bnhgqk', qb, kw, preferred_element_type=jnp.float32)
    logits = logits * (SWA_HEAD_DIM ** -0.5) + bias
    logits = jnp.where(valid[None, :, None, None], logits, NEG_INF)
    sink = sinks.astype(jnp.float32).reshape(SWA_KV_HEADS, SWA_GROUP)[:, :, None, None]
    m = jnp.maximum(jnp.max(logits, axis=-1, keepdims=True), sink)
    e = jnp.exp(logits - m)
    probs = e / (jnp.sum(e, axis=-1, keepdims=True) + jnp.exp(sink - m))
    out = jnp.einsum('bnhgqk,bnkhd->bnqhgd', probs.astype(v.dtype), vw)
    return out.reshape(b, t, SWA_Q_W)


def causal_depthwise_conv(x, w):
    return lax.conv_general_dilated(
        x, w[:, None, :].astype(x.dtype), window_strides=(1,), padding=[(GDN_CONV - 1, 0)],
        dimension_numbers=('NWC', 'WIO', 'NWC'), feature_group_count=x.shape[-1])


def gated_delta_rule(q, k, v, g, beta):
    b, t, h, dk = q.shape
    n = t // GDN_CHUNK
    c = GDN_CHUNK

    def chunks(z):
        return z.astype(jnp.float32).reshape(b, n, c, h, -1).transpose(0, 3, 1, 2, 4)

    q = chunks(q) * (dk ** -0.5)
    k = chunks(k)
    v = chunks(v)
    g = jnp.cumsum(g.astype(jnp.float32).reshape(b, n, c, h).transpose(0, 3, 1, 2), axis=-1)
    beta = beta.astype(jnp.float32).reshape(b, n, c, h).transpose(0, 3, 1, 2)

    causal = jnp.tril(jnp.ones((c, c), dtype=bool))
    strict = jnp.tril(jnp.ones((c, c), dtype=bool), -1)
    diff = g[..., :, None] - g[..., None, :]
    decay = jnp.where(causal, jnp.exp(jnp.where(causal, diff, 0.0)), 0.0)
    kb = k * beta[..., None]
    lower = jnp.where(strict, jnp.einsum('bhncd,bhnsd->bhncs', kb, k) * decay, 0.0)
    eye = jnp.eye(c, dtype=jnp.float32)
    tinv = lax.linalg.triangular_solve(eye + lower, jnp.broadcast_to(eye, lower.shape),
                                       left_side=True, lower=True)
    u = jnp.einsum('bhncs,bhnsd->bhncd', tinv, v * beta[..., None])
    w = jnp.einsum('bhncs,bhnsd->bhncd', tinv, kb * jnp.exp(g)[..., None])
    qk = jnp.where(causal, jnp.einsum('bhncd,bhnsd->bhncs', q, k) * decay, 0.0)
    q_dec = q * jnp.exp(g)[..., None]
    k_dec = k * jnp.exp(g[..., -1:] - g)[..., None]
    chunk_decay = jnp.exp(g[..., -1])

    def step(state, xs):
        w_c, u_c, qk_c, qd_c, kd_c, dec_c = xs
        v_new = u_c - jnp.einsum('bhcd,bhde->bhce', w_c, state)
        o_c = jnp.einsum('bhcd,bhde->bhce', qd_c, state) + jnp.einsum('bhcs,bhse->bhce', qk_c, v_new)
        state = state * dec_c[..., None, None] + jnp.einsum('bhcd,bhce->bhde', kd_c, v_new)
        return state, o_c

    xs = tuple(jnp.moveaxis(z, 2, 0) for z in (w, u, qk, q_dec, k_dec, chunk_decay))
    s0 = jnp.zeros((b, h, dk, v.shape[-1]), jnp.float32)
    _, o = lax.scan(step, s0, xs)
    return o.transpose(1, 0, 3, 2, 4).reshape(b, t, h, -1)


def hybrid_mixer(h, w_in, conv_w, a_log, dt_bias, gdn_norm, sinks, rel_table, w_out):
    b, t, _ = h.shape
    proj = h @ w_in
    offs = np.cumsum(IN_SPLITS)[:-1].tolist()
    qa, ka, va, qg, kg, vg, z, beta_in, a_in = jnp.split(proj, offs, axis=-1)
    attn = sliding_window_attention(
        qa.reshape(b, t, SWA_HEADS, SWA_HEAD_DIM),
        ka.reshape(b, t, SWA_KV_HEADS, SWA_HEAD_DIM),
        va.reshape(b, t, SWA_KV_HEADS, SWA_HEAD_DIM), sinks, rel_table)
    qkv = jax.nn.silu(causal_depthwise_conv(jnp.concatenate([qg, kg, vg], axis=-1), conv_w))
    qg, kg, vg = jnp.split(qkv, [GDN_QK_W, 2 * GDN_QK_W], axis=-1)
    rep = GDN_V_HEADS // GDN_QK_HEADS
    qg = jnp.repeat(l2norm(qg.reshape(b, t, GDN_QK_HEADS, GDN_HEAD_DIM)), rep, axis=2)
    kg = jnp.repeat(l2norm(kg.reshape(b, t, GDN_QK_HEADS, GDN_HEAD_DIM)), rep, axis=2)
    vg = vg.reshape(b, t, GDN_V_HEADS, GDN_HEAD_DIM)
    beta = jax.nn.sigmoid(beta_in.astype(jnp.float32))
    g = -jnp.exp(a_log.astype(jnp.float32)) * jax.nn.softplus(a_in.astype(jnp.float32) + dt_bias.astype(jnp.float32))
    o = gated_delta_rule(qg, kg, vg, g, beta)
    o = o * lax.rsqrt(jnp.mean(o * o, axis=-1, keepdims=True) + EPS) * gdn_norm.astype(jnp.float32)
    o = o * jax.nn.silu(z.astype(jnp.float32)).reshape(b, t, GDN_V_HEADS, GDN_HEAD_DIM)
    o = o.reshape(b, t, GDN_V_W).astype(h.dtype)
    return jnp.concatenate([attn, o], axis=-1) @ w_out


def swiglu(h, w_gate, w_up, w_down):
    return (jax.nn.silu(h @ w_gate) * (h @ w_up)) @ w_down


def moe_swiglu(h, w_router, w_gate, w_up, w_down):
    b, t, d = h.shape
    hf = h.reshape(-1, d)
    n = hf.shape[0]
    logits = (hf @ w_router).astype(jnp.float32)
    top_logit, top_idx = lax.top_k(logits, TOP_K)
    gates = jax.nn.softmax(top_logit, axis=-1)
    na = n * TOP_K
    flat_e = top_idx.reshape(-1).astype(jnp.int32)
    flat_tok = jnp.arange(na, dtype=jnp.int32) // TOP_K
    flat_g = gates.reshape(-1)
    order = jnp.argsort(flat_e)
    e_sorted = flat_e[order]
    counts = jnp.bincount(flat_e, length=N_EXPERTS).astype(jnp.int32)
    padded = (counts + MOE_BLOCK - 1) // MOE_BLOCK * MOE_BLOCK
    pad_end = jnp.cumsum(padded)
    pad_start = pad_end - padded
    start = jnp.cumsum(counts) - counts
    dest = pad_start[e_sorted] + jnp.arange(na, dtype=jnp.int32) - start[e_sorted]
    n_blocks = -(-na // MOE_BLOCK) + N_EXPERTS
    cap = n_blocks * MOE_BLOCK
    slot_tok = jnp.zeros((cap,), jnp.int32).at[dest].set(flat_tok[order])
    slot_w = jnp.zeros((cap,), jnp.float32).at[dest].set(flat_g[order])
    block_e = jnp.minimum(
        jnp.searchsorted(pad_end, jnp.arange(n_blocks, dtype=jnp.int32) * MOE_BLOCK, side='right'),
        N_EXPERTS - 1)
    xb = hf[slot_tok].reshape(n_blocks, MOE_BLOCK, d)

    def expert_block(args):
        xs, e = args
        return (jax.nn.silu(xs @ w_gate[e]) * (xs @ w_up[e])) @ w_down[e]

    yb = lax.map(expert_block, (xb, block_e))
    y = jnp.zeros((n, d), jnp.float32).at[slot_tok].add(
        yb.reshape(cap, d).astype(jnp.float32) * slot_w[:, None])
    return y.astype(h.dtype).reshape(b, t, d)


def setup_inputs(seed: int = 0) -> dict:
    key = jax.random.key(seed)
    ks = jax.random.split(key, 24)

    def nrm(k, shape, scale):
        return jax.random.normal(k, shape, jnp.float32) * scale

    dt = jnp.exp(jax.random.uniform(ks[5], (DEPTH, GDN_V_HEADS), jnp.float32,
                                    minval=math.log(1e-3), maxval=math.log(1e-1)))
    return {
        "x": nrm(ks[0], (BATCH, SEQ, D_MODEL), 1.0),
        "p": nrm(ks[1], (DEPTH, BATCH, SEQ, PLE_DIM), 1.0),
        "w_in": nrm(ks[2], (DEPTH, D_MODEL, IN_WIDTH), D_MODEL ** -0.5),
        "conv_w": nrm(ks[3], (DEPTH, GDN_CONV, GDN_CONV_CH), GDN_CONV ** -0.5),
        "a_log": jnp.log(jax.random.uniform(ks[4], (DEPTH, GDN_V_HEADS), jnp.float32, minval=1.0, maxval=16.0)),
        "dt_bias": dt + jnp.log(-jnp.expm1(-dt)),
        "gdn_norm": 1.0 + nrm(ks[6], (DEPTH, GDN_HEAD_DIM), 0.05),
        "attn_sinks": nrm(ks[7], (DEPTH, SWA_HEADS), 0.5),
        "rel_bias_table": nrm(ks[8], (REL_BUCKETS, SWA_HEADS), 0.5),
        "w_out": nrm(ks[9], (DEPTH, MIX_WIDTH, D_MODEL), MIX_WIDTH ** -0.5),
        "norm_mix": 1.0 + nrm(ks[10], (DEPTH, D_MODEL), 0.05),
        "norm_ffn": 1.0 + nrm(ks[11], (DEPTH, D_MODEL), 0.05),
        "w_dense_gate": nrm(ks[12], (N_DENSE, D_MODEL, D_FF), D_MODEL ** -0.5),
        "w_dense_up": nrm(ks[13], (N_DENSE, D_MODEL, D_FF), D_MODEL ** -0.5),
        "w_dense_down": nrm(ks[14], (N_DENSE, D_FF, D_MODEL), D_FF ** -0.5),
        "w_router": nrm(ks[15], (N_MOE, D_MODEL, N_EXPERTS), D_MODEL ** -0.5),
        "w_exp_gate": nrm(ks[16], (N_MOE, N_EXPERTS, D_MODEL, D_FF), D_MODEL ** -0.5),
        "w_exp_up": nrm(ks[17], (N_MOE, N_EXPERTS, D_MODEL, D_FF), D_MODEL ** -0.5),
        "w_exp_down": nrm(ks[18], (N_MOE, N_EXPERTS, D_FF, D_MODEL), D_FF ** -0.5),
        "norm_ple": 1.0 + nrm(ks[19], (DEPTH, D_MODEL), 0.05),
        "w_ple_gate": nrm(ks[20], (DEPTH, D_MODEL, D_MODEL), D_MODEL ** -0.5),
        "w_ple_proj": nrm(ks[21], (DEPTH, PLE_DIM, D_MODEL), PLE_DIM ** -0.5),
        "norm_final": 1.0 + nrm(ks[22], (D_MODEL,), 0.05),
    }


def reference(x, p, w_in, conv_w, a_log, dt_bias, gdn_norm, attn_sinks, rel_bias_table, w_out,
              norm_mix, norm_ffn, w_dense_gate, w_dense_up, w_dense_down, w_router,
              w_exp_gate, w_exp_up, w_exp_down, norm_ple, w_ple_gate, w_ple_proj, norm_final):
    for i in range(DEPTH):
        x = x + hybrid_mixer(rmsnorm(x, norm_mix[i]), w_in[i], conv_w[i], a_log[i], dt_bias[i],
                             gdn_norm[i], attn_sinks[i], rel_bias_table, w_out[i])
        hf = rmsnorm(x, norm_ffn[i])
        if i % 2 == 0:
            x = x + swiglu(hf, w_dense_gate[i // 2], w_dense_up[i // 2], w_dense_down[i // 2])
        else:
            x = x + moe_swiglu(hf, w_router[i // 2], w_exp_gate[i // 2], w_exp_up[i // 2], w_exp_down[i // 2])
        gate = jax.nn.sigmoid(rmsnorm(x, norm_ple[i]) @ w_ple_gate[i])
        x = x + (p[i] @ w_ple_proj[i]) * gate
    return rmsnorm(x, norm_final)
```

```python
import functools
import math

import jax
import jax.numpy as jnp
from jax import lax
from jax.experimental import pallas as pl
from jax.experimental.pallas import tpu as pltpu

F32 = jnp.float32
BF16 = jnp.bfloat16

D_MODEL = 2048
PLE_DIM = 256
EPS = 1e-6
NEG_INF = -1e30

SWA_HEADS = 16
SWA_KV_HEADS = 2
SWA_HEAD_DIM = 64
SWA_GROUP = SWA_HEADS // SWA_KV_HEADS
WINDOW = 128
SWA_BLOCK = 128
REL_BUCKETS = 32
REL_MAX_DIST = 128

GDN_QK_HEADS = 4
GDN_V_HEADS = 8
GDN_HEAD_DIM = 128
GDN_CONV = 4
GDN_CHUNK = 64

SWA_Q_W = SWA_HEADS * SWA_HEAD_DIM
SWA_KV_W = SWA_KV_HEADS * SWA_HEAD_DIM
GDN_QK_W = GDN_QK_HEADS * GDN_HEAD_DIM
GDN_V_W = GDN_V_HEADS * GDN_HEAD_DIM
GDN_CONV_CH = 2 * GDN_QK_W + GDN_V_W
MIX_WIDTH = SWA_Q_W + GDN_V_W
IN_MAIN_W = SWA_Q_W + 2 * SWA_KV_W + GDN_CONV_CH + GDN_V_W
IN_TAIL_W = 2 * GDN_V_HEADS

D_FF = 7 * D_MODEL // 2
N_EXPERTS = 8
TOP_K = 2

LANES = 128
SUBLANES = 8
VMEM_LIMIT = 56 * 1024 * 1024

COL_K = SWA_Q_W // LANES
COL_V = COL_K + SWA_KV_W // LANES
COL_GDN = COL_V + SWA_KV_W // LANES
COL_Z = COL_GDN + GDN_CONV_CH // LANES


def _params(*sem):
    return pltpu.CompilerParams(dimension_semantics=sem, vmem_limit_bytes=VMEM_LIMIT)


def _rms_kernel(x_ref, g_ref, o_ref):
    x = x_ref[...]
    y = x * lax.rsqrt(jnp.mean(x * x, axis=-1, keepdims=True) + EPS)
    o_ref[...] = (y * g_ref[...]).astype(o_ref.dtype)


def rmsnorm_rows(x, gain, out_dtype, tm=512):
    n, d = x.shape
    return pl.pallas_call(
        _rms_kernel,
        grid=(n // tm,),
        in_specs=[pl.BlockSpec((tm, d), lambda i: (i, 0)), pl.BlockSpec((1, d), lambda i: (0, 0))],
        out_specs=pl.BlockSpec((tm, d), lambda i: (i, 0)),
        out_shape=jax.ShapeDtypeStruct((n, d), out_dtype),
        compiler_params=_params("parallel"),
        name="rmsnorm",
    )(x, gain.reshape(1, d))


def _mm_kernel(a_ref, w_ref, o_ref):
    o_ref[...] = jnp.dot(a_ref[...], w_ref[...].astype(BF16), preferred_element_type=F32).astype(o_ref.dtype)


def matmul_rows_outer(a, w, li, n_out, tm, tn, out_dtype=F32):
    m, k = a.shape
    return pl.pallas_call(
        _mm_kernel,
        grid=(m // tm, n_out // tn),
        in_specs=[pl.BlockSpec((tm, k), lambda i, j: (i, 0)),
                  pl.BlockSpec((None, k, tn), lambda i, j: (li, 0, j))],
        out_specs=pl.BlockSpec((tm, tn), lambda i, j: (i, j)),
        out_shape=jax.ShapeDtypeStruct((m, n_out), out_dtype),
        compiler_params=_params("parallel", "arbitrary"),
        name="matmul",
    )(a, w)


def _attn_kernel(sink_ref, q_ref, kc_ref, kp_ref, vc_ref, vp_ref, bias_ref, o_ref):
    n = pl.program_id(1)
    qi = lax.broadcasted_iota(jnp.int32, (SWA_BLOCK, 2 * SWA_BLOCK), 0)
    kj = lax.broadcasted_iota(jnp.int32, (SWA_BLOCK, 2 * SWA_BLOCK), 1)
    dist = qi + SWA_BLOCK - kj
    first_key = jnp.where(n == 0, SWA_BLOCK, 0)
    valid = (dist >= 0) & (dist < WINDOW) & (kj >= first_key)
    scale = SWA_HEAD_DIM ** -0.5
    for g in range(SWA_KV_HEADS):
        cols = slice(g * SWA_HEAD_DIM, (g + 1) * SWA_HEAD_DIM)
        kw = jnp.concatenate([kp_ref[:, cols], kc_ref[:, cols]], axis=0).astype(BF16)
        vw = jnp.concatenate([vp_ref[:, cols], vc_ref[:, cols]], axis=0).astype(BF16)
        for pair in range(SWA_GROUP // 2):
            outs = []
            for hh in range(2):
                h = g * SWA_GROUP + pair * 2 + hh
                qh = q_ref[:, h * SWA_HEAD_DIM:(h + 1) * SWA_HEAD_DIM].astype(BF16)
                logits = lax.dot_general(qh, kw, (((1,), (1,)), ((), ())), preferred_element_type=F32)
                logits = logits * scale + bias_ref[h]
                logits = jnp.where(valid, logits, NEG_INF)
                sink = sink_ref[h]
                m = jnp.maximum(jnp.max(logits, axis=-1, keepdims=True), sink)
                e = jnp.exp(logits - m)
                denom = jnp.sum(e, axis=-1, keepdims=True) + jnp.exp(sink - m)
                p = e / denom
                outs.append(jnp.dot(p.astype(BF16), vw, preferred_element_type=F32))
            h0 = g * SWA_GROUP + pair * 2
            o_ref[:, h0 * SWA_HEAD_DIM:(h0 + 2) * SWA_HEAD_DIM] = jnp.concatenate(outs, axis=-1).astype(o_ref.dtype)


def relative_bias(rel_table):
    qi = jnp.arange(SWA_BLOCK, dtype=jnp.int32)[:, None]
    kj = jnp.arange(2 * SWA_BLOCK, dtype=jnp.int32)[None, :]
    dist = qi + SWA_BLOCK - kj
    max_exact = REL_BUCKETS // 2
    d = jnp.maximum(dist, 0)
    log_ratio = jnp.log(jnp.maximum(d, 1).astype(F32) / max_exact) / math.log(REL_MAX_DIST / max_exact)
    large = jnp.minimum(max_exact + (log_ratio * (REL_BUCKETS - max_exact)).astype(jnp.int32), REL_BUCKETS - 1)
    bucket = jnp.where(d < max_exact, d, large)
    return rel_table[bucket].astype(F32).transpose(2, 0, 1)


def swa_attention(proj, sinks, bias, batch, seq):
    n = proj.shape[0]
    nb = seq // SWA_BLOCK
    cur = lambda b, i, s: b * nb + i
    prev = lambda b, i, s: jnp.maximum(b * nb + i - 1, 0)
    blk = (SWA_BLOCK, LANES)
    return pl.pallas_call(
        _attn_kernel,
        grid_spec=pltpu.PrefetchScalarGridSpec(
            num_scalar_prefetch=1,
            grid=(batch, nb),
            in_specs=[
                pl.BlockSpec((SWA_BLOCK, SWA_Q_W), lambda b, i, s: (cur(b, i, s), 0)),
                pl.BlockSpec(blk, lambda b, i, s: (cur(b, i, s), COL_K)),
                pl.BlockSpec(blk, lambda b, i, s: (prev(b, i, s), COL_K)),
                pl.BlockSpec(blk, lambda b, i, s: (cur(b, i, s), COL_V)),
                pl.BlockSpec(blk, lambda b, i, s: (prev(b, i, s), COL_V)),
                pl.BlockSpec((SWA_HEADS, SWA_BLOCK, 2 * SWA_BLOCK), lambda b, i, s: (0, 0, 0)),
            ],
            out_specs=pl.BlockSpec((SWA_BLOCK, SWA_Q_W), lambda b, i, s: (cur(b, i, s), 0)),
        ),
        out_shape=jax.ShapeDtypeStruct((n, SWA_Q_W), BF16),
        compiler_params=_params("parallel", "parallel"),
        name="swa_attention",
    )(sinks, proj, proj, proj, proj, proj, bias)


def _gdn_prep_kernel(cur_ref, prev_ref, w_ref, o_ref, *, tiles_per_seq):
    i = pl.program_id(0)
    c = pl.program_id(1)
    tm = cur_ref.shape[0]
    cur = cur_ref[...]
    prev = jnp.where(i % tiles_per_seq == 0, 0.0, prev_ref[...])
    xp = jnp.concatenate([prev, cur], axis=0)
    w = w_ref[...]
    acc = xp[SUBLANES - 3:SUBLANES - 3 + tm] * w[0:1]
    for k in range(1, GDN_CONV):
        off = SUBLANES - (GDN_CONV - 1) + k
        acc = acc + xp[off:off + tm] * w[k:k + 1]
    y = acc * jax.nn.sigmoid(acc)
    normed = y * lax.rsqrt(jnp.sum(y * y, axis=-1, keepdims=True) + EPS)
    o_ref[...] = jnp.where(c < 2 * GDN_QK_HEADS, normed, y)


def gdn_prep(proj, conv_w, seq, tm=512):
    n = proj.shape[0]
    heads = GDN_CONV_CH // LANES
    return pl.pallas_call(
        functools.partial(_gdn_prep_kernel, tiles_per_seq=seq // tm),
        grid=(n // tm, heads),
        in_specs=[
            pl.BlockSpec((tm, LANES), lambda i, c: (i, COL_GDN + c)),
            pl.BlockSpec((SUBLANES, LANES), lambda i, c: (jnp.maximum(i * (tm // SUBLANES) - 1, 0), COL_GDN + c)),
            pl.BlockSpec((GDN_CONV, LANES), lambda i, c: (0, c)),
        ],
        out_specs=pl.BlockSpec((tm, LANES), lambda i, c: (i, c)),
        out_shape=jax.ShapeDtypeStruct((n, GDN_CONV_CH), F32),
        compiler_params=_params("parallel", "parallel"),
        name="gdn_conv_norm",
    )(proj, proj, conv_w)


def _gdn_gate_kernel(t_ref, alog_ref, dtb_ref, o_ref):
    tm = t_ref.shape[0]
    t = t_ref[...]
    lane = lax.broadcasted_iota(jnp.int32, t.shape, 1)
    beta = jax.nn.sigmoid(t)
    g = -jnp.exp(alog_ref[...]) * jax.nn.softplus(t + dtb_ref[...])
    g = jnp.where((lane >= GDN_V_HEADS) & (lane < 2 * GDN_V_HEADS), g, 0.0)
    r = lax.broadcasted_iota(jnp.int32, (tm, tm), 0)
    s = lax.broadcasted_iota(jnp.int32, (tm, tm), 1)
    tri = jnp.where((s <= r) & (r // GDN_CHUNK == s // GDN_CHUNK), 1.0, 0.0).astype(F32)
    gcum = jnp.dot(tri, g, preferred_element_type=F32, precision=lax.Precision.HIGHEST)
    o_ref[...] = jnp.where(lane < GDN_V_HEADS, beta, gcum)


def gdn_gates(tail, a_log, dt_bias, tm=512):
    n = tail.shape[0]
    pad = lambda v: jnp.zeros((1, LANES), F32).at[0, GDN_V_HEADS:2 * GDN_V_HEADS].set(v.astype(F32))
    return pl.pallas_call(
        _gdn_gate_kernel,
        grid=(n // tm,),
        in_specs=[pl.BlockSpec((tm, LANES), lambda i: (i, 0)),
                  pl.BlockSpec((1, LANES), lambda i: (0, 0)),
                  pl.BlockSpec((1, LANES), lambda i: (0, 0))],
        out_specs=pl.BlockSpec((tm, LANES), lambda i: (i, 0)),
        out_shape=jax.ShapeDtypeStruct((n, LANES), F32),
        compiler_params=_params("parallel"),
        name="gdn_gates",
    )(tail, pad(a_log), pad(dt_bias))


def _unit_lower_inverse(low):
    c = low.shape[0]
    r = lax.broadcasted_iota(jnp.int32, (c, c), 0)
    s = lax.broadcasted_iota(jnp.int32, (c, c), 1)
    eye = jnp.where(r == s, 1.0, 0.0).astype(F32)
    size = 1
    x = eye
    while size < c:
        off = jnp.where((r // (2 * size) == s // (2 * size)) & ((r // size) % 2 == 1) & ((s // size) % 2 == 0),
                        low, 0.0)
        if size == 1:
            x = x - off
        else:
            xo = jnp.dot(x, off, preferred_element_type=F32, precision=lax.Precision.HIGHEST)
            x = x - jnp.dot(xo, x, preferred_element_type=F32, precision=lax.Precision.HIGHEST)
        size *= 2
    return x


def _gdn_kernel(q_ref, k_ref, v_ref, z_ref, gb_ref, grow_ref, norm_ref, o_ref, state_ref):
    h = pl.program_id(1)
    t = pl.program_id(2)
    tb = q_ref.shape[0]
    c = GDN_CHUNK

    @pl.when(t == 0)
    def _():
        state_ref[...] = jnp.zeros_like(state_ref)

    lane = lax.broadcasted_iota(jnp.int32, (tb, LANES), 1)
    gb = gb_ref[...]
    beta_all = jnp.sum(jnp.where(lane == h, gb, 0.0), axis=-1, keepdims=True)
    g_all = jnp.sum(jnp.where(lane == h + GDN_V_HEADS, gb, 0.0), axis=-1, keepdims=True)
    r = lax.broadcasted_iota(jnp.int32, (c, c), 0)
    s = lax.broadcasted_iota(jnp.int32, (c, c), 1)
    causal = s <= r
    strict = s < r
    nt = (((1,), (1,)), ((), ()))
    for ci in range(tb // c):
        rows = slice(ci * c, (ci + 1) * c)
        q = q_ref[rows, :] * (GDN_HEAD_DIM ** -0.5)
        k = k_ref[rows, :]
        v = v_ref[rows, :]
        beta = beta_all[rows]
        g = g_all[rows]
        g_row = grow_ref[ci:ci + 1, :]
        decay = jnp.where(causal, jnp.exp(jnp.where(causal, g - g_row, 0.0)), 0.0)
        kb = k * beta
        low = jnp.where(strict, lax.dot_general(kb, k, nt, preferred_element_type=F32) * decay, 0.0)
        tinv = _unit_lower_inverse(low)
        eg = jnp.exp(g)
        u = jnp.dot(tinv, v * beta, preferred_element_type=F32)
        w = jnp.dot(tinv, kb * eg, preferred_element_type=F32)
        qk = jnp.where(causal, lax.dot_general(q, k, nt, preferred_element_type=F32) * decay, 0.0)
        g_last = g[c - 1:c, :]
        k_dec = k * jnp.exp(g_last - g)
        state = state_ref[...]
        v_new = u - jnp.dot(w, state, preferred_element_type=F32)
        o = jnp.dot(q * eg, state, preferred_element_type=F32) + jnp.dot(qk, v_new, preferred_element_type=F32)
        state_ref[...] = state * jnp.exp(g_last) + lax.dot_general(
            k_dec, v_new, (((0,), (0,)), ((), ())), preferred_element_type=F32)
        o = o * lax.rsqrt(jnp.mean(o * o, axis=-1, keepdims=True) + EPS) * norm_ref[...]
        zc = z_ref[rows, :]
        o_ref[rows, :] = (o * (zc * jax.nn.sigmoid(zc))).astype(o_ref.dtype)


def gdn_mixer(qkv, proj, gates, g_rows, gdn_norm, batch, seq, tb=512):
    n = qkv.shape[0]
    nt = seq // tb
    rep = GDN_V_HEADS // GDN_QK_HEADS
    row = lambda b, h, t: b * nt + t
    blk = (tb, LANES)
    return pl.pallas_call(
        _gdn_kernel,
        grid=(batch, GDN_V_HEADS, nt),
        in_specs=[
            pl.BlockSpec(blk, lambda b, h, t: (row(b, h, t), h // rep)),
            pl.BlockSpec(blk, lambda b, h, t: (row(b, h, t), GDN_QK_HEADS + h // rep)),
            pl.BlockSpec(blk, lambda b, h, t: (row(b, h, t), 2 * GDN_QK_HEADS + h)),
            pl.BlockSpec(blk, lambda b, h, t: (row(b, h, t), COL_Z + h)),
            pl.BlockSpec(blk, lambda b, h, t: (row(b, h, t), 0)),
            pl.BlockSpec((None, tb // GDN_CHUNK, GDN_CHUNK), lambda b, h, t: (h, row(b, h, t), 0)),
            pl.BlockSpec((1, LANES), lambda b, h, t: (0, 0)),
        ],
        out_specs=pl.BlockSpec(blk, lambda b, h, t: (row(b, h, t), h)),
        out_shape=jax.ShapeDtypeStruct((n, GDN_V_W), BF16),
        scratch_shapes=[pltpu.VMEM((GDN_HEAD_DIM, GDN_HEAD_DIM), F32)],
        compiler_params=_params("parallel", "parallel", "arbitrary"),
        name="gated_delta_rule",
    )(qkv, qkv, qkv, proj, gates, g_rows, gdn_norm.reshape(1, LANES).astype(F32))


def _out_proj_kernel(a_ref, b_ref, wa_ref, wb_ref, x_ref, o_ref):
    acc = jnp.dot(a_ref[...], wa_ref[...].astype(BF16), preferred_element_type=F32)
    acc += jnp.dot(b_ref[...], wb_ref[...].astype(BF16), preferred_element_type=F32)
    o_ref[...] = x_ref[...] + acc


def out_proj_residual(attn, gdn, w_out, li, x, tm=512, tn=512):
    n, d = x.shape
    ka, kb = attn.shape[1], gdn.shape[1]
    assert ka == kb
    return pl.pallas_call(
        _out_proj_kernel,
        grid=(d // tn, n // tm),
        in_specs=[
            pl.BlockSpec((tm, ka), lambda j, i: (i, 0)),
            pl.BlockSpec((tm, kb), lambda j, i: (i, 0)),
            pl.BlockSpec((None, ka, tn), lambda j, i: (li, 0, j)),
            pl.BlockSpec((None, kb, tn), lambda j, i: (li, 1, j)),
            pl.BlockSpec((tm, tn), lambda j, i: (i, j)),
        ],
        out_specs=pl.BlockSpec((tm, tn), lambda j, i: (i, j)),
        out_shape=jax.ShapeDtypeStruct((n, d), F32),
        compiler_params=_params("parallel", "parallel"),
        name="out_proj_residual",
    )(attn, gdn, w_out, w_out, x)


def _gate_up_kernel(h_ref, wg_ref, wu_ref, o_ref):
    h = h_ref[...]
    g = jnp.dot(h, wg_ref[...].astype(BF16), preferred_element_type=F32)
    u = jnp.dot(h, wu_ref[...].astype(BF16), preferred_element_type=F32)
    o_ref[...] = (g * jax.nn.sigmoid(g) * u).astype(o_ref.dtype)


def gate_up(h, w_gate, w_up, li, tm=512, tn=512):
    m, k = h.shape
    f = w_gate.shape[2]
    return pl.pallas_call(
        _gate_up_kernel,
        grid=(f // tn, m // tm),
        in_specs=[
            pl.BlockSpec((tm, k), lambda j, i: (i, 0)),
            pl.BlockSpec((None, k, tn), lambda j, i: (li, 0, j)),
            pl.BlockSpec((None, k, tn), lambda j, i: (li, 0, j)),
        ],
        out_specs=pl.BlockSpec((tm, tn), lambda j, i: (i, j)),
        out_shape=jax.ShapeDtypeStruct((m, f), BF16),
        compiler_params=_params("parallel", "parallel"),
        name="swiglu_gate_up",
    )(h, w_gate, w_up)


def _gate_up_grouped_kernel(be_ref, h_ref, wg_ref, wu_ref, o_ref):
    _gate_up_kernel(h_ref, wg_ref, wu_ref, o_ref)


def gate_up_grouped(block_expert, h, w_gate, w_up, li, tm, tn=512):
    m, k = h.shape
    f = w_gate.shape[3]
    return pl.pallas_call(
        _gate_up_grouped_kernel,
        grid_spec=pltpu.PrefetchScalarGridSpec(
            num_scalar_prefetch=1,
            grid=(f // tn, m // tm),
            in_specs=[
                pl.BlockSpec((tm, k), lambda j, i, be: (i, 0)),
                pl.BlockSpec((None, None, k, tn), lambda j, i, be: (li, be[i], 0, j)),
                pl.BlockSpec((None, None, k, tn), lambda j, i, be: (li, be[i], 0, j)),
            ],
            out_specs=pl.BlockSpec((tm, tn), lambda j, i, be: (i, j)),
        ),
        out_shape=jax.ShapeDtypeStruct((m, f), BF16),
        compiler_params=_params("parallel", "parallel"),
        name="moe_gate_up",
    )(block_expert, h, w_gate, w_up)


def _down_kernel(a_ref, w_ref, x_ref, o_ref, acc_ref):
    kk = pl.program_id(1)

    @pl.when(kk == 0)
    def _():
        acc_ref[...] = jnp.zeros_like(acc_ref)

    acc_ref[...] += jnp.dot(a_ref[...], w_ref[...].astype(BF16), preferred_element_type=F32)

    @pl.when(kk == pl.num_programs(1) - 1)
    def _():
        o_ref[...] = x_ref[...] + acc_ref[...]


def down_residual(a, w_down, li, x, tm=512, tk=512):
    n, d = x.shape
    f = a.shape[1]
    return pl.pallas_call(
        _down_kernel,
        grid=(n // tm, f // tk),
        in_specs=[
            pl.BlockSpec((tm, tk), lambda i, kk: (i, kk)),
            pl.BlockSpec((None, tk, d), lambda i, kk: (li, kk, 0)),
            pl.BlockSpec((tm, d), lambda i, kk: (i, 0)),
        ],
        out_specs=pl.BlockSpec((tm, d), lambda i, kk: (i, 0)),
        out_shape=jax.ShapeDtypeStruct((n, d), F32),
        scratch_shapes=[pltpu.VMEM((tm, d), F32)],
        compiler_params=_params("parallel", "arbitrary"),
        name="swiglu_down_residual",
    )(a, w_down, x)


def _down_grouped_kernel(be_ref, a_ref, w_ref, o_ref, acc_ref):
    kk = pl.program_id(1)

    @pl.when(kk == 0)
    def _():
        acc_ref[...] = jnp.zeros_like(acc_ref)

    acc_ref[...] += jnp.dot(a_ref[...], w_ref[...].astype(BF16), preferred_element_type=F32)

    @pl.when(kk == pl.num_programs(1) - 1)
    def _():
        o_ref[...] = acc_ref[...]


def down_grouped(block_expert, a, w_down, li, tm, tk=512):
    m, f = a.shape
    d = w_down.shape[3]
    return pl.pallas_call(
        _down_grouped_kernel,
        grid_spec=pltpu.PrefetchScalarGridSpec(
            num_scalar_prefetch=1,
            grid=(m // tm, f // tk),
            in_specs=[
                pl.BlockSpec((tm, tk), lambda i, kk, be: (i, kk)),
                pl.BlockSpec((None, None, tk, d), lambda i, kk, be: (li, be[i], kk, 0)),
            ],
            out_specs=pl.BlockSpec((tm, d), lambda i, kk, be: (i, 0)),
            scratch_shapes=[pltpu.VMEM((tm, d), F32)],
        ),
        out_shape=jax.ShapeDtypeStruct((m, d), F32),
        compiler_params=_params("parallel", "arbitrary"),
        name="moe_down",
    )(block_expert, a, w_down)


def _router_kernel(x_ref, g_ref, w_ref, h_ref, idx_ref, gate_ref):
    x = x_ref[...]
    h = x * lax.rsqrt(jnp.mean(x * x, axis=-1, keepdims=True) + EPS) * g_ref[...]
    h_ref[...] = h.astype(h_ref.dtype)
    logits = jnp.dot(h, w_ref[...], preferred_element_type=F32, precision=lax.Precision.HIGHEST)
    lane = lax.broadcasted_iota(jnp.int32, logits.shape, 1)
    logits = jnp.where(lane < N_EXPERTS, logits, -jnp.inf)
    m1 = jnp.max(logits, axis=-1, keepdims=True)
    i1 = jnp.min(jnp.where(logits == m1, lane, LANES), axis=-1, keepdims=True)
    rest = jnp.where(lane == i1, -jnp.inf, logits)
    m2 = jnp.max(rest, axis=-1, keepdims=True)
    i2 = jnp.min(jnp.where(rest == m2, lane, LANES), axis=-1, keepdims=True)
    e2 = jnp.exp(m2 - m1)
    denom = 1.0 + e2
    idx_ref[...] = jnp.where(lane == 0, i1, jnp.where(lane == 1, i2, 0))
    gate_ref[...] = jnp.where(lane == 0, 1.0 / denom, jnp.where(lane == 1, e2 / denom, 0.0))


def moe_router(x, gain, w_router, tm=256):
    n, d = x.shape
    w = jnp.zeros((d, LANES), F32).at[:, :N_EXPERTS].set(w_router.astype(F32))
    return pl.pallas_call(
        _router_kernel,
        grid=(n // tm,),
        in_specs=[pl.BlockSpec((tm, d), lambda i: (i, 0)),
                  pl.BlockSpec((1, d), lambda i: (0, 0)),
                  pl.BlockSpec((d, LANES), lambda i: (0, 0))],
        out_specs=[pl.BlockSpec((tm, d), lambda i: (i, 0)),
                   pl.BlockSpec((tm, LANES), lambda i: (i, 0)),
                   pl.BlockSpec((tm, LANES), lambda i: (i, 0))],
        out_shape=[jax.ShapeDtypeStruct((n, d), BF16),
                   jax.ShapeDtypeStruct((n, LANES), jnp.int32),
                   jax.ShapeDtypeStruct((n, LANES), F32)],
        compiler_params=_params("parallel"),
        name="moe_router",
    )(x, gain.reshape(1, d), w)


def _combine_kernel(x_ref, a_ref, b_ref, g_ref, o_ref):
    g = g_ref[...]
    o_ref[...] = x_ref[...] + (a_ref[...] * g[:, 0:1] + b_ref[...] * g[:, 1:2])


def moe_combine(x, ya, yb, gates, tm=512):
    n, d = x.shape
    row = pl.BlockSpec((tm, d), lambda i: (i, 0))
    return pl.pallas_call(
        _combine_kernel,
        grid=(n // tm,),
        in_specs=[row, row, row, pl.BlockSpec((tm, LANES), lambda i: (i, 0))],
        out_specs=row,
        out_shape=jax.ShapeDtypeStruct((n, d), F32),
        compiler_params=_params("parallel"),
        name="moe_combine",
    )(x, ya, yb, gates)


def moe_block(x, gain, w_router, w_gate, w_up, w_down, li, tm=512):
    n, d = x.shape
    h, idx, gates = moe_router(x, gain, w_router)
    flat_e = idx[:, :TOP_K].reshape(-1)
    na = n * TOP_K
    onehot = (flat_e[:, None] == jnp.arange(N_EXPERTS, dtype=jnp.int32)[None, :]).astype(jnp.int32)
    csum = jnp.cumsum(onehot, axis=0)
    rank = jnp.take_along_axis(csum, flat_e[:, None], axis=1)[:, 0] - 1
    counts = csum[-1]
    padded = (counts + tm - 1) // tm * tm
    pad_end = jnp.cumsum(padded)
    pad_start = pad_end - padded
    dest = pad_start[flat_e] + rank
    n_blocks = na // tm + N_EXPERTS
    cap = n_blocks * tm
    slot_tok = jnp.zeros((cap,), jnp.int32).at[dest].set(jnp.arange(na, dtype=jnp.int32) // TOP_K)
    block_e = jnp.minimum(
        jnp.searchsorted(pad_end, jnp.arange(n_blocks, dtype=jnp.int32) * tm, side='right'),
        N_EXPERTS - 1).astype(jnp.int32)
    xs = jnp.take(h, slot_tok, axis=0)
    act = gate_up_grouped(block_e, xs, w_gate, w_up, li, tm)
    ys = down_grouped(block_e, act, w_down, li, tm)
    dest2 = dest.reshape(n, TOP_K)
    return moe_combine(x, jnp.take(ys, dest2[:, 0], axis=0), jnp.take(ys, dest2[:, 1], axis=0), gates)


def _ple_kernel(h_ref, wg_ref, p_ref, wp_ref, x_ref, o_ref):
    gate = jax.nn.sigmoid(jnp.dot(h_ref[...], wg_ref[...].astype(BF16), preferred_element_type=F32))
    proj = jnp.dot(p_ref[...].astype(BF16), wp_ref[...].astype(BF16), preferred_element_type=F32)
    o_ref[...] = x_ref[...] + proj * gate


def ple_residual(h, w_gate, p, w_proj, li, x, tm=512, tn=512):
    n, d = x.shape
    kp = p.shape[2]
    return pl.pallas_call(
        _ple_kernel,
        grid=(d // tn, n // tm),
        in_specs=[
            pl.BlockSpec((tm, d), lambda j, i: (i, 0)),
            pl.BlockSpec((None, d, tn), lambda j, i: (li, 0, j)),
            pl.BlockSpec((None, tm, kp), lambda j, i: (li, i, 0)),
            pl.BlockSpec((None, kp, tn), lambda j, i: (li, 0, j)),
            pl.BlockSpec((tm, tn), lambda j, i: (i, j)),
        ],
        out_specs=pl.BlockSpec((tm, tn), lambda j, i: (i, j)),
        out_shape=jax.ShapeDtypeStruct((n, d), F32),
        compiler_params=_params("parallel", "parallel"),
        name="ple_residual",
    )(h, w_gate, p, w_proj, x)


def hybrid_mixer_residual(x, gain, w_in, li, conv_w, a_log, dt_bias, gdn_norm, sinks, bias, w_out, batch, seq):
    n = x.shape[0]
    h = rmsnorm_rows(x, gain, BF16)
    proj = matmul_rows_outer(h, w_in, li, IN_MAIN_W, tm=1024, tn=256)
    w_tail = jnp.zeros((1, D_MODEL, LANES), F32).at[0, :, :IN_TAIL_W].set(w_in[li, :, IN_MAIN_W:])
    tail = matmul_rows_outer(h, w_tail, 0, LANES, tm=1024, tn=LANES)
    attn = swa_attention(proj, sinks.astype(F32), bias, batch, seq)
    qkv = gdn_prep(proj, conv_w, seq)
    gates = gdn_gates(tail, a_log, dt_bias)
    g_rows = gates[:, GDN_V_HEADS:2 * GDN_V_HEADS].T.reshape(GDN_V_HEADS, n // GDN_CHUNK, GDN_CHUNK)
    gdn = gdn_mixer(qkv, proj, gates, g_rows, gdn_norm, batch, seq)
    return out_proj_residual(attn, gdn, w_out, li, x)


def kernel(x, p, w_in, conv_w, a_log, dt_bias, gdn_norm, attn_sinks, rel_bias_table, w_out, norm_mix, norm_ffn, w_dense_gate, w_dense_up, w_dense_down, w_router, w_exp_gate, w_exp_up, w_exp_down, norm_ple, w_ple_gate, w_ple_proj, norm_final):
    batch, seq, d = x.shape
    depth = w_in.shape[0]
    n = batch * seq
    out_dtype = x.dtype
    x = x.reshape(n, d).astype(F32)
    bias = relative_bias(rel_bias_table)
    p = p.reshape(depth, n, PLE_DIM)
    for i in range(depth):
        x = hybrid_mixer_residual(x, norm_mix[i], w_in, i, conv_w[i], a_log[i], dt_bias[i], gdn_norm[i],
                                  attn_sinks[i], bias, w_out, batch, seq)
        if i % 2 == 0:
            hf = rmsnorm_rows(x, norm_ffn[i], BF16)
            act = gate_up(hf, w_dense_gate, w_dense_up, i // 2)
            x = down_residual(act, w_dense_down, i // 2, x)
        else:
            x = moe_block(x, norm_ffn[i], w_router[i // 2], w_exp_gate, w_exp_up, w_exp_down, i // 2)
        hp = rmsnorm_rows(x, norm_ple[i], BF16)
        x = ple_residual(hp, w_ple_gate, p, w_ple_proj, i, x)
    return rmsnorm_rows(x, norm_final, out_dtype).reshape(batch, seq, d)
```

```python
import functools
import math

import jax
import jax.numpy as jnp
from jax import lax
from jax.experimental import pallas as pl
from jax.experimental.pallas import tpu as pltpu

F32 = jnp.float32
BF16 = jnp.bfloat16

D_MODEL = 2048
PLE_DIM = 256
EPS = 1e-6
NEG_INF = -1e30

SWA_HEADS = 16
SWA_KV_HEADS = 2
SWA_HEAD_DIM = 64
SWA_GROUP = SWA_HEADS // SWA_KV_HEADS
WINDOW = 128
SWA_BLOCK = 128
REL_BUCKETS = 32
REL_MAX_DIST = 128

GDN_QK_HEADS = 4
GDN_V_HEADS = 8
GDN_HEAD_DIM = 128
GDN_CONV = 4
GDN_TILE = 256
GDN_BLOCK = 256

SWA_Q_W = SWA_HEADS * SWA_HEAD_DIM
SWA_KV_W = SWA_KV_HEADS * SWA_HEAD_DIM
GDN_QK_W = GDN_QK_HEADS * GDN_HEAD_DIM
GDN_V_W = GDN_V_HEADS * GDN_HEAD_DIM
GDN_CONV_CH = 2 * GDN_QK_W + GDN_V_W
MIX_WIDTH = SWA_Q_W + GDN_V_W
IN_MAIN_W = SWA_Q_W + 2 * SWA_KV_W + GDN_CONV_CH + GDN_V_W
IN_TAIL_W = 2 * GDN_V_HEADS

D_FF = 7 * D_MODEL // 2
N_EXPERTS = 8
TOP_K = 2

LANES = 128
SUBLANES = 8
VMEM_LIMIT = 56 * 1024 * 1024

COL_K = SWA_Q_W // LANES
COL_V = COL_K + SWA_KV_W // LANES
COL_GDN = COL_V + SWA_KV_W // LANES
COL_Z = COL_GDN + GDN_CONV_CH // LANES


def _params(*sem):
    return pltpu.CompilerParams(dimension_semantics=sem, vmem_limit_bytes=VMEM_LIMIT)


def _rms_kernel(x_ref, g_ref, o_ref):
    x = x_ref[...]
    y = x * lax.rsqrt(jnp.mean(x * x, axis=-1, keepdims=True) + EPS)
    o_ref[...] = (y * g_ref[...]).astype(o_ref.dtype)


def rmsnorm_rows(x, gain, out_dtype, tm=512):
    n, d = x.shape
    return pl.pallas_call(
        _rms_kernel,
        grid=(n // tm,),
        in_specs=[pl.BlockSpec((tm, d), lambda i: (i, 0)), pl.BlockSpec((1, d), lambda i: (0, 0))],
        out_specs=pl.BlockSpec((tm, d), lambda i: (i, 0)),
        out_shape=jax.ShapeDtypeStruct((n, d), out_dtype),
        compiler_params=_params("parallel"),
        name="rmsnorm",
    )(x, gain.reshape(1, d))


def _mm_kernel(a_ref, w_ref, o_ref):
    o_ref[...] = jnp.dot(a_ref[...], w_ref[...], preferred_element_type=F32).astype(o_ref.dtype)


def matmul_cols_outer(a, w, li, n_out, tm, tn, out_dtype=F32):
    m, k = a.shape
    return pl.pallas_call(
        _mm_kernel,
        grid=(n_out // tn, m // tm),
        in_specs=[pl.BlockSpec((tm, k), lambda j, i: (i, 0)),
                  pl.BlockSpec((None, k, tn), lambda j, i: (li, 0, j))],
        out_specs=pl.BlockSpec((tm, tn), lambda j, i: (i, j)),
        out_shape=jax.ShapeDtypeStruct((m, n_out), out_dtype),
        compiler_params=_params("parallel", "parallel"),
        name="in_proj",
    )(a, w)


def _attn_kernel(sink_ref, q_ref, kc_ref, kp_ref, vc_ref, vp_ref, bias_ref, o_ref):
    n = pl.program_id(1)
    qi = lax.broadcasted_iota(jnp.int32, (SWA_BLOCK, 2 * SWA_BLOCK), 0)
    kj = lax.broadcasted_iota(jnp.int32, (SWA_BLOCK, 2 * SWA_BLOCK), 1)
    dist = qi + SWA_BLOCK - kj
    first_key = jnp.where(n == 0, SWA_BLOCK, 0)
    valid = (dist >= 0) & (dist < WINDOW) & (kj >= first_key)
    scale = SWA_HEAD_DIM ** -0.5
    for g in range(SWA_KV_HEADS):
        cols = slice(g * SWA_HEAD_DIM, (g + 1) * SWA_HEAD_DIM)
        kw = jnp.concatenate([kp_ref[:, cols], kc_ref[:, cols]], axis=0).astype(BF16)
        vw = jnp.concatenate([vp_ref[:, cols], vc_ref[:, cols]], axis=0).astype(BF16)
        for pair in range(SWA_GROUP // 2):
            outs = []
            for hh in range(2):
                h = g * SWA_GROUP + pair * 2 + hh
                qh = q_ref[:, h * SWA_HEAD_DIM:(h + 1) * SWA_HEAD_DIM].astype(BF16)
                logits = lax.dot_general(qh, kw, (((1,), (1,)), ((), ())), preferred_element_type=F32)
                logits = logits * scale + bias_ref[h]
                logits = jnp.where(valid, logits, NEG_INF)
                sink = sink_ref[h]
                m = jnp.maximum(jnp.max(logits, axis=-1, keepdims=True), sink)
                e = jnp.exp(logits - m)
                denom = jnp.sum(e, axis=-1, keepdims=True) + jnp.exp(sink - m)
                p = e / denom
                outs.append(jnp.dot(p.astype(BF16), vw, preferred_element_type=F32))
            h0 = g * SWA_GROUP + pair * 2
            o_ref[:, h0 * SWA_HEAD_DIM:(h0 + 2) * SWA_HEAD_DIM] = jnp.concatenate(outs, axis=-1).astype(o_ref.dtype)


def relative_bias(rel_table):
    qi = jnp.arange(SWA_BLOCK, dtype=jnp.int32)[:, None]
    kj = jnp.arange(2 * SWA_BLOCK, dtype=jnp.int32)[None, :]
    dist = qi + SWA_BLOCK - kj
    max_exact = REL_BUCKETS // 2
    d = jnp.maximum(dist, 0)
    log_ratio = jnp.log(jnp.maximum(d, 1).astype(F32) / max_exact) / math.log(REL_MAX_DIST / max_exact)
    large = jnp.minimum(max_exact + (log_ratio * (REL_BUCKETS - max_exact)).astype(jnp.int32), REL_BUCKETS - 1)
    bucket = jnp.where(d < max_exact, d, large)
    return rel_table[bucket].astype(F32).transpose(2, 0, 1)


def swa_attention(proj, sinks, bias, batch, seq):
    n = proj.shape[0]
    nb = seq // SWA_BLOCK
    cur = lambda b, i, s: b * nb + i
    prev = lambda b, i, s: jnp.maximum(b * nb + i - 1, 0)
    blk = (SWA_BLOCK, LANES)
    return pl.pallas_call(
        _attn_kernel,
        grid_spec=pltpu.PrefetchScalarGridSpec(
            num_scalar_prefetch=1,
            grid=(batch, nb),
            in_specs=[
                pl.BlockSpec((SWA_BLOCK, SWA_Q_W), lambda b, i, s: (cur(b, i, s), 0)),
                pl.BlockSpec(blk, lambda b, i, s: (cur(b, i, s), COL_K)),
                pl.BlockSpec(blk, lambda b, i, s: (prev(b, i, s), COL_K)),
                pl.BlockSpec(blk, lambda b, i, s: (cur(b, i, s), COL_V)),
                pl.BlockSpec(blk, lambda b, i, s: (prev(b, i, s), COL_V)),
                pl.BlockSpec((SWA_HEADS, SWA_BLOCK, 2 * SWA_BLOCK), lambda b, i, s: (0, 0, 0)),
            ],
            out_specs=pl.BlockSpec((SWA_BLOCK, SWA_Q_W), lambda b, i, s: (cur(b, i, s), 0)),
        ),
        out_shape=jax.ShapeDtypeStruct((n, SWA_Q_W), BF16),
        compiler_params=_params("parallel", "parallel"),
        name="swa_attention",
    )(sinks, proj, proj, proj, proj, proj, bias)


def _gdn_prep_kernel(cur_ref, prev_ref, w_ref, o_ref, *, tiles_per_seq):
    i = pl.program_id(0)
    c = pl.program_id(1)
    tm = cur_ref.shape[0]
    cur = cur_ref[...]
    prev = jnp.where(i % tiles_per_seq == 0, 0.0, prev_ref[...])
    xp = jnp.concatenate([prev, cur], axis=0)
    w = w_ref[...]
    acc = xp[SUBLANES - 3:SUBLANES - 3 + tm] * w[0:1]
    for k in range(1, GDN_CONV):
        off = SUBLANES - (GDN_CONV - 1) + k
        acc = acc + xp[off:off + tm] * w[k:k + 1]
    y = acc * jax.nn.sigmoid(acc)
    d = GDN_HEAD_DIM
    normed = jnp.concatenate(
        [y[:, h * d:(h + 1) * d] * lax.rsqrt(jnp.sum(y[:, h * d:(h + 1) * d] ** 2, axis=-1, keepdims=True) + EPS)
         for h in range(y.shape[1] // d)], axis=1)
    qk_tiles = 2 * GDN_QK_W // y.shape[1]
    o_ref[...] = jnp.where(c < qk_tiles, normed, y)


def gdn_prep(proj, conv_w, seq, tm=1024, tc=256):
    n = proj.shape[0]
    col0 = COL_GDN * LANES // tc
    return pl.pallas_call(
        functools.partial(_gdn_prep_kernel, tiles_per_seq=seq // tm),
        grid=(n // tm, GDN_CONV_CH // tc),
        in_specs=[
            pl.BlockSpec((tm, tc), lambda i, c: (i, col0 + c)),
            pl.BlockSpec((SUBLANES, tc), lambda i, c: (jnp.maximum(i * (tm // SUBLANES) - 1, 0), col0 + c)),
            pl.BlockSpec((GDN_CONV, tc), lambda i, c: (0, c)),
        ],
        out_specs=pl.BlockSpec((tm, tc), lambda i, c: (i, c)),
        out_shape=jax.ShapeDtypeStruct((n, GDN_CONV_CH), F32),
        compiler_params=_params("parallel", "parallel"),
        name="gdn_conv_norm",
    )(proj, proj, conv_w)


def _gdn_gate_kernel(h_ref, w_ref, alog_ref, dtb_ref, o_ref):
    tm = h_ref.shape[0]
    t = jnp.dot(h_ref[...], w_ref[...], preferred_element_type=F32)
    lane = lax.broadcasted_iota(jnp.int32, t.shape, 1)
    beta = jax.nn.sigmoid(t)
    g = -jnp.exp(alog_ref[...]) * jax.nn.softplus(t + dtb_ref[...])
    g = jnp.where((lane >= GDN_V_HEADS) & (lane < 2 * GDN_V_HEADS), g, 0.0)
    r = lax.broadcasted_iota(jnp.int32, (tm, tm), 0)
    s = lax.broadcasted_iota(jnp.int32, (tm, tm), 1)
    tri = jnp.where((s <= r) & (r // GDN_BLOCK == s // GDN_BLOCK), 1.0, 0.0).astype(F32)
    gcum = jnp.dot(tri, g, preferred_element_type=F32, precision=lax.Precision.HIGHEST)
    o_ref[...] = jnp.where(lane < GDN_V_HEADS, beta, gcum)


def gdn_gates(h, w_tail, a_log, dt_bias, tm=512):
    n, k = h.shape
    pad = lambda v: jnp.zeros((1, LANES), F32).at[0, GDN_V_HEADS:2 * GDN_V_HEADS].set(v.astype(F32))
    return pl.pallas_call(
        _gdn_gate_kernel,
        grid=(n // tm,),
        in_specs=[pl.BlockSpec((tm, k), lambda i: (i, 0)),
                  pl.BlockSpec((k, LANES), lambda i: (0, 0)),
                  pl.BlockSpec((1, LANES), lambda i: (0, 0)),
                  pl.BlockSpec((1, LANES), lambda i: (0, 0))],
        out_specs=pl.BlockSpec((tm, LANES), lambda i: (i, 0)),
        out_shape=jax.ShapeDtypeStruct((n, LANES), F32),
        compiler_params=_params("parallel"),
        name="gdn_gates",
    )(h, w_tail, pad(a_log), pad(dt_bias))


def _bdot(a, b, dims=(((1,), (0,)), ((), ()))):
    return lax.dot_general(a.astype(BF16), b.astype(BF16), dims, preferred_element_type=F32)


_NT = (((1,), (1,)), ((), ()))
_TN = (((0,), (0,)), ((), ()))


def _unit_lower_inverses(lows, block):
    r_dim = lows[0].shape[0]
    r = lax.broadcasted_iota(jnp.int32, (r_dim, r_dim), 0)
    s = lax.broadcasted_iota(jnp.int32, (r_dim, r_dim), 1)
    size = 1
    xs = None
    while size < block:
        sel = (r // (2 * size) == s // (2 * size)) & ((r // size) % 2 == 1) & ((s // size) % 2 == 0)
        offs = [jnp.where(sel, low, 0.0) for low in lows]
        if size == 1:
            eye = jnp.where(r == s, 1.0, 0.0).astype(F32)
            xs = [eye - off for off in offs]
        else:
            xo = [_bdot(x, off) for x, off in zip(xs, offs)]
            xs = [x - _bdot(y, x) for x, y in zip(xs, xo)]
        size *= 2
    return xs


def _gdn_kernel(q_ref, k_ref, v_ref, z_ref, gb_ref, grow_ref, norm_ref, o_ref,
                state_ref, w_s, u_s, qd_s, kd_s, qk_s):
    hp = pl.program_id(1)
    t = pl.program_id(2)
    tb = q_ref.shape[0]
    rt = GDN_TILE
    blk = GDN_BLOCK
    pair = GDN_V_HEADS // GDN_QK_HEADS
    d = GDN_HEAD_DIM

    @pl.when(t == 0)
    def _():
        state_ref[...] = jnp.zeros_like(state_ref)

    lane = lax.broadcasted_iota(jnp.int32, (tb, LANES), 1)
    gb = gb_ref[...]
    r = lax.broadcasted_iota(jnp.int32, (rt, rt), 0)
    s = lax.broadcasted_iota(jnp.int32, (rt, rt), 1)
    same = (r // blk) == (s // blk)
    causal = (s <= r) & same
    strict = (s < r) & same

    g_heads = []
    problems = []
    for hh in range(pair):
        h = pair * hp + hh
        beta_all = jnp.sum(jnp.where(lane == h, gb, 0.0), axis=-1, keepdims=True)
        g_all = jnp.sum(jnp.where(lane == h + GDN_V_HEADS, gb, 0.0), axis=-1, keepdims=True)
        g_heads.append(g_all)
        for ti in range(tb // rt):
            rows = slice(ti * rt, (ti + 1) * rt)
            q = q_ref[rows, :] * (d ** -0.5)
            k = k_ref[rows, :]
            v = v_ref[rows, hh * d:(hh + 1) * d]
            beta = beta_all[rows]
            g = g_all[rows]
            g_row = grow_ref[ti, hh:hh + 1, :]
            decay = jnp.where(causal, jnp.exp(jnp.where(causal, g - g_row, 0.0)), 0.0)
            kb = k * beta
            eg = jnp.exp(g)
            kq = _bdot(jnp.concatenate([kb, q], axis=0), k, _NT)
            low = jnp.where(strict, kq[:rt] * decay, 0.0)
            qk_s[hh, rows, :] = (kq[rt:] * decay).astype(BF16)
            g_last = jnp.concatenate(
                [jnp.broadcast_to(g[(c + 1) * blk - 1:(c + 1) * blk], (blk, 1)) for c in range(rt // blk)], axis=0)
            qd_s[hh, rows, :] = (q * eg).astype(BF16)
            kd_s[hh, rows, :] = (k * jnp.exp(g_last - g)).astype(BF16)
            rhs = jnp.concatenate([v * beta, kb * eg], axis=1).astype(BF16)
            problems.append((hh, rows, low, rhs))

    tinvs = _unit_lower_inverses([p[2] for p in problems], blk)
    for (hh, rows, _, rhs), tinv in zip(problems, tinvs):
        uw = _bdot(tinv, rhs)
        u_s[hh, rows, :] = uw[:, :d]
        w_s[hh, rows, :] = uw[:, d:].astype(BF16)

    for c in range(tb // blk):
        rows = slice(c * blk, (c + 1) * blk)
        off = (c * blk) % rt
        for hh in range(pair):
            state = state_ref[hh]
            ws = _bdot(jnp.concatenate([w_s[hh, rows, :], qd_s[hh, rows, :]], axis=0), state)
            v_new = u_s[hh, rows, :] - ws[:blk]
            o = ws[blk:] + _bdot(qk_s[hh, rows, off:off + blk], v_new)
            g_end = g_heads[hh][(c + 1) * blk - 1:(c + 1) * blk]
            state_ref[hh] = state * jnp.exp(g_end) + _bdot(kd_s[hh, rows, :], v_new, _TN)
            o = o * lax.rsqrt(jnp.mean(o * o, axis=-1, keepdims=True) + EPS) * norm_ref[...]
            zc = z_ref[rows, hh * d:(hh + 1) * d]
            o_ref[rows, hh * d:(hh + 1) * d] = (o * (zc * jax.nn.sigmoid(zc))).astype(o_ref.dtype)


def gdn_mixer(qkv, proj, gates, g_rows, gdn_norm, batch, seq, tb=512):
    n = qkv.shape[0]
    nt = seq // tb
    pair = GDN_V_HEADS // GDN_QK_HEADS
    d = GDN_HEAD_DIM
    row = lambda b, hp, t: b * nt + t
    wide = (tb, pair * d)
    v_col = 2 * GDN_QK_HEADS // pair
    z_col = COL_Z // pair
    return pl.pallas_call(
        _gdn_kernel,
        grid=(batch, GDN_QK_HEADS, nt),
        in_specs=[
            pl.BlockSpec((tb, d), lambda b, hp, t: (row(b, hp, t), hp)),
            pl.BlockSpec((tb, d), lambda b, hp, t: (row(b, hp, t), GDN_QK_HEADS + hp)),
            pl.BlockSpec(wide, lambda b, hp, t: (row(b, hp, t), v_col + hp)),
            pl.BlockSpec(wide, lambda b, hp, t: (row(b, hp, t), z_col + hp)),
            pl.BlockSpec((tb, LANES), lambda b, hp, t: (row(b, hp, t), 0)),
            pl.BlockSpec((None, tb // GDN_TILE, pair, GDN_TILE), lambda b, hp, t: (hp, row(b, hp, t), 0, 0)),
            pl.BlockSpec((1, LANES), lambda b, hp, t: (0, 0)),
        ],
        out_specs=pl.BlockSpec(wide, lambda b, hp, t: (row(b, hp, t), hp)),
        out_shape=jax.ShapeDtypeStruct((n, GDN_V_W), BF16),
        scratch_shapes=[
            pltpu.VMEM((pair, d, d), F32),
            pltpu.VMEM((pair, tb, d), BF16),
            pltpu.VMEM((pair, tb, d), F32),
            pltpu.VMEM((pair, tb, d), BF16),
            pltpu.VMEM((pair, tb, d), BF16),
            pltpu.VMEM((pair, tb, GDN_TILE), BF16),
        ],
        compiler_params=_params("parallel", "parallel", "arbitrary"),
        name="gated_delta_rule",
    )(qkv, qkv, qkv, proj, gates, g_rows, gdn_norm.reshape(1, LANES).astype(F32))


def _rms(x, gain):
    return x * lax.rsqrt(jnp.mean(x * x, axis=-1, keepdims=True) + EPS) * gain


def _out_proj_kernel(a_ref, b_ref, w_ref, x_ref, g_ref, o_ref, h_ref):
    ka = a_ref.shape[1]
    acc = jnp.dot(a_ref[...], w_ref[:ka, :], preferred_element_type=F32)
    acc += jnp.dot(b_ref[...], w_ref[ka:, :], preferred_element_type=F32)
    x = x_ref[...] + acc
    o_ref[...] = x
    h_ref[...] = _rms(x, g_ref[...]).astype(h_ref.dtype)


def out_proj_residual(attn, gdn, w_out, li, x, next_gain, tm=512):
    n, d = x.shape
    ka, kb = attn.shape[1], gdn.shape[1]
    row = lambda w: pl.BlockSpec((tm, w), lambda i: (i, 0))
    return pl.pallas_call(
        _out_proj_kernel,
        grid=(n // tm,),
        in_specs=[row(ka), row(kb),
                  pl.BlockSpec((None, ka + kb, d), lambda i: (li, 0, 0)),
                  row(d),
                  pl.BlockSpec((1, d), lambda i: (0, 0))],
        out_specs=[row(d), row(d)],
        out_shape=[jax.ShapeDtypeStruct((n, d), F32), jax.ShapeDtypeStruct((n, d), BF16)],
        compiler_params=_params("parallel"),
        name="out_proj_residual",
    )(attn, gdn, w_out, x, next_gain.reshape(1, d))


def _swiglu_tile(h, wg, wu):
    g = jnp.dot(h, wg, preferred_element_type=F32)
    u = jnp.dot(h, wu, preferred_element_type=F32)
    return g * jax.nn.sigmoid(g) * u


def _gate_up_kernel(h_ref, wg_ref, wu_ref, o_ref):
    o_ref[...] = _swiglu_tile(h_ref[...], wg_ref[...], wu_ref[...]).astype(o_ref.dtype)


def gate_up(h, w_gate, w_up, li, tm=512, tn=1024):
    m, k = h.shape
    f = w_gate.shape[2]
    return pl.pallas_call(
        _gate_up_kernel,
        grid=(f // tn, m // tm),
        in_specs=[
            pl.BlockSpec((tm, k), lambda j, i: (i, 0)),
            pl.BlockSpec((None, k, tn), lambda j, i: (li, 0, j)),
            pl.BlockSpec((None, k, tn), lambda j, i: (li, 0, j)),
        ],
        out_specs=pl.BlockSpec((tm, tn), lambda j, i: (i, j)),
        out_shape=jax.ShapeDtypeStruct((m, f), BF16),
        compiler_params=_params("parallel", "parallel"),
        name="swiglu_gate_up",
    )(h, w_gate, w_up)


def _gate_up_grouped_kernel(meta_ref, h_ref, wg_ref, wu_ref, o_ref):
    i = pl.program_id(1)
    used = meta_ref[pl.num_programs(1)]

    @pl.when(i < used)
    def _():
        o_ref[...] = _swiglu_tile(h_ref[...], wg_ref[...].astype(BF16), wu_ref[...].astype(BF16)).astype(o_ref.dtype)

    @pl.when(i >= used)
    def _():
        o_ref[...] = jnp.zeros_like(o_ref)


def gate_up_grouped(meta, h, w_gate, w_up, li, tm, tn=512):
    m, k = h.shape
    f = w_gate.shape[3]
    nb = m // tm
    row = lambda j, i, meta: (jnp.minimum(i, meta[nb] - 1), 0)
    wmap = lambda j, i, meta: (li, meta[i], 0, j)
    return pl.pallas_call(
        _gate_up_grouped_kernel,
        grid_spec=pltpu.PrefetchScalarGridSpec(
            num_scalar_prefetch=1,
            grid=(f // tn, nb),
            in_specs=[
                pl.BlockSpec((tm, k), row),
                pl.BlockSpec((None, None, k, tn), wmap),
                pl.BlockSpec((None, None, k, tn), wmap),
            ],
            out_specs=pl.BlockSpec((tm, tn), lambda j, i, meta: (i, j)),
        ),
        out_shape=jax.ShapeDtypeStruct((m, f), BF16),
        compiler_params=_params("parallel", "arbitrary"),
        name="moe_gate_up",
    )(meta, h, w_gate, w_up)


def _down_kernel(a_ref, w_ref, x_ref, o_ref):
    o_ref[...] = x_ref[...] + jnp.dot(a_ref[...], w_ref[...], preferred_element_type=F32)


def down_residual(a, w_down, li, x, tm=512, tn=512):
    n, d = x.shape
    f = a.shape[1]
    return pl.pallas_call(
        _down_kernel,
        grid=(d // tn, n // tm),
        in_specs=[
            pl.BlockSpec((tm, f), lambda j, i: (i, 0)),
            pl.BlockSpec((None, f, tn), lambda j, i: (li, 0, j)),
            pl.BlockSpec((tm, tn), lambda j, i: (i, j)),
        ],
        out_specs=pl.BlockSpec((tm, tn), lambda j, i: (i, j)),
        out_shape=jax.ShapeDtypeStruct((n, d), F32),
        compiler_params=_params("parallel", "parallel"),
        name="swiglu_down_residual",
    )(a, w_down, x)


def _down_grouped_kernel(meta_ref, a_ref, w_ref, o_ref):
    i = pl.program_id(1)
    used = meta_ref[pl.num_programs(1)]

    @pl.when(i < used)
    def _():
        o_ref[...] = jnp.dot(a_ref[...], w_ref[...].astype(BF16), preferred_element_type=F32)

    @pl.when(i >= used)
    def _():
        o_ref[...] = jnp.zeros_like(o_ref)


def down_grouped(meta, a, w_down, li, tm, tn=512):
    m, f = a.shape
    d = w_down.shape[3]
    nb = m // tm
    return pl.pallas_call(
        _down_grouped_kernel,
        grid_spec=pltpu.PrefetchScalarGridSpec(
            num_scalar_prefetch=1,
            grid=(d // tn, nb),
            in_specs=[
                pl.BlockSpec((tm, f), lambda j, i, meta: (jnp.minimum(i, meta[nb] - 1), 0)),
                pl.BlockSpec((None, None, f, tn), lambda j, i, meta: (li, meta[i], 0, j)),
            ],
            out_specs=pl.BlockSpec((tm, tn), lambda j, i, meta: (i, j)),
        ),
        out_shape=jax.ShapeDtypeStruct((m, d), F32),
        compiler_params=_params("parallel", "arbitrary"),
        name="moe_down",
    )(meta, a, w_down)


def _router_kernel(x_ref, g_ref, w_ref, h_ref, idx_ref, gate_ref):
    x = x_ref[...]
    h = x * lax.rsqrt(jnp.mean(x * x, axis=-1, keepdims=True) + EPS) * g_ref[...]
    h_ref[...] = h.astype(h_ref.dtype)
    logits = jnp.dot(h, w_ref[...], preferred_element_type=F32, precision=lax.Precision.HIGHEST)
    lane = lax.broadcasted_iota(jnp.int32, logits.shape, 1)
    logits = jnp.where(lane < N_EXPERTS, logits, -jnp.inf)
    m1 = jnp.max(logits, axis=-1, keepdims=True)
    i1 = jnp.min(jnp.where(logits == m1, lane, LANES), axis=-1, keepdims=True)
    rest = jnp.where(lane == i1, -jnp.inf, logits)
    m2 = jnp.max(rest, axis=-1, keepdims=True)
    i2 = jnp.min(jnp.where(rest == m2, lane, LANES), axis=-1, keepdims=True)
    e2 = jnp.exp(m2 - m1)
    denom = 1.0 + e2
    idx_ref[...] = jnp.where(lane == 0, i1, jnp.where(lane == 1, i2, 0))
    gate_ref[...] = jnp.where(lane == 0, 1.0 / denom, jnp.where(lane == 1, e2 / denom, 0.0))


def moe_router(x, gain, w_router, tm=256):
    n, d = x.shape
    w = jnp.zeros((d, LANES), F32).at[:, :N_EXPERTS].set(w_router.astype(F32))
    return pl.pallas_call(
        _router_kernel,
        grid=(n // tm,),
        in_specs=[pl.BlockSpec((tm, d), lambda i: (i, 0)),
                  pl.BlockSpec((1, d), lambda i: (0, 0)),
                  pl.BlockSpec((d, LANES), lambda i: (0, 0))],
        out_specs=[pl.BlockSpec((tm, d), lambda i: (i, 0)),
                   pl.BlockSpec((tm, LANES), lambda i: (i, 0)),
                   pl.BlockSpec((tm, LANES), lambda i: (i, 0))],
        out_shape=[jax.ShapeDtypeStruct((n, d), BF16),
                   jax.ShapeDtypeStruct((n, LANES), jnp.int32),
                   jax.ShapeDtypeStruct((n, LANES), F32)],
        compiler_params=_params("parallel"),
        name="moe_router",
    )(x, gain.reshape(1, d), w)


def _combine_kernel(x_ref, a_ref, b_ref, g_ref, gain_ref, o_ref, h_ref):
    g = g_ref[...]
    x = x_ref[...] + (a_ref[...] * g[:, 0:1] + b_ref[...] * g[:, 1:2])
    o_ref[...] = x
    h_ref[...] = _rms(x, gain_ref[...]).astype(h_ref.dtype)


def moe_combine(x, ya, yb, gates, next_gain, tm=512):
    n, d = x.shape
    row = pl.BlockSpec((tm, d), lambda i: (i, 0))
    return pl.pallas_call(
        _combine_kernel,
        grid=(n // tm,),
        in_specs=[row, row, row, pl.BlockSpec((tm, LANES), lambda i: (i, 0)), pl.BlockSpec((1, d), lambda i: (0, 0))],
        out_specs=[row, row],
        out_shape=[jax.ShapeDtypeStruct((n, d), F32), jax.ShapeDtypeStruct((n, d), BF16)],
        compiler_params=_params("parallel"),
        name="moe_combine",
    )(x, ya, yb, gates, next_gain.reshape(1, d))


def moe_block(x, gain, w_router, w_gate, w_up, w_down, li, next_gain, tm=512):
    n, d = x.shape
    h, idx, gates = moe_router(x, gain, w_router)
    flat_e = idx[:, :TOP_K].reshape(-1)
    na = n * TOP_K
    onehot = (flat_e[:, None] == jnp.arange(N_EXPERTS, dtype=jnp.int32)[None, :]).astype(jnp.int32)
    csum = jnp.cumsum(onehot, axis=0)
    rank = jnp.take_along_axis(csum, flat_e[:, None], axis=1)[:, 0] - 1
    counts = csum[-1]
    padded = (counts + tm - 1) // tm * tm
    pad_end = jnp.cumsum(padded)
    pad_start = pad_end - padded
    dest = pad_start[flat_e] + rank
    n_blocks = na // tm + N_EXPERTS
    cap = n_blocks * tm
    slot_tok = jnp.zeros((cap,), jnp.int32).at[dest].set(jnp.arange(na, dtype=jnp.int32) // TOP_K)
    block_e = jnp.minimum(
        jnp.searchsorted(pad_end, jnp.arange(n_blocks, dtype=jnp.int32) * tm, side='right'),
        N_EXPERTS - 1).astype(jnp.int32)
    meta = jnp.concatenate([block_e, (pad_end[-1:] // tm).astype(jnp.int32)])
    xs = jnp.take(h, slot_tok, axis=0)
    act = gate_up_grouped(meta, xs, w_gate, w_up, li, tm)
    ys = down_grouped(meta, act, w_down, li, tm)
    dest2 = dest.reshape(n, TOP_K)
    return moe_combine(x, jnp.take(ys, dest2[:, 0], axis=0), jnp.take(ys, dest2[:, 1], axis=0), gates, next_gain)


def _ple_kernel(h_ref, wg_ref, p_ref, wp_ref, x_ref, gain_ref, o_ref, hn_ref):
    gate = jax.nn.sigmoid(jnp.dot(h_ref[...], wg_ref[...], preferred_element_type=F32))
    proj = jnp.dot(p_ref[...].astype(BF16), wp_ref[...].astype(BF16), preferred_element_type=F32)
    x = x_ref[...] + proj * gate
    o_ref[...] = x
    hn_ref[...] = _rms(x, gain_ref[...]).astype(hn_ref.dtype)


def ple_residual(h, w_gate, p, w_proj, li, x, next_gain, next_dtype, tm=512):
    n, d = x.shape
    kp = p.shape[2]
    row = lambda w: pl.BlockSpec((tm, w), lambda i: (i, 0))
    return pl.pallas_call(
        _ple_kernel,
        grid=(n // tm,),
        in_specs=[
            row(d),
            pl.BlockSpec((None, d, d), lambda i: (li, 0, 0)),
            pl.BlockSpec((None, tm, kp), lambda i: (li, i, 0)),
            pl.BlockSpec((None, kp, d), lambda i: (li, 0, 0)),
            row(d),
            pl.BlockSpec((1, d), lambda i: (0, 0)),
        ],
        out_specs=[row(d), row(d)],
        out_shape=[jax.ShapeDtypeStruct((n, d), F32), jax.ShapeDtypeStruct((n, d), next_dtype)],
        compiler_params=_params("parallel"),
        name="ple_residual",
    )(h, w_gate, p, w_proj, x, next_gain.reshape(1, d))


def hybrid_mixer_residual(x, h, w_in, li, conv_w, a_log, dt_bias, gdn_norm, sinks, bias, w_out, next_gain,
                          batch, seq):
    n = x.shape[0]
    proj = matmul_cols_outer(h, w_in, li, IN_MAIN_W, tm=512, tn=IN_MAIN_W // 2)
    w_tail = jnp.zeros((D_MODEL, LANES), BF16).at[:, :IN_TAIL_W].set(w_in[li, :, IN_MAIN_W:])
    attn = swa_attention(proj, sinks.astype(F32), bias, batch, seq)
    qkv = gdn_prep(proj, conv_w, seq)
    gates = gdn_gates(h, w_tail, a_log, dt_bias)
    pair = GDN_V_HEADS // GDN_QK_HEADS
    g_rows = gates[:, GDN_V_HEADS:2 * GDN_V_HEADS].T.reshape(GDN_QK_HEADS, pair, n // GDN_TILE, GDN_TILE)
    g_rows = g_rows.transpose(0, 2, 1, 3)
    gdn = gdn_mixer(qkv, proj, gates, g_rows, gdn_norm, batch, seq)
    return out_proj_residual(attn, gdn, w_out, li, x, next_gain)


def kernel(x, p, w_in, conv_w, a_log, dt_bias, gdn_norm, attn_sinks, rel_bias_table, w_out, norm_mix, norm_ffn, w_dense_gate, w_dense_up, w_dense_down, w_router, w_exp_gate, w_exp_up, w_exp_down, norm_ple, w_ple_gate, w_ple_proj, norm_final):
    batch, seq, d = x.shape
    depth = w_in.shape[0]
    n = batch * seq
    out_dtype = x.dtype
    x = x.reshape(n, d).astype(F32)
    bias = relative_bias(rel_bias_table)
    p = p.reshape(depth, n, PLE_DIM)
    w_in, w_out, w_ple_gate = w_in.astype(BF16), w_out.astype(BF16), w_ple_gate.astype(BF16)
    w_dense_gate, w_dense_up, w_dense_down = (w.astype(BF16) for w in (w_dense_gate, w_dense_up, w_dense_down))
    h = rmsnorm_rows(x, norm_mix[0], BF16)
    for i in range(depth):
        x, hf = hybrid_mixer_residual(x, h, w_in, i, conv_w[i], a_log[i], dt_bias[i], gdn_norm[i],
                                      attn_sinks[i], bias, w_out, norm_ffn[i], batch, seq)
        if i % 2 == 0:
            act = gate_up(hf, w_dense_gate, w_dense_up, i // 2)
            x = down_residual(act, w_dense_down, i // 2, x)
            hp = rmsnorm_rows(x, norm_ple[i], BF16)
        else:
            x, hp = moe_block(x, norm_ffn[i], w_router[i // 2], w_exp_gate, w_exp_up, w_exp_down, i // 2,
                              norm_ple[i])
        last = i == depth - 1
        x, h = ple_residual(hp, w_ple_gate, p, w_ple_proj, i, x, norm_final if last else norm_mix[i + 1],
                            out_dtype if last else BF16)
    return h.reshape(batch, seq, d)
```

```python
import functools
import math

import jax
import jax.numpy as jnp
from jax import lax
from jax.experimental import pallas as pl
from jax.experimental.pallas import tpu as pltpu

F32 = jnp.float32
BF16 = jnp.bfloat16

D_MODEL = 2048
PLE_DIM = 256
EPS = 1e-6
NEG_INF = -1e30

SWA_HEADS = 16
SWA_KV_HEADS = 2
SWA_HEAD_DIM = 64
SWA_GROUP = SWA_HEADS // SWA_KV_HEADS
WINDOW = 128
SWA_BLOCK = 128
REL_BUCKETS = 32
REL_MAX_DIST = 128

GDN_QK_HEADS = 4
GDN_V_HEADS = 8
GDN_HEAD_DIM = 128
GDN_CONV = 4
GDN_TILE = 256
GDN_BLOCK = 256

SWA_Q_W = SWA_HEADS * SWA_HEAD_DIM
SWA_KV_W = SWA_KV_HEADS * SWA_HEAD_DIM
GDN_QK_W = GDN_QK_HEADS * GDN_HEAD_DIM
GDN_V_W = GDN_V_HEADS * GDN_HEAD_DIM
GDN_CONV_CH = 2 * GDN_QK_W + GDN_V_W
MIX_WIDTH = SWA_Q_W + GDN_V_W
IN_MAIN_W = SWA_Q_W + 2 * SWA_KV_W + GDN_CONV_CH + GDN_V_W
IN_TAIL_W = 2 * GDN_V_HEADS

D_FF = 7 * D_MODEL // 2
N_EXPERTS = 8
TOP_K = 2

LANES = 128
SUBLANES = 8
VMEM_LIMIT = 56 * 1024 * 1024

COL_K = SWA_Q_W // LANES
COL_V = COL_K + SWA_KV_W // LANES
COL_GDN = COL_V + SWA_KV_W // LANES
COL_Z = COL_GDN + GDN_CONV_CH // LANES


def _params(*sem):
    return pltpu.CompilerParams(dimension_semantics=sem, vmem_limit_bytes=VMEM_LIMIT)


def _rms_kernel(x_ref, g_ref, o_ref):
    x = x_ref[...]
    y = x * lax.rsqrt(jnp.mean(x * x, axis=-1, keepdims=True) + EPS)
    o_ref[...] = (y * g_ref[...]).astype(o_ref.dtype)


def rmsnorm_rows(x, gain, out_dtype, tm=512):
    n, d = x.shape
    return pl.pallas_call(
        _rms_kernel,
        grid=(n // tm,),
        in_specs=[pl.BlockSpec((tm, d), lambda i: (i, 0)), pl.BlockSpec((1, d), lambda i: (0, 0))],
        out_specs=pl.BlockSpec((tm, d), lambda i: (i, 0)),
        out_shape=jax.ShapeDtypeStruct((n, d), out_dtype),
        compiler_params=_params("parallel"),
        name="rmsnorm",
    )(x, gain.reshape(1, d))


def _mm_kernel(a_ref, w_ref, o_ref):
    o_ref[...] = jnp.dot(a_ref[...], w_ref[...], preferred_element_type=F32).astype(o_ref.dtype)


def matmul_cols_outer(a, w, li, n_out, tm, tn, out_dtype=F32):
    m, k = a.shape
    return pl.pallas_call(
        _mm_kernel,
        grid=(n_out // tn, m // tm),
        in_specs=[pl.BlockSpec((tm, k), lambda j, i: (i, 0)),
                  pl.BlockSpec((None, k, tn), lambda j, i: (li, 0, j))],
        out_specs=pl.BlockSpec((tm, tn), lambda j, i: (i, j)),
        out_shape=jax.ShapeDtypeStruct((m, n_out), out_dtype),
        compiler_params=_params("parallel", "parallel"),
        name="in_proj",
    )(a, w)


def _attn_kernel(sink_ref, q_ref, kc_ref, kp_ref, vc_ref, vp_ref, bias_ref, o_ref):
    n = pl.program_id(1)
    qi = lax.broadcasted_iota(jnp.int32, (SWA_BLOCK, 2 * SWA_BLOCK), 0)
    kj = lax.broadcasted_iota(jnp.int32, (SWA_BLOCK, 2 * SWA_BLOCK), 1)
    dist = qi + SWA_BLOCK - kj
    first_key = jnp.where(n == 0, SWA_BLOCK, 0)
    valid = (dist >= 0) & (dist < WINDOW) & (kj >= first_key)
    scale = SWA_HEAD_DIM ** -0.5
    for g in range(SWA_KV_HEADS):
        cols = slice(g * SWA_HEAD_DIM, (g + 1) * SWA_HEAD_DIM)
        kw = jnp.concatenate([kp_ref[:, cols], kc_ref[:, cols]], axis=0).astype(BF16)
        vw = jnp.concatenate([vp_ref[:, cols], vc_ref[:, cols]], axis=0).astype(BF16)
        for pair in range(SWA_GROUP // 2):
            outs = []
            for hh in range(2):
                h = g * SWA_GROUP + pair * 2 + hh
                qh = (q_ref[:, h * SWA_HEAD_DIM:(h + 1) * SWA_HEAD_DIM] * scale).astype(BF16)
                logits = lax.dot_general(qh, kw, (((1,), (1,)), ((), ())), preferred_element_type=F32)
                logits = jnp.where(valid, logits + bias_ref[h], NEG_INF)
                sink = sink_ref[h]
                m = jnp.maximum(jnp.max(logits, axis=-1, keepdims=True), sink)
                e = jnp.exp(logits - m)
                denom = jnp.sum(e, axis=-1, keepdims=True) + jnp.exp(sink - m)
                outs.append(jnp.dot(e.astype(BF16), vw, preferred_element_type=F32) * (1.0 / denom))
            h0 = g * SWA_GROUP + pair * 2
            o_ref[:, h0 * SWA_HEAD_DIM:(h0 + 2) * SWA_HEAD_DIM] = jnp.concatenate(outs, axis=-1).astype(o_ref.dtype)


def relative_bias(rel_table):
    qi = jnp.arange(SWA_BLOCK, dtype=jnp.int32)[:, None]
    kj = jnp.arange(2 * SWA_BLOCK, dtype=jnp.int32)[None, :]
    dist = qi + SWA_BLOCK - kj
    max_exact = REL_BUCKETS // 2
    d = jnp.maximum(dist, 0)
    log_ratio = jnp.log(jnp.maximum(d, 1).astype(F32) / max_exact) / math.log(REL_MAX_DIST / max_exact)
    large = jnp.minimum(max_exact + (log_ratio * (REL_BUCKETS - max_exact)).astype(jnp.int32), REL_BUCKETS - 1)
    bucket = jnp.where(d < max_exact, d, large)
    return rel_table[bucket].astype(F32).transpose(2, 0, 1)


def swa_attention(proj, sinks, bias, batch, seq):
    n = proj.shape[0]
    nb = seq // SWA_BLOCK
    cur = lambda b, i, s: b * nb + i
    prev = lambda b, i, s: jnp.maximum(b * nb + i - 1, 0)
    blk = (SWA_BLOCK, LANES)
    return pl.pallas_call(
        _attn_kernel,
        grid_spec=pltpu.PrefetchScalarGridSpec(
            num_scalar_prefetch=1,
            grid=(batch, nb),
            in_specs=[
                pl.BlockSpec((SWA_BLOCK, SWA_Q_W), lambda b, i, s: (cur(b, i, s), 0)),
                pl.BlockSpec(blk, lambda b, i, s: (cur(b, i, s), COL_K)),
                pl.BlockSpec(blk, lambda b, i, s: (prev(b, i, s), COL_K)),
                pl.BlockSpec(blk, lambda b, i, s: (cur(b, i, s), COL_V)),
                pl.BlockSpec(blk, lambda b, i, s: (prev(b, i, s), COL_V)),
                pl.BlockSpec((SWA_HEADS, SWA_BLOCK, 2 * SWA_BLOCK), lambda b, i, s: (0, 0, 0)),
            ],
            out_specs=pl.BlockSpec((SWA_BLOCK, SWA_Q_W), lambda b, i, s: (cur(b, i, s), 0)),
        ),
        out_shape=jax.ShapeDtypeStruct((n, SWA_Q_W), BF16),
        compiler_params=_params("parallel", "parallel"),
        name="swa_attention",
    )(sinks, proj, proj, proj, proj, proj, bias)


def _gdn_prep_kernel(cur_ref, prev_ref, w_ref, o_ref, *, tiles_per_seq):
    i = pl.program_id(0)
    c = pl.program_id(1)
    tm = cur_ref.shape[0]
    cur = cur_ref[...]
    prev = jnp.where(i % tiles_per_seq == 0, 0.0, prev_ref[...])
    xp = jnp.concatenate([prev, cur], axis=0)
    w = w_ref[...]
    acc = xp[SUBLANES - 3:SUBLANES - 3 + tm] * w[0:1]
    for k in range(1, GDN_CONV):
        off = SUBLANES - (GDN_CONV - 1) + k
        acc = acc + xp[off:off + tm] * w[k:k + 1]
    y = acc * jax.nn.sigmoid(acc)
    d = GDN_HEAD_DIM
    normed = jnp.concatenate(
        [y[:, h * d:(h + 1) * d] * lax.rsqrt(jnp.sum(y[:, h * d:(h + 1) * d] ** 2, axis=-1, keepdims=True) + EPS)
         for h in range(y.shape[1] // d)], axis=1)
    qk_tiles = 2 * GDN_QK_W // y.shape[1]
    o_ref[...] = jnp.where(c < qk_tiles, normed, y)


def gdn_prep(proj, conv_w, seq, tm=1024, tc=256):
    n = proj.shape[0]
    assert (COL_GDN * LANES) % tc == 0 and GDN_QK_W % tc == 0
    col0 = COL_GDN * LANES // tc
    return pl.pallas_call(
        functools.partial(_gdn_prep_kernel, tiles_per_seq=seq // tm),
        grid=(n // tm, GDN_CONV_CH // tc),
        in_specs=[
            pl.BlockSpec((tm, tc), lambda i, c: (i, col0 + c)),
            pl.BlockSpec((SUBLANES, tc), lambda i, c: (jnp.maximum(i * (tm // SUBLANES) - 1, 0), col0 + c)),
            pl.BlockSpec((GDN_CONV, tc), lambda i, c: (0, c)),
        ],
        out_specs=pl.BlockSpec((tm, tc), lambda i, c: (i, c)),
        out_shape=jax.ShapeDtypeStruct((n, GDN_CONV_CH), F32),
        compiler_params=_params("parallel", "parallel"),
        name="gdn_conv_norm",
    )(proj, proj, conv_w)


def _gdn_gate_kernel(h_ref, w_ref, alog_ref, dtb_ref, o_ref):
    tm = h_ref.shape[0]
    t = jnp.dot(h_ref[...], w_ref[...], preferred_element_type=F32)
    lane = lax.broadcasted_iota(jnp.int32, t.shape, 1)
    beta = jax.nn.sigmoid(t)
    g = -jnp.exp(alog_ref[...]) * jax.nn.softplus(t + dtb_ref[...])
    g = jnp.where((lane >= GDN_V_HEADS) & (lane < 2 * GDN_V_HEADS), g, 0.0)
    blk = GDN_BLOCK
    r = lax.broadcasted_iota(jnp.int32, (blk, blk), 0)
    s = lax.broadcasted_iota(jnp.int32, (blk, blk), 1)
    tri = jnp.where(s <= r, 1.0, 0.0).astype(BF16)
    sums = []
    for b in range(tm // blk):
        gb = g[b * blk:(b + 1) * blk]
        hi = gb.astype(BF16)
        rest = gb - hi.astype(F32)
        mid = rest.astype(BF16)
        lo = (rest - mid.astype(F32)).astype(BF16)
        sums.append(jnp.dot(tri, hi, preferred_element_type=F32)
                    + (jnp.dot(tri, mid, preferred_element_type=F32) + jnp.dot(tri, lo, preferred_element_type=F32)))
    gcum = jnp.concatenate(sums, axis=0)
    o_ref[...] = jnp.where(lane < GDN_V_HEADS, beta, gcum)


def gdn_gates(h, w_tail, a_log, dt_bias, tm=512):
    n, k = h.shape
    pad = lambda v: jnp.zeros((1, LANES), F32).at[0, GDN_V_HEADS:2 * GDN_V_HEADS].set(v.astype(F32))
    return pl.pallas_call(
        _gdn_gate_kernel,
        grid=(n // tm,),
        in_specs=[pl.BlockSpec((tm, k), lambda i: (i, 0)),
                  pl.BlockSpec((k, LANES), lambda i: (0, 0)),
                  pl.BlockSpec((1, LANES), lambda i: (0, 0)),
                  pl.BlockSpec((1, LANES), lambda i: (0, 0))],
        out_specs=pl.BlockSpec((tm, LANES), lambda i: (i, 0)),
        out_shape=jax.ShapeDtypeStruct((n, LANES), F32),
        compiler_params=_params("parallel"),
        name="gdn_gates",
    )(h, w_tail, pad(a_log), pad(dt_bias))


def _bdot(a, b, dims=(((1,), (0,)), ((), ()))):
    return lax.dot_general(a.astype(BF16), b.astype(BF16), dims, preferred_element_type=F32)


_NT = (((1,), (1,)), ((), ()))
_TN = (((0,), (0,)), ((), ()))


def _unit_lower_inverses(lows, block):
    r_dim = lows[0].shape[0]
    r = lax.broadcasted_iota(jnp.int32, (r_dim, r_dim), 0)
    s = lax.broadcasted_iota(jnp.int32, (r_dim, r_dim), 1)
    size = 1
    xs = None
    while size < block:
        sel = (r // (2 * size) == s // (2 * size)) & ((r // size) % 2 == 1) & ((s // size) % 2 == 0)
        offs = [jnp.where(sel, low, 0.0) for low in lows]
        if size == 1:
            eye = jnp.where(r == s, 1.0, 0.0).astype(F32)
            xs = [eye - off for off in offs]
        else:
            xo = [_bdot(x, off) for x, off in zip(xs, offs)]
            xs = [x - _bdot(y, x) for x, y in zip(xs, xo)]
        size *= 2
    return xs


def _gdn_kernel(q_ref, k_ref, v_ref, z_ref, gb_ref, grow_ref, norm_ref, o_ref,
                state_ref, w_s, u_s, qd_s, kd_s, qk_s):
    hp = pl.program_id(1)
    t = pl.program_id(2)
    tb = q_ref.shape[0]
    rt = GDN_TILE
    blk = GDN_BLOCK
    pair = GDN_V_HEADS // GDN_QK_HEADS
    d = GDN_HEAD_DIM

    @pl.when(t == 0)
    def _():
        state_ref[...] = jnp.zeros_like(state_ref)

    lane = lax.broadcasted_iota(jnp.int32, (tb, LANES), 1)
    gb = gb_ref[...]
    r = lax.broadcasted_iota(jnp.int32, (rt, rt), 0)
    s = lax.broadcasted_iota(jnp.int32, (rt, rt), 1)
    same = (r // blk) == (s // blk)
    causal = (s <= r) & same
    strict = (s < r) & same

    g_heads = []
    problems = []
    for hh in range(pair):
        h = pair * hp + hh
        beta_all = jnp.sum(jnp.where(lane == h, gb, 0.0), axis=-1, keepdims=True)
        g_all = jnp.sum(jnp.where(lane == h + GDN_V_HEADS, gb, 0.0), axis=-1, keepdims=True)
        g_heads.append(g_all)
        for ti in range(tb // rt):
            rows = slice(ti * rt, (ti + 1) * rt)
            q = q_ref[rows, :] * (d ** -0.5)
            k = k_ref[rows, :]
            v = v_ref[rows, hh * d:(hh + 1) * d]
            beta = beta_all[rows]
            g = g_all[rows]
            g_row = grow_ref[ti, hh:hh + 1, :]
            decay = jnp.where(causal, jnp.exp(jnp.where(causal, g - g_row, 0.0)), 0.0)
            kb = k * beta
            eg = jnp.exp(g)
            kq = _bdot(jnp.concatenate([kb, q], axis=0), k, _NT)
            low = jnp.where(strict, kq[:rt] * decay, 0.0)
            qk_s[hh, rows, :] = (kq[rt:] * decay).astype(BF16)
            g_last = jnp.concatenate(
                [jnp.broadcast_to(g[(c + 1) * blk - 1:(c + 1) * blk], (blk, 1)) for c in range(rt // blk)], axis=0)
            qd_s[hh, rows, :] = (q * eg).astype(BF16)
            kd_s[hh, rows, :] = (k * jnp.exp(g_last - g)).astype(BF16)
            rhs = jnp.concatenate([v * beta, kb * eg], axis=1).astype(BF16)
            problems.append((hh, rows, low, rhs))

    tinvs = _unit_lower_inverses([p[2] for p in problems], blk)
    for (hh, rows, _, rhs), tinv in zip(problems, tinvs):
        uw = _bdot(tinv, rhs)
        u_s[hh, rows, :] = uw[:, :d]
        w_s[hh, rows, :] = uw[:, d:].astype(BF16)

    for c in range(tb // blk):
        rows = slice(c * blk, (c + 1) * blk)
        off = (c * blk) % rt
        for hh in range(pair):
            state = state_ref[hh]
            ws = _bdot(jnp.concatenate([w_s[hh, rows, :], qd_s[hh, rows, :]], axis=0), state)
            v_new = u_s[hh, rows, :] - ws[:blk]
            o = ws[blk:] + _bdot(qk_s[hh, rows, off:off + blk], v_new)
            g_end = g_heads[hh][(c + 1) * blk - 1:(c + 1) * blk]
            state_ref[hh] = state * jnp.exp(g_end) + _bdot(kd_s[hh, rows, :], v_new, _TN)
            o = o * lax.rsqrt(jnp.mean(o * o, axis=-1, keepdims=True) + EPS) * norm_ref[...]
            zc = z_ref[rows, hh * d:(hh + 1) * d]
            o_ref[rows, hh * d:(hh + 1) * d] = (o * (zc * jax.nn.sigmoid(zc))).astype(o_ref.dtype)


def gdn_mixer(qkv, proj, gates, g_rows, gdn_norm, batch, seq, tb=512):
    n = qkv.shape[0]
    nt = seq // tb
    pair = GDN_V_HEADS // GDN_QK_HEADS
    d = GDN_HEAD_DIM
    row = lambda b, hp, t: b * nt + t
    wide = (tb, pair * d)
    v_col = 2 * GDN_QK_HEADS // pair
    z_col = COL_Z // pair
    return pl.pallas_call(
        _gdn_kernel,
        grid=(batch, GDN_QK_HEADS, nt),
        in_specs=[
            pl.BlockSpec((tb, d), lambda b, hp, t: (row(b, hp, t), hp)),
            pl.BlockSpec((tb, d), lambda b, hp, t: (row(b, hp, t), GDN_QK_HEADS + hp)),
            pl.BlockSpec(wide, lambda b, hp, t: (row(b, hp, t), v_col + hp)),
            pl.BlockSpec(wide, lambda b, hp, t: (row(b, hp, t), z_col + hp)),
            pl.BlockSpec((tb, LANES), lambda b, hp, t: (row(b, hp, t), 0)),
            pl.BlockSpec((None, tb // GDN_TILE, pair, GDN_TILE), lambda b, hp, t: (hp, row(b, hp, t), 0, 0)),
            pl.BlockSpec((1, LANES), lambda b, hp, t: (0, 0)),
        ],
        out_specs=pl.BlockSpec(wide, lambda b, hp, t: (row(b, hp, t), hp)),
        out_shape=jax.ShapeDtypeStruct((n, GDN_V_W), BF16),
        scratch_shapes=[
            pltpu.VMEM((pair, d, d), F32),
            pltpu.VMEM((pair, tb, d), BF16),
            pltpu.VMEM((pair, tb, d), F32),
            pltpu.VMEM((pair, tb, d), BF16),
            pltpu.VMEM((pair, tb, d), BF16),
            pltpu.VMEM((pair, tb, GDN_TILE), BF16),
        ],
        compiler_params=_params("parallel", "parallel", "arbitrary"),
        name="gated_delta_rule",
    )(qkv, qkv, qkv, proj, gates, g_rows, gdn_norm.reshape(1, LANES).astype(F32))


def _rms(x, gain):
    return x * lax.rsqrt(jnp.mean(x * x, axis=-1, keepdims=True) + EPS) * gain


def _out_proj_kernel(a_ref, b_ref, w_ref, x_ref, g_ref, o_ref, h_ref):
    ka = a_ref.shape[1]
    acc = jnp.dot(a_ref[...], w_ref[:ka, :], preferred_element_type=F32)
    acc += jnp.dot(b_ref[...], w_ref[ka:, :], preferred_element_type=F32)
    x = x_ref[...] + acc
    o_ref[...] = x
    h_ref[...] = _rms(x, g_ref[...]).astype(h_ref.dtype)


def out_proj_residual(attn, gdn, w_out, li, x, next_gain, tm=512):
    n, d = x.shape
    ka, kb = attn.shape[1], gdn.shape[1]
    row = lambda w: pl.BlockSpec((tm, w), lambda i: (i, 0))
    return pl.pallas_call(
        _out_proj_kernel,
        grid=(n // tm,),
        in_specs=[row(ka), row(kb),
                  pl.BlockSpec((None, ka + kb, d), lambda i: (li, 0, 0)),
                  row(d),
                  pl.BlockSpec((1, d), lambda i: (0, 0))],
        out_specs=[row(d), row(d)],
        out_shape=[jax.ShapeDtypeStruct((n, d), F32), jax.ShapeDtypeStruct((n, d), BF16)],
        compiler_params=_params("parallel"),
        name="out_proj_residual",
    )(attn, gdn, w_out, x, next_gain.reshape(1, d))


def _swiglu_tile(h, wg, wu):
    g = jnp.dot(h, wg, preferred_element_type=F32)
    u = jnp.dot(h, wu, preferred_element_type=F32)
    return g * jax.nn.sigmoid(g) * u


def _gate_up_kernel(h_ref, wg_ref, wu_ref, o_ref):
    o_ref[...] = _swiglu_tile(h_ref[...], wg_ref[...], wu_ref[...]).astype(o_ref.dtype)


def gate_up(h, w_gate, w_up, li, tm=512, tn=1024):
    m, k = h.shape
    f = w_gate.shape[2]
    return pl.pallas_call(
        _gate_up_kernel,
        grid=(f // tn, m // tm),
        in_specs=[
            pl.BlockSpec((tm, k), lambda j, i: (i, 0)),
            pl.BlockSpec((None, k, tn), lambda j, i: (li, 0, j)),
            pl.BlockSpec((None, k, tn), lambda j, i: (li, 0, j)),
        ],
        out_specs=pl.BlockSpec((tm, tn), lambda j, i: (i, j)),
        out_shape=jax.ShapeDtypeStruct((m, f), BF16),
        compiler_params=_params("parallel", "parallel"),
        name="swiglu_gate_up",
    )(h, w_gate, w_up)


def _gate_up_grouped_kernel(meta_ref, h_ref, wg_ref, wu_ref, o_ref):
    i = pl.program_id(1)
    used = meta_ref[pl.num_programs(1)]

    @pl.when(i < used)
    def _():
        o_ref[...] = _swiglu_tile(h_ref[...], wg_ref[...].astype(BF16), wu_ref[...].astype(BF16)).astype(o_ref.dtype)

    @pl.when(i >= used)
    def _():
        o_ref[...] = jnp.zeros_like(o_ref)


def gate_up_grouped(meta, h, w_gate, w_up, li, tm, tn=1024):
    m, k = h.shape
    f = w_gate.shape[3]
    nb = m // tm
    row = lambda j, i, meta: (jnp.minimum(i, meta[nb] - 1), 0)
    wmap = lambda j, i, meta: (li, meta[i], 0, j)
    return pl.pallas_call(
        _gate_up_grouped_kernel,
        grid_spec=pltpu.PrefetchScalarGridSpec(
            num_scalar_prefetch=1,
            grid=(f // tn, nb),
            in_specs=[
                pl.BlockSpec((tm, k), row),
                pl.BlockSpec((None, None, k, tn), wmap),
                pl.BlockSpec((None, None, k, tn), wmap),
            ],
            out_specs=pl.BlockSpec((tm, tn), lambda j, i, meta: (i, j)),
        ),
        out_shape=jax.ShapeDtypeStruct((m, f), BF16),
        compiler_params=_params("parallel", "arbitrary"),
        name="moe_gate_up",
    )(meta, h, w_gate, w_up)


def _down_kernel(a_ref, w_ref, x_ref, o_ref):
    o_ref[...] = x_ref[...] + jnp.dot(a_ref[...], w_ref[...], preferred_element_type=F32)


def down_residual(a, w_down, li, x, tm=512, tn=512):
    n, d = x.shape
    f = a.shape[1]
    return pl.pallas_call(
        _down_kernel,
        grid=(d // tn, n // tm),
        in_specs=[
            pl.BlockSpec((tm, f), lambda j, i: (i, 0)),
            pl.BlockSpec((None, f, tn), lambda j, i: (li, 0, j)),
            pl.BlockSpec((tm, tn), lambda j, i: (i, j)),
        ],
        out_specs=pl.BlockSpec((tm, tn), lambda j, i: (i, j)),
        out_shape=jax.ShapeDtypeStruct((n, d), F32),
        compiler_params=_params("parallel", "parallel"),
        name="swiglu_down_residual",
    )(a, w_down, x)


def _down_grouped_kernel(meta_ref, a_ref, w_ref, o_ref):
    i = pl.program_id(1)
    used = meta_ref[pl.num_programs(1)]

    @pl.when(i < used)
    def _():
        o_ref[...] = jnp.dot(a_ref[...], w_ref[...].astype(BF16), preferred_element_type=F32)

    @pl.when(i >= used)
    def _():
        o_ref[...] = jnp.zeros_like(o_ref)


def down_grouped(meta, a, w_down, li, tm, tn=512):
    m, f = a.shape
    d = w_down.shape[3]
    nb = m // tm
    return pl.pallas_call(
        _down_grouped_kernel,
        grid_spec=pltpu.PrefetchScalarGridSpec(
            num_scalar_prefetch=1,
            grid=(d // tn, nb),
            in_specs=[
                pl.BlockSpec((tm, f), lambda j, i, meta: (jnp.minimum(i, meta[nb] - 1), 0)),
                pl.BlockSpec((None, None, f, tn), lambda j, i, meta: (li, meta[i], 0, j)),
            ],
            out_specs=pl.BlockSpec((tm, tn), lambda j, i, meta: (i, j)),
        ),
        out_shape=jax.ShapeDtypeStruct((m, d), F32),
        compiler_params=_params("parallel", "arbitrary"),
        name="moe_down",
    )(meta, a, w_down)


def _router_kernel(x_ref, g_ref, w_ref, h_ref, idx_ref, gate_ref):
    x = x_ref[...]
    h = x * lax.rsqrt(jnp.mean(x * x, axis=-1, keepdims=True) + EPS) * g_ref[...]
    h_ref[...] = h.astype(h_ref.dtype)
    logits = jnp.dot(h, w_ref[...], preferred_element_type=F32, precision=lax.Precision.HIGHEST)
    lane = lax.broadcasted_iota(jnp.int32, logits.shape, 1)
    logits = jnp.where(lane < N_EXPERTS, logits, -jnp.inf)
    m1 = jnp.max(logits, axis=-1, keepdims=True)
    i1 = jnp.min(jnp.where(logits == m1, lane, LANES), axis=-1, keepdims=True)
    rest = jnp.where(lane == i1, -jnp.inf, logits)
    m2 = jnp.max(rest, axis=-1, keepdims=True)
    i2 = jnp.min(jnp.where(rest == m2, lane, LANES), axis=-1, keepdims=True)
    e2 = jnp.exp(m2 - m1)
    denom = 1.0 + e2
    idx_ref[...] = jnp.where(lane == 0, i1, jnp.where(lane == 1, i2, 0))
    gate_ref[...] = jnp.where(lane == 0, 1.0 / denom, jnp.where(lane == 1, e2 / denom, 0.0))


def moe_router(x, gain, w_router, tm=256):
    n, d = x.shape
    w = jnp.zeros((d, LANES), F32).at[:, :N_EXPERTS].set(w_router.astype(F32))
    return pl.pallas_call(
        _router_kernel,
        grid=(n // tm,),
        in_specs=[pl.BlockSpec((tm, d), lambda i: (i, 0)),
                  pl.BlockSpec((1, d), lambda i: (0, 0)),
                  pl.BlockSpec((d, LANES), lambda i: (0, 0))],
        out_specs=[pl.BlockSpec((tm, d), lambda i: (i, 0)),
                   pl.BlockSpec((tm, LANES), lambda i: (i, 0)),
                   pl.BlockSpec((tm, LANES), lambda i: (i, 0))],
        out_shape=[jax.ShapeDtypeStruct((n, d), BF16),
                   jax.ShapeDtypeStruct((n, LANES), jnp.int32),
                   jax.ShapeDtypeStruct((n, LANES), F32)],
        compiler_params=_params("parallel"),
        name="moe_router",
    )(x, gain.reshape(1, d), w)


def _combine_kernel(x_ref, a_ref, b_ref, g_ref, gain_ref, o_ref, h_ref):
    g = g_ref[...]
    x = x_ref[...] + (a_ref[...] * g[:, 0:1] + b_ref[...] * g[:, 1:2])
    o_ref[...] = x
    h_ref[...] = _rms(x, gain_ref[...]).astype(h_ref.dtype)


def moe_combine(x, ya, yb, gates, next_gain, tm=512):
    n, d = x.shape
    row = pl.BlockSpec((tm, d), lambda i: (i, 0))
    return pl.pallas_call(
        _combine_kernel,
        grid=(n // tm,),
        in_specs=[row, row, row, pl.BlockSpec((tm, LANES), lambda i: (i, 0)), pl.BlockSpec((1, d), lambda i: (0, 0))],
        out_specs=[row, row],
        out_shape=[jax.ShapeDtypeStruct((n, d), F32), jax.ShapeDtypeStruct((n, d), BF16)],
        compiler_params=_params("parallel"),
        name="moe_combine",
    )(x, ya, yb, gates, next_gain.reshape(1, d))


def moe_block(x, gain, w_router, w_gate, w_up, w_down, li, next_gain, tm=512):
    n, d = x.shape
    h, idx, gates = moe_router(x, gain, w_router)
    flat_e = idx[:, :TOP_K].reshape(-1)
    na = n * TOP_K
    onehot = (flat_e[:, None] == jnp.arange(N_EXPERTS, dtype=jnp.int32)[None, :]).astype(jnp.int32)
    csum = jnp.cumsum(onehot, axis=0)
    rank = jnp.take_along_axis(csum, flat_e[:, None], axis=1)[:, 0] - 1
    counts = csum[-1]
    padded = (counts + tm - 1) // tm * tm
    pad_end = jnp.cumsum(padded)
    pad_start = pad_end - padded
    dest = pad_start[flat_e] + rank
    n_blocks = na // tm + N_EXPERTS
    cap = n_blocks * tm
    slot_tok = jnp.zeros((cap,), jnp.int32).at[dest].set(jnp.arange(na, dtype=jnp.int32) // TOP_K)
    block_e = jnp.minimum(
        jnp.searchsorted(pad_end, jnp.arange(n_blocks, dtype=jnp.int32) * tm, side='right'),
        N_EXPERTS - 1).astype(jnp.int32)
    meta = jnp.concatenate([block_e, (pad_end[-1:] // tm).astype(jnp.int32)])
    rows = lambda a, idx: a.at[idx].get(mode="promise_in_bounds")
    xs = rows(h, slot_tok)
    act = gate_up_grouped(meta, xs, w_gate, w_up, li, tm)
    ys = down_grouped(meta, act, w_down, li, tm)
    dest2 = dest.reshape(n, TOP_K)
    return moe_combine(x, rows(ys, dest2[:, 0]), rows(ys, dest2[:, 1]), gates, next_gain)


def _ple_kernel(h_ref, wg_ref, p_ref, wp_ref, x_ref, gain_ref, o_ref, hn_ref):
    gate = jax.nn.sigmoid(jnp.dot(h_ref[...], wg_ref[...], preferred_element_type=F32))
    proj = jnp.dot(p_ref[...].astype(BF16), wp_ref[...].astype(BF16), preferred_element_type=F32)
    x = x_ref[...] + proj * gate
    o_ref[...] = x
    hn_ref[...] = _rms(x, gain_ref[...]).astype(hn_ref.dtype)


def ple_residual(h, w_gate, p, w_proj, li, x, next_gain, next_dtype, tm=512):
    n, d = x.shape
    kp = p.shape[2]
    row = lambda w: pl.BlockSpec((tm, w), lambda i: (i, 0))
    return pl.pallas_call(
        _ple_kernel,
        grid=(n // tm,),
        in_specs=[
            row(d),
            pl.BlockSpec((None, d, d), lambda i: (li, 0, 0)),
            pl.BlockSpec((None, tm, kp), lambda i: (li, i, 0)),
            pl.BlockSpec((None, kp, d), lambda i: (li, 0, 0)),
            row(d),
            pl.BlockSpec((1, d), lambda i: (0, 0)),
        ],
        out_specs=[row(d), row(d)],
        out_shape=[jax.ShapeDtypeStruct((n, d), F32), jax.ShapeDtypeStruct((n, d), next_dtype)],
        compiler_params=_params("parallel"),
        name="ple_residual",
    )(h, w_gate, p, w_proj, x, next_gain.reshape(1, d))


def hybrid_mixer_residual(x, h, w_in, li, conv_w, a_log, dt_bias, gdn_norm, sinks, bias, w_out, next_gain,
                          batch, seq):
    n = x.shape[0]
    proj = matmul_cols_outer(h, w_in, li, IN_MAIN_W, tm=512, tn=IN_MAIN_W // 2)
    w_tail = jnp.zeros((D_MODEL, LANES), BF16).at[:, :IN_TAIL_W].set(w_in[li, :, IN_MAIN_W:])
    attn = swa_attention(proj, sinks.astype(F32), bias, batch, seq)
    qkv = gdn_prep(proj, conv_w, seq)
    gates = gdn_gates(h, w_tail, a_log, dt_bias)
    pair = GDN_V_HEADS // GDN_QK_HEADS
    g_rows = gates[:, GDN_V_HEADS:2 * GDN_V_HEADS].T.reshape(GDN_QK_HEADS, pair, n // GDN_TILE, GDN_TILE)
    g_rows = g_rows.transpose(0, 2, 1, 3)
    gdn = gdn_mixer(qkv, proj, gates, g_rows, gdn_norm, batch, seq)
    return out_proj_residual(attn, gdn, w_out, li, x, next_gain)


def kernel(x, p, w_in, conv_w, a_log, dt_bias, gdn_norm, attn_sinks, rel_bias_table, w_out, norm_mix, norm_ffn, w_dense_gate, w_dense_up, w_dense_down, w_router, w_exp_gate, w_exp_up, w_exp_down, norm_ple, w_ple_gate, w_ple_proj, norm_final):
    batch, seq, d = x.shape
    depth = w_in.shape[0]
    n = batch * seq
    out_dtype = x.dtype
    x = x.reshape(n, d).astype(F32)
    bias = relative_bias(rel_bias_table)
    p = p.reshape(depth, n, PLE_DIM)
    w_in, w_out, w_ple_gate = w_in.astype(BF16), w_out.astype(BF16), w_ple_gate.astype(BF16)
    w_dense_gate, w_dense_up, w_dense_down = (w.astype(BF16) for w in (w_dense_gate, w_dense_up, w_dense_down))
    h = rmsnorm_rows(x, norm_mix[0], BF16)
    for i in range(depth):
        x, hf = hybrid_mixer_residual(x, h, w_in, i, conv_w[i], a_log[i], dt_bias[i], gdn_norm[i],
                                      attn_sinks[i], bias, w_out, norm_ffn[i], batch, seq)
        if i % 2 == 0:
            act = gate_up(hf, w_dense_gate, w_dense_up, i // 2)
            x = down_residual(act, w_dense_down, i // 2, x)
            hp = rmsnorm_rows(x, norm_ple[i], BF16)
        else:
            x, hp = moe_block(x, norm_ffn[i], w_router[i // 2], w_exp_gate, w_exp_up, w_exp_down, i // 2,
                              norm_ple[i])
        last = i == depth - 1
        x, h = ple_residual(hp, w_ple_gate, p, w_ple_proj, i, x, norm_final if last else norm_mix[i + 1],
                            out_dtype if last else BF16)
    return h.reshape(batch, seq, d)
```

```python
import functools
import math

import jax
import jax.numpy as jnp
from jax import lax
from jax.experimental import pallas as pl
from jax.experimental.pallas import tpu as pltpu

F32 = jnp.float32
BF16 = jnp.bfloat16

D_MODEL = 2048
PLE_DIM = 256
EPS = 1e-6
NEG_INF = -1e30

SWA_HEADS = 16
SWA_KV_HEADS = 2
SWA_HEAD_DIM = 64
SWA_GROUP = SWA_HEADS // SWA_KV_HEADS
WINDOW = 128
SWA_BLOCK = 128
REL_BUCKETS = 32
REL_MAX_DIST = 128

GDN_QK_HEADS = 4
GDN_V_HEADS = 8
GDN_HEAD_DIM = 128
GDN_CONV = 4
GDN_TILE = 256
GDN_BLOCK = 256

SWA_Q_W = SWA_HEADS * SWA_HEAD_DIM
SWA_KV_W = SWA_KV_HEADS * SWA_HEAD_DIM
GDN_QK_W = GDN_QK_HEADS * GDN_HEAD_DIM
GDN_V_W = GDN_V_HEADS * GDN_HEAD_DIM
GDN_CONV_CH = 2 * GDN_QK_W + GDN_V_W
MIX_WIDTH = SWA_Q_W + GDN_V_W
IN_MAIN_W = SWA_Q_W + 2 * SWA_KV_W + GDN_CONV_CH + GDN_V_W
IN_TAIL_W = 2 * GDN_V_HEADS

D_FF = 7 * D_MODEL // 2
N_EXPERTS = 8
TOP_K = 2

LANES = 128
SUBLANES = 8
VMEM_LIMIT = 56 * 1024 * 1024

COL_K = SWA_Q_W // LANES
COL_V = COL_K + SWA_KV_W // LANES
COL_GDN = COL_V + SWA_KV_W // LANES
COL_Z = COL_GDN + GDN_CONV_CH // LANES


def _params(*sem):
    return pltpu.CompilerParams(dimension_semantics=sem, vmem_limit_bytes=VMEM_LIMIT)


def _rms_kernel(x_ref, g_ref, o_ref):
    x = x_ref[...]
    y = x * lax.rsqrt(jnp.mean(x * x, axis=-1, keepdims=True) + EPS)
    o_ref[...] = (y * g_ref[...]).astype(o_ref.dtype)


def rmsnorm_rows(x, gain, out_dtype, tm=512):
    n, d = x.shape
    return pl.pallas_call(
        _rms_kernel,
        grid=(n // tm,),
        in_specs=[pl.BlockSpec((tm, d), lambda i: (i, 0)), pl.BlockSpec((1, d), lambda i: (0, 0))],
        out_specs=pl.BlockSpec((tm, d), lambda i: (i, 0)),
        out_shape=jax.ShapeDtypeStruct((n, d), out_dtype),
        compiler_params=_params("parallel"),
        name="rmsnorm",
    )(x, gain.reshape(1, d))


def _mm_kernel(a_ref, w_ref, o_ref):
    o_ref[...] = jnp.dot(a_ref[...], w_ref[...], preferred_element_type=F32).astype(o_ref.dtype)


def matmul_cols_outer(a, w, li, n_out, tm, tn, out_dtype=F32):
    m, k = a.shape
    return pl.pallas_call(
        _mm_kernel,
        grid=(n_out // tn, m // tm),
        in_specs=[pl.BlockSpec((tm, k), lambda j, i: (i, 0)),
                  pl.BlockSpec((None, k, tn), lambda j, i: (li, 0, j))],
        out_specs=pl.BlockSpec((tm, tn), lambda j, i: (i, j)),
        out_shape=jax.ShapeDtypeStruct((m, n_out), out_dtype),
        compiler_params=_params("parallel", "parallel"),
        name="in_proj",
    )(a, w)


def _attn_kernel(sink_ref, q_ref, kc_ref, kp_ref, vc_ref, vp_ref, bias_ref, o_ref):
    n = pl.program_id(1)
    kj = lax.broadcasted_iota(jnp.int32, (SWA_BLOCK, SWA_BLOCK), 0)
    qi = lax.broadcasted_iota(jnp.int32, (SWA_BLOCK, SWA_BLOCK), 1)
    in_cur = kj <= qi
    valid = in_cur | (n > 0)
    scale = SWA_HEAD_DIM ** -0.5
    for g in range(SWA_KV_HEADS):
        cols = slice(g * SWA_HEAD_DIM, (g + 1) * SWA_HEAD_DIM)
        kw = jnp.concatenate([kp_ref[:, cols], kc_ref[:, cols]], axis=0).astype(BF16)
        vt = jnp.concatenate([vp_ref[:, cols], vc_ref[:, cols]], axis=0).T.astype(BF16)
        heads = [g * SWA_GROUP + hh for hh in range(SWA_GROUP)]
        qs = [(q_ref[:, h * SWA_HEAD_DIM:(h + 1) * SWA_HEAD_DIM] * scale).astype(BF16) for h in heads]
        boths = [lax.dot_general(kw, qh, _NT, preferred_element_type=F32) for qh in qs]
        logits = [jnp.where(valid, jnp.where(in_cur, b[SWA_BLOCK:], b[:SWA_BLOCK]) + bias_ref[h], NEG_INF)
                  for h, b in zip(heads, boths)]
        ms = [jnp.maximum(jnp.max(l, axis=0, keepdims=True), sink_ref[h]) for h, l in zip(heads, logits)]
        es = [jnp.exp(l - m) for l, m in zip(logits, ms)]
        denoms = [jnp.sum(e, axis=0, keepdims=True) + jnp.exp(sink_ref[h] - m) for h, e, m in zip(heads, es, ms)]
        unfolded = [jnp.concatenate([jnp.where(in_cur, 0.0, e), jnp.where(in_cur, e, 0.0)], axis=0).astype(BF16)
                    for e in es]
        outs = [jnp.dot(vt, u, preferred_element_type=F32) * (1.0 / d) for u, d in zip(unfolded, denoms)]
        for pair in range(SWA_GROUP // 2):
            h0 = heads[2 * pair]
            both_heads = jnp.concatenate(outs[2 * pair:2 * pair + 2], axis=0).T
            o_ref[:, h0 * SWA_HEAD_DIM:(h0 + 2) * SWA_HEAD_DIM] = both_heads.astype(o_ref.dtype)


def folded_relative_bias(rel_table):
    qi = jnp.arange(SWA_BLOCK, dtype=jnp.int32)[:, None]
    kj = jnp.arange(SWA_BLOCK, dtype=jnp.int32)[None, :]
    banded = relative_bias(rel_table)
    folded = jnp.where((kj <= qi)[None], banded[:, :, SWA_BLOCK:], banded[:, :, :SWA_BLOCK])
    return folded.transpose(0, 2, 1)


def relative_bias(rel_table):
    qi = jnp.arange(SWA_BLOCK, dtype=jnp.int32)[:, None]
    kj = jnp.arange(2 * SWA_BLOCK, dtype=jnp.int32)[None, :]
    dist = qi + SWA_BLOCK - kj
    max_exact = REL_BUCKETS // 2
    d = jnp.maximum(dist, 0)
    log_ratio = jnp.log(jnp.maximum(d, 1).astype(F32) / max_exact) / math.log(REL_MAX_DIST / max_exact)
    large = jnp.minimum(max_exact + (log_ratio * (REL_BUCKETS - max_exact)).astype(jnp.int32), REL_BUCKETS - 1)
    bucket = jnp.where(d < max_exact, d, large)
    return rel_table[bucket].astype(F32).transpose(2, 0, 1)


def swa_attention(proj, sinks, bias, batch, seq):
    n = proj.shape[0]
    nb = seq // SWA_BLOCK
    cur = lambda b, i, s: b * nb + i
    prev = lambda b, i, s: jnp.maximum(b * nb + i - 1, 0)
    blk = (SWA_BLOCK, LANES)
    return pl.pallas_call(
        _attn_kernel,
        grid_spec=pltpu.PrefetchScalarGridSpec(
            num_scalar_prefetch=1,
            grid=(batch, nb),
            in_specs=[
                pl.BlockSpec((SWA_BLOCK, SWA_Q_W), lambda b, i, s: (cur(b, i, s), 0)),
                pl.BlockSpec(blk, lambda b, i, s: (cur(b, i, s), COL_K)),
                pl.BlockSpec(blk, lambda b, i, s: (prev(b, i, s), COL_K)),
                pl.BlockSpec(blk, lambda b, i, s: (cur(b, i, s), COL_V)),
                pl.BlockSpec(blk, lambda b, i, s: (prev(b, i, s), COL_V)),
                pl.BlockSpec((SWA_HEADS, SWA_BLOCK, SWA_BLOCK), lambda b, i, s: (0, 0, 0)),
            ],
            out_specs=pl.BlockSpec((SWA_BLOCK, SWA_Q_W), lambda b, i, s: (cur(b, i, s), 0)),
        ),
        out_shape=jax.ShapeDtypeStruct((n, SWA_Q_W), BF16),
        compiler_params=_params("parallel", "parallel"),
        name="swa_attention",
    )(sinks, proj, proj, proj, proj, proj, bias)


def _gdn_prep_kernel(cur_ref, prev_ref, w_ref, o_ref, *, tiles_per_seq):
    i = pl.program_id(0)
    c = pl.program_id(1)
    tm = cur_ref.shape[0]
    cur = cur_ref[...]
    prev = jnp.where(i % tiles_per_seq == 0, 0.0, prev_ref[...])
    xp = jnp.concatenate([prev, cur], axis=0)
    w = w_ref[...]
    acc = xp[SUBLANES - 3:SUBLANES - 3 + tm] * w[0:1]
    for k in range(1, GDN_CONV):
        off = SUBLANES - (GDN_CONV - 1) + k
        acc = acc + xp[off:off + tm] * w[k:k + 1]
    y = acc * jax.nn.sigmoid(acc)
    d = GDN_HEAD_DIM
    normed = jnp.concatenate(
        [y[:, h * d:(h + 1) * d] * lax.rsqrt(jnp.sum(y[:, h * d:(h + 1) * d] ** 2, axis=-1, keepdims=True) + EPS)
         for h in range(y.shape[1] // d)], axis=1)
    qk_tiles = 2 * GDN_QK_W // y.shape[1]
    o_ref[...] = jnp.where(c < qk_tiles, normed, y)


def gdn_prep(proj, conv_w, seq, tm=1024, tc=256):
    n = proj.shape[0]
    assert (COL_GDN * LANES) % tc == 0 and GDN_QK_W % tc == 0
    col0 = COL_GDN * LANES // tc
    return pl.pallas_call(
        functools.partial(_gdn_prep_kernel, tiles_per_seq=seq // tm),
        grid=(n // tm, GDN_CONV_CH // tc),
        in_specs=[
            pl.BlockSpec((tm, tc), lambda i, c: (i, col0 + c)),
            pl.BlockSpec((SUBLANES, tc), lambda i, c: (jnp.maximum(i * (tm // SUBLANES) - 1, 0), col0 + c)),
            pl.BlockSpec((GDN_CONV, tc), lambda i, c: (0, c)),
        ],
        out_specs=pl.BlockSpec((tm, tc), lambda i, c: (i, c)),
        out_shape=jax.ShapeDtypeStruct((n, GDN_CONV_CH), F32),
        compiler_params=_params("parallel", "parallel"),
        name="gdn_conv_norm",
    )(proj, proj, conv_w)


def _gdn_gate_kernel(h_ref, w_ref, alog_ref, dtb_ref, o_ref):
    tm = h_ref.shape[0]
    t = jnp.dot(h_ref[...], w_ref[...], preferred_element_type=F32)
    lane = lax.broadcasted_iota(jnp.int32, t.shape, 1)
    beta = jax.nn.sigmoid(t)
    g = -jnp.exp(alog_ref[...]) * jax.nn.softplus(t + dtb_ref[...])
    g = jnp.where((lane >= GDN_V_HEADS) & (lane < 2 * GDN_V_HEADS), g, 0.0)
    blk = GDN_BLOCK
    r = lax.broadcasted_iota(jnp.int32, (blk, blk), 0)
    s = lax.broadcasted_iota(jnp.int32, (blk, blk), 1)
    tri = jnp.where(s <= r, 1.0, 0.0).astype(BF16)
    sums = []
    for b in range(tm // blk):
        gb = g[b * blk:(b + 1) * blk]
        hi = gb.astype(BF16)
        rest = gb - hi.astype(F32)
        mid = rest.astype(BF16)
        lo = (rest - mid.astype(F32)).astype(BF16)
        sums.append(jnp.dot(tri, hi, preferred_element_type=F32)
                    + (jnp.dot(tri, mid, preferred_element_type=F32) + jnp.dot(tri, lo, preferred_element_type=F32)))
    gcum = jnp.concatenate(sums, axis=0)
    o_ref[...] = jnp.where(lane < GDN_V_HEADS, beta, gcum)


def gdn_gates(h, w_tail, a_log, dt_bias, tm=512):
    n, k = h.shape
    pad = lambda v: jnp.zeros((1, LANES), F32).at[0, GDN_V_HEADS:2 * GDN_V_HEADS].set(v.astype(F32))
    return pl.pallas_call(
        _gdn_gate_kernel,
        grid=(n // tm,),
        in_specs=[pl.BlockSpec((tm, k), lambda i: (i, 0)),
                  pl.BlockSpec((k, LANES), lambda i: (0, 0)),
                  pl.BlockSpec((1, LANES), lambda i: (0, 0)),
                  pl.BlockSpec((1, LANES), lambda i: (0, 0))],
        out_specs=pl.BlockSpec((tm, LANES), lambda i: (i, 0)),
        out_shape=jax.ShapeDtypeStruct((n, LANES), F32),
        compiler_params=_params("parallel"),
        name="gdn_gates",
    )(h, w_tail, pad(a_log), pad(dt_bias))


def _bdot(a, b, dims=(((1,), (0,)), ((), ()))):
    return lax.dot_general(a.astype(BF16), b.astype(BF16), dims, preferred_element_type=F32)


_NT = (((1,), (1,)), ((), ()))
_TN = (((0,), (0,)), ((), ()))


def _unit_lower_inverses(lows, block):
    r_dim = lows[0].shape[0]
    r = lax.broadcasted_iota(jnp.int32, (r_dim, r_dim), 0)
    s = lax.broadcasted_iota(jnp.int32, (r_dim, r_dim), 1)
    level = 31 - lax.clz(r ^ s)
    size = 1
    step = 0
    xs = None
    while size < block:
        sel = level == step
        step += 1
        offs = [jnp.where(sel, low, 0.0) for low in lows]
        if size == 1:
            eye = jnp.where(r == s, 1.0, 0.0).astype(F32)
            xs = [eye - off for off in offs]
        else:
            xo = [_bdot(x, off) for x, off in zip(xs, offs)]
            xs = [x - _bdot(y, x) for x, y in zip(xs, xo)]
        size *= 2
    return xs


def _gdn_kernel(q_ref, k_ref, v_ref, z_ref, gb_ref, grow_ref, norm_ref, o_ref,
                state_ref, w_s, u_s, qd_s, kd_s, qk_s):
    hp = pl.program_id(1)
    t = pl.program_id(2)
    tb = q_ref.shape[0]
    rt = GDN_TILE
    blk = GDN_BLOCK
    pair = GDN_V_HEADS // GDN_QK_HEADS
    d = GDN_HEAD_DIM

    @pl.when(t == 0)
    def _():
        state_ref[...] = jnp.zeros_like(state_ref)

    lane = lax.broadcasted_iota(jnp.int32, (tb, LANES), 1)
    gb = gb_ref[...]
    r = lax.broadcasted_iota(jnp.int32, (rt, rt), 0)
    s = lax.broadcasted_iota(jnp.int32, (rt, rt), 1)
    same = (r // blk) == (s // blk)
    causal = (s <= r) & same
    strict = (s < r) & same

    g_heads = []
    beta_heads = []
    for hh in range(pair):
        h = pair * hp + hh
        beta_heads.append(jnp.sum(jnp.where(lane == h, gb, 0.0), axis=-1, keepdims=True))
        g_heads.append(jnp.sum(jnp.where(lane == h + GDN_V_HEADS, gb, 0.0), axis=-1, keepdims=True))

    probs = [(hh, ti, slice(ti * rt, (ti + 1) * rt)) for hh in range(pair) for ti in range(tb // rt)]
    qs = {ti: q_ref[ti * rt:(ti + 1) * rt, :] * (d ** -0.5) for ti in range(tb // rt)}
    ks = {ti: k_ref[ti * rt:(ti + 1) * rt, :] for ti in range(tb // rt)}
    betas = [beta_heads[hh][rows] for hh, ti, rows in probs]
    gs = [g_heads[hh][rows] for hh, ti, rows in probs]
    egs = [jnp.exp(g) for g in gs]
    kbs = [ks[ti] * beta for (hh, ti, rows), beta in zip(probs, betas)]
    kqs = [_bdot(jnp.concatenate([kb, qs[ti]], axis=0), ks[ti], _NT) for (hh, ti, rows), kb in zip(probs, kbs)]
    decays = [jnp.where(causal, jnp.exp(jnp.where(causal, g - grow_ref[ti, hh:hh + 1, :], 0.0)), 0.0)
              for (hh, ti, rows), g in zip(probs, gs)]
    lows = [jnp.where(strict, kq[:rt] * decay, 0.0) for kq, decay in zip(kqs, decays)]
    for (hh, ti, rows), kq, decay, g, eg in zip(probs, kqs, decays, gs, egs):
        qk_s[hh, rows, :] = (kq[rt:] * decay).astype(BF16)
        g_last = jnp.concatenate(
            [jnp.broadcast_to(g[(c + 1) * blk - 1:(c + 1) * blk], (blk, 1)) for c in range(rt // blk)], axis=0)
        qd_s[hh, rows, :] = (qs[ti] * eg).astype(BF16)
        kd_s[hh, rows, :] = (ks[ti] * jnp.exp(g_last - g)).astype(BF16)
    rhss = [jnp.concatenate([v_ref[rows, hh * d:(hh + 1) * d] * beta, kb * eg], axis=1).astype(BF16)
            for (hh, ti, rows), beta, kb, eg in zip(probs, betas, kbs, egs)]

    tinvs = _unit_lower_inverses(lows, blk)
    uws = [_bdot(tinv, rhs) for tinv, rhs in zip(tinvs, rhss)]
    for (hh, ti, rows), uw in zip(probs, uws):
        u_s[hh, rows, :] = uw[:, :d]
        w_s[hh, rows, :] = uw[:, d:].astype(BF16)

    heads = range(pair)
    for c in range(tb // blk):
        rows = slice(c * blk, (c + 1) * blk)
        off = (c * blk) % rt
        states = [state_ref[hh] for hh in heads]
        ws = [_bdot(jnp.concatenate([w_s[hh, rows, :], qd_s[hh, rows, :]], axis=0), states[hh]) for hh in heads]
        v_new = [u_s[hh, rows, :] - ws[hh][:blk] for hh in heads]
        outs = [ws[hh][blk:] + _bdot(qk_s[hh, rows, off:off + blk], v_new[hh]) for hh in heads]
        for hh in heads:
            g_end = g_heads[hh][(c + 1) * blk - 1:(c + 1) * blk]
            state_ref[hh] = states[hh] * jnp.exp(g_end) + _bdot(kd_s[hh, rows, :], v_new[hh], _TN)
        for hh in heads:
            o = outs[hh]
            o = o * lax.rsqrt(jnp.mean(o * o, axis=-1, keepdims=True) + EPS) * norm_ref[...]
            zc = z_ref[rows, hh * d:(hh + 1) * d]
            o_ref[rows, hh * d:(hh + 1) * d] = (o * (zc * jax.nn.sigmoid(zc))).astype(o_ref.dtype)


def gdn_mixer(qkv, proj, gates, g_rows, gdn_norm, batch, seq, tb=512):
    n = qkv.shape[0]
    nt = seq // tb
    pair = GDN_V_HEADS // GDN_QK_HEADS
    d = GDN_HEAD_DIM
    row = lambda b, hp, t: b * nt + t
    wide = (tb, pair * d)
    v_col = 2 * GDN_QK_HEADS // pair
    z_col = COL_Z // pair
    return pl.pallas_call(
        _gdn_kernel,
        grid=(batch, GDN_QK_HEADS, nt),
        in_specs=[
            pl.BlockSpec((tb, d), lambda b, hp, t: (row(b, hp, t), hp)),
            pl.BlockSpec((tb, d), lambda b, hp, t: (row(b, hp, t), GDN_QK_HEADS + hp)),
            pl.BlockSpec(wide, lambda b, hp, t: (row(b, hp, t), v_col + hp)),
            pl.BlockSpec(wide, lambda b, hp, t: (row(b, hp, t), z_col + hp)),
            pl.BlockSpec((tb, LANES), lambda b, hp, t: (row(b, hp, t), 0)),
            pl.BlockSpec((None, tb // GDN_TILE, pair, GDN_TILE), lambda b, hp, t: (hp, row(b, hp, t), 0, 0)),
            pl.BlockSpec((1, LANES), lambda b, hp, t: (0, 0)),
        ],
        out_specs=pl.BlockSpec(wide, lambda b, hp, t: (row(b, hp, t), hp)),
        out_shape=jax.ShapeDtypeStruct((n, GDN_V_W), BF16),
        scratch_shapes=[
            pltpu.VMEM((pair, d, d), F32),
            pltpu.VMEM((pair, tb, d), BF16),
            pltpu.VMEM((pair, tb, d), F32),
            pltpu.VMEM((pair, tb, d), BF16),
            pltpu.VMEM((pair, tb, d), BF16),
            pltpu.VMEM((pair, tb, GDN_TILE), BF16),
        ],
        compiler_params=_params("parallel", "parallel", "arbitrary"),
        name="gated_delta_rule",
    )(qkv, qkv, qkv, proj, gates, g_rows, gdn_norm.reshape(1, LANES).astype(F32))


def _rms(x, gain):
    return x * lax.rsqrt(jnp.mean(x * x, axis=-1, keepdims=True) + EPS) * gain


def _out_proj_kernel(a_ref, b_ref, w_ref, x_ref, g_ref, o_ref, h_ref):
    ka = a_ref.shape[1]
    acc = jnp.dot(a_ref[...], w_ref[:ka, :], preferred_element_type=F32)
    acc += jnp.dot(b_ref[...], w_ref[ka:, :], preferred_element_type=F32)
    x = x_ref[...] + acc
    o_ref[...] = x
    h_ref[...] = _rms(x, g_ref[...]).astype(h_ref.dtype)


def out_proj_residual(attn, gdn, w_out, li, x, next_gain, tm=512):
    n, d = x.shape
    ka, kb = attn.shape[1], gdn.shape[1]
    row = lambda w: pl.BlockSpec((tm, w), lambda i: (i, 0))
    return pl.pallas_call(
        _out_proj_kernel,
        grid=(n // tm,),
        in_specs=[row(ka), row(kb),
                  pl.BlockSpec((None, ka + kb, d), lambda i: (li, 0, 0)),
                  row(d),
                  pl.BlockSpec((1, d), lambda i: (0, 0))],
        out_specs=[row(d), row(d)],
        out_shape=[jax.ShapeDtypeStruct((n, d), F32), jax.ShapeDtypeStruct((n, d), BF16)],
        compiler_params=_params("parallel"),
        name="out_proj_residual",
    )(attn, gdn, w_out, x, next_gain.reshape(1, d))


def _swiglu_tile(h, wg, wu):
    g = jnp.dot(h, wg, preferred_element_type=F32)
    u = jnp.dot(h, wu, preferred_element_type=F32)
    return g * jax.nn.sigmoid(g) * u


def _gate_up_kernel(h_ref, wg_ref, wu_ref, o_ref):
    o_ref[...] = _swiglu_tile(h_ref[...], wg_ref[...], wu_ref[...]).astype(o_ref.dtype)


def gate_up(h, w_gate, w_up, li, tm=512, tn=1024):
    m, k = h.shape
    f = w_gate.shape[2]
    return pl.pallas_call(
        _gate_up_kernel,
        grid=(f // tn, m // tm),
        in_specs=[
            pl.BlockSpec((tm, k), lambda j, i: (i, 0)),
            pl.BlockSpec((None, k, tn), lambda j, i: (li, 0, j)),
            pl.BlockSpec((None, k, tn), lambda j, i: (li, 0, j)),
        ],
        out_specs=pl.BlockSpec((tm, tn), lambda j, i: (i, j)),
        out_shape=jax.ShapeDtypeStruct((m, f), BF16),
        compiler_params=_params("parallel", "parallel"),
        name="swiglu_gate_up",
    )(h, w_gate, w_up)


def _gate_up_grouped_kernel(meta_ref, h_ref, wg_ref, wu_ref, o_ref):
    i = pl.program_id(1)
    used = meta_ref[pl.num_programs(1)]

    @pl.when(i < used)
    def _():
        o_ref[...] = _swiglu_tile(h_ref[...], wg_ref[...].astype(BF16), wu_ref[...].astype(BF16)).astype(o_ref.dtype)

    @pl.when(i >= used)
    def _():
        o_ref[...] = jnp.zeros_like(o_ref)


def gate_up_grouped(meta, h, w_gate, w_up, li, tm, tn=1024):
    m, k = h.shape
    f = w_gate.shape[3]
    nb = m // tm
    row = lambda j, i, meta: (jnp.minimum(i, meta[nb] - 1), 0)
    wmap = lambda j, i, meta: (li, meta[i], 0, j)
    return pl.pallas_call(
        _gate_up_grouped_kernel,
        grid_spec=pltpu.PrefetchScalarGridSpec(
            num_scalar_prefetch=1,
            grid=(f // tn, nb),
            in_specs=[
                pl.BlockSpec((tm, k), row),
                pl.BlockSpec((None, None, k, tn), wmap),
                pl.BlockSpec((None, None, k, tn), wmap),
            ],
            out_specs=pl.BlockSpec((tm, tn), lambda j, i, meta: (i, j)),
        ),
        out_shape=jax.ShapeDtypeStruct((m, f), BF16),
        compiler_params=_params("parallel", "arbitrary"),
        name="moe_gate_up",
    )(meta, h, w_gate, w_up)


def _down_kernel(a_ref, w_ref, x_ref, o_ref):
    o_ref[...] = x_ref[...] + jnp.dot(a_ref[...], w_ref[...], preferred_element_type=F32)


def down_residual(a, w_down, li, x, tm=512, tn=512):
    n, d = x.shape
    f = a.shape[1]
    return pl.pallas_call(
        _down_kernel,
        grid=(d // tn, n // tm),
        in_specs=[
            pl.BlockSpec((tm, f), lambda j, i: (i, 0)),
            pl.BlockSpec((None, f, tn), lambda j, i: (li, 0, j)),
            pl.BlockSpec((tm, tn), lambda j, i: (i, j)),
        ],
        out_specs=pl.BlockSpec((tm, tn), lambda j, i: (i, j)),
        out_shape=jax.ShapeDtypeStruct((n, d), F32),
        compiler_params=_params("parallel", "parallel"),
        name="swiglu_down_residual",
    )(a, w_down, x)


def _down_grouped_kernel(meta_ref, a_ref, w_ref, o_ref):
    i = pl.program_id(1)
    used = meta_ref[pl.num_programs(1)]

    @pl.when(i < used)
    def _():
        o_ref[...] = jnp.dot(a_ref[...], w_ref[...].astype(BF16), preferred_element_type=F32)

    @pl.when(i >= used)
    def _():
        o_ref[...] = jnp.zeros_like(o_ref)


def down_grouped(meta, a, w_down, li, tm, tn=512):
    m, f = a.shape
    d = w_down.shape[3]
    nb = m // tm
    return pl.pallas_call(
        _down_grouped_kernel,
        grid_spec=pltpu.PrefetchScalarGridSpec(
            num_scalar_prefetch=1,
            grid=(d // tn, nb),
            in_specs=[
                pl.BlockSpec((tm, f), lambda j, i, meta: (jnp.minimum(i, meta[nb] - 1), 0)),
                pl.BlockSpec((None, None, f, tn), lambda j, i, meta: (li, meta[i], 0, j)),
            ],
            out_specs=pl.BlockSpec((tm, tn), lambda j, i, meta: (i, j)),
        ),
        out_shape=jax.ShapeDtypeStruct((m, d), F32),
        compiler_params=_params("parallel", "arbitrary"),
        name="moe_down",
    )(meta, a, w_down)


def _router_kernel(x_ref, g_ref, w_ref, h_ref, idx_ref, gate_ref):
    x = x_ref[...]
    h = x * lax.rsqrt(jnp.mean(x * x, axis=-1, keepdims=True) + EPS) * g_ref[...]
    h_ref[...] = h.astype(h_ref.dtype)
    w = w_ref[...]
    hh, wh = h.astype(BF16), w.astype(BF16)
    hl, wl = (h - hh.astype(F32)).astype(BF16), (w - wh.astype(F32)).astype(BF16)
    logits = jnp.dot(hh, wh, preferred_element_type=F32) + (
        jnp.dot(hh, wl, preferred_element_type=F32) + jnp.dot(hl, wh, preferred_element_type=F32))
    lane = lax.broadcasted_iota(jnp.int32, logits.shape, 1)
    logits = jnp.where(lane < N_EXPERTS, logits, -jnp.inf)
    m1 = jnp.max(logits, axis=-1, keepdims=True)
    i1 = jnp.min(jnp.where(logits == m1, lane, LANES), axis=-1, keepdims=True)
    rest = jnp.where(lane == i1, -jnp.inf, logits)
    m2 = jnp.max(rest, axis=-1, keepdims=True)
    i2 = jnp.min(jnp.where(rest == m2, lane, LANES), axis=-1, keepdims=True)
    e2 = jnp.exp(m2 - m1)
    denom = 1.0 + e2
    idx_ref[...] = jnp.where(lane == 0, i1, jnp.where(lane == 1, i2, 0))
    gate_ref[...] = jnp.where(lane == 0, 1.0 / denom, jnp.where(lane == 1, e2 / denom, 0.0))


def moe_router(x, gain, w_router, tm=256):
    n, d = x.shape
    w = jnp.zeros((d, LANES), F32).at[:, :N_EXPERTS].set(w_router.astype(F32))
    return pl.pallas_call(
        _router_kernel,
        grid=(n // tm,),
        in_specs=[pl.BlockSpec((tm, d), lambda i: (i, 0)),
                  pl.BlockSpec((1, d), lambda i: (0, 0)),
                  pl.BlockSpec((d, LANES), lambda i: (0, 0))],
        out_specs=[pl.BlockSpec((tm, d), lambda i: (i, 0)),
                   pl.BlockSpec((tm, LANES), lambda i: (i, 0)),
                   pl.BlockSpec((tm, LANES), lambda i: (i, 0))],
        out_shape=[jax.ShapeDtypeStruct((n, d), BF16),
                   jax.ShapeDtypeStruct((n, LANES), jnp.int32),
                   jax.ShapeDtypeStruct((n, LANES), F32)],
        compiler_params=_params("parallel"),
        name="moe_router",
    )(x, gain.reshape(1, d), w)


def _combine_kernel(x_ref, a_ref, b_ref, g_ref, gain_ref, o_ref, h_ref):
    g = g_ref[...]
    x = x_ref[...] + (a_ref[...] * g[:, 0:1] + b_ref[...] * g[:, 1:2])
    o_ref[...] = x
    h_ref[...] = _rms(x, gain_ref[...]).astype(h_ref.dtype)


def moe_combine(x, ya, yb, gates, next_gain, tm=512):
    n, d = x.shape
    row = pl.BlockSpec((tm, d), lambda i: (i, 0))
    return pl.pallas_call(
        _combine_kernel,
        grid=(n // tm,),
        in_specs=[row, row, row, pl.BlockSpec((tm, LANES), lambda i: (i, 0)), pl.BlockSpec((1, d), lambda i: (0, 0))],
        out_specs=[row, row],
        out_shape=[jax.ShapeDtypeStruct((n, d), F32), jax.ShapeDtypeStruct((n, d), BF16)],
        compiler_params=_params("parallel"),
        name="moe_combine",
    )(x, ya, yb, gates, next_gain.reshape(1, d))


def moe_block(x, gain, w_router, w_gate, w_up, w_down, li, next_gain, tm=512):
    n, d = x.shape
    h, idx, gates = moe_router(x, gain, w_router)
    flat_e = idx[:, :TOP_K].reshape(-1)
    na = n * TOP_K
    onehot = (flat_e[None, :] == jnp.arange(N_EXPERTS, dtype=jnp.int32)[:, None]).astype(jnp.int32)
    csum = jnp.cumsum(onehot, axis=1)
    counts = csum[:, -1]
    padded = (counts + tm - 1) // tm * tm
    pad_end = jnp.cumsum(padded)
    pad_start = pad_end - padded
    dest = jnp.sum(onehot * (csum - 1 + pad_start[:, None]), axis=0)
    n_blocks = na // tm + N_EXPERTS
    cap = n_blocks * tm
    slot_tok = jnp.zeros((cap,), jnp.int32).at[dest].set(jnp.arange(na, dtype=jnp.int32) // TOP_K)
    block_e = jnp.minimum(
        jnp.searchsorted(pad_end, jnp.arange(n_blocks, dtype=jnp.int32) * tm, side='right'),
        N_EXPERTS - 1).astype(jnp.int32)
    meta = jnp.concatenate([block_e, (pad_end[-1:] // tm).astype(jnp.int32)])
    rows = lambda a, idx: a.at[idx].get(mode="promise_in_bounds")
    xs = rows(h, slot_tok)
    act = gate_up_grouped(meta, xs, w_gate, w_up, li, tm)
    ys = down_grouped(meta, act, w_down, li, tm)
    dest2 = dest.reshape(n, TOP_K)
    return moe_combine(x, rows(ys, dest2[:, 0]), rows(ys, dest2[:, 1]), gates, next_gain)


def _ple_kernel(h_ref, wg_ref, p_ref, wp_ref, x_ref, gain_ref, o_ref, hn_ref):
    gate = jax.nn.sigmoid(jnp.dot(h_ref[...], wg_ref[...], preferred_element_type=F32))
    proj = jnp.dot(p_ref[...].astype(BF16), wp_ref[...].astype(BF16), preferred_element_type=F32)
    x = x_ref[...] + proj * gate
    o_ref[...] = x
    hn_ref[...] = _rms(x, gain_ref[...]).astype(hn_ref.dtype)


def ple_residual(h, w_gate, p, w_proj, li, x, next_gain, next_dtype, tm=512):
    n, d = x.shape
    kp = p.shape[2]
    row = lambda w: pl.BlockSpec((tm, w), lambda i: (i, 0))
    return pl.pallas_call(
        _ple_kernel,
        grid=(n // tm,),
        in_specs=[
            row(d),
            pl.BlockSpec((None, d, d), lambda i: (li, 0, 0)),
            pl.BlockSpec((None, tm, kp), lambda i: (li, i, 0)),
            pl.BlockSpec((None, kp, d), lambda i: (li, 0, 0)),
            row(d),
            pl.BlockSpec((1, d), lambda i: (0, 0)),
        ],
        out_specs=[row(d), row(d)],
        out_shape=[jax.ShapeDtypeStruct((n, d), F32), jax.ShapeDtypeStruct((n, d), next_dtype)],
        compiler_params=_params("parallel"),
        name="ple_residual",
    )(h, w_gate, p, w_proj, x, next_gain.reshape(1, d))


def hybrid_mixer_residual(x, h, w_in, li, conv_w, a_log, dt_bias, gdn_norm, sinks, bias, w_out, next_gain,
                          batch, seq):
    n = x.shape[0]
    proj = matmul_cols_outer(h, w_in, li, IN_MAIN_W, tm=512, tn=IN_MAIN_W // 2)
    w_tail = jnp.zeros((D_MODEL, LANES), BF16).at[:, :IN_TAIL_W].set(w_in[li, :, IN_MAIN_W:])
    attn = swa_attention(proj, sinks.astype(F32), bias, batch, seq)
    qkv = gdn_prep(proj, conv_w, seq)
    gates = gdn_gates(h, w_tail, a_log, dt_bias)
    pair = GDN_V_HEADS // GDN_QK_HEADS
    g_rows = gates[:, GDN_V_HEADS:2 * GDN_V_HEADS].T.reshape(GDN_QK_HEADS, pair, n // GDN_TILE, GDN_TILE)
    g_rows = g_rows.transpose(0, 2, 1, 3)
    gdn = gdn_mixer(qkv, proj, gates, g_rows, gdn_norm, batch, seq)
    return out_proj_residual(attn, gdn, w_out, li, x, next_gain)


def kernel(x, p, w_in, conv_w, a_log, dt_bias, gdn_norm, attn_sinks, rel_bias_table, w_out, norm_mix, norm_ffn, w_dense_gate, w_dense_up, w_dense_down, w_router, w_exp_gate, w_exp_up, w_exp_down, norm_ple, w_ple_gate, w_ple_proj, norm_final):
    batch, seq, d = x.shape
    depth = w_in.shape[0]
    n = batch * seq
    out_dtype = x.dtype
    x = x.reshape(n, d).astype(F32)
    assert WINDOW == SWA_BLOCK
    bias = folded_relative_bias(rel_bias_table)
    p = p.reshape(depth, n, PLE_DIM)
    w_in, w_out, w_ple_gate = w_in.astype(BF16), w_out.astype(BF16), w_ple_gate.astype(BF16)
    w_dense_gate, w_dense_up, w_dense_down = (w.astype(BF16) for w in (w_dense_gate, w_dense_up, w_dense_down))
    h = rmsnorm_rows(x, norm_mix[0], BF16)
    for i in range(depth):
        x, hf = hybrid_mixer_residual(x, h, w_in, i, conv_w[i], a_log[i], dt_bias[i], gdn_norm[i],
                                      attn_sinks[i], bias, w_out, norm_ffn[i], batch, seq)
        if i % 2 == 0:
            act = gate_up(hf, w_dense_gate, w_dense_up, i // 2)
            x = down_residual(act, w_dense_down, i // 2, x)
            hp = rmsnorm_rows(x, norm_ple[i], BF16)
        else:
            x, hp = moe_block(x, norm_ffn[i], w_router[i // 2], w_exp_gate, w_exp_up, w_exp_down, i // 2,
                              norm_ple[i])
        last = i == depth - 1
        x, h = ple_residual(hp, w_ple_gate, p, w_ple_proj, i, x, norm_final if last else norm_mix[i + 1],
                            out_dtype if last else BF16)
    return h.reshape(batch, seq, d)
```

```python
import functools
import math

import jax
import jax.numpy as jnp
from jax import lax
from jax.experimental import pallas as pl
from jax.experimental.pallas import tpu as pltpu

F32 = jnp.float32
BF16 = jnp.bfloat16

D_MODEL = 2048
PLE_DIM = 256
EPS = 1e-6
NEG_INF = -1e30

SWA_HEADS = 16
SWA_KV_HEADS = 2
SWA_HEAD_DIM = 64
SWA_GROUP = SWA_HEADS // SWA_KV_HEADS
WINDOW = 128
SWA_BLOCK = 128
REL_BUCKETS = 32
REL_MAX_DIST = 128

GDN_QK_HEADS = 4
GDN_V_HEADS = 8
GDN_HEAD_DIM = 128
GDN_CONV = 4
GDN_TILE = 256
GDN_BLOCK = 256

SWA_Q_W = SWA_HEADS * SWA_HEAD_DIM
SWA_KV_W = SWA_KV_HEADS * SWA_HEAD_DIM
GDN_QK_W = GDN_QK_HEADS * GDN_HEAD_DIM
GDN_V_W = GDN_V_HEADS * GDN_HEAD_DIM
GDN_CONV_CH = 2 * GDN_QK_W + GDN_V_W
MIX_WIDTH = SWA_Q_W + GDN_V_W
IN_MAIN_W = SWA_Q_W + 2 * SWA_KV_W + GDN_CONV_CH + GDN_V_W
IN_TAIL_W = 2 * GDN_V_HEADS

D_FF = 7 * D_MODEL // 2
N_EXPERTS = 8
TOP_K = 2

LANES = 128
SUBLANES = 8
VMEM_LIMIT = 56 * 1024 * 1024

COL_K = SWA_Q_W // LANES
COL_V = COL_K + SWA_KV_W // LANES
COL_GDN = COL_V + SWA_KV_W // LANES
COL_Z = COL_GDN + GDN_CONV_CH // LANES


def _params(*sem):
    return pltpu.CompilerParams(dimension_semantics=sem, vmem_limit_bytes=VMEM_LIMIT)


def _rms_kernel(x_ref, g_ref, o_ref):
    x = x_ref[...]
    y = x * lax.rsqrt(jnp.mean(x * x, axis=-1, keepdims=True) + EPS)
    o_ref[...] = (y * g_ref[...]).astype(o_ref.dtype)


def rmsnorm_rows(x, gain, out_dtype, tm=512):
    n, d = x.shape
    return pl.pallas_call(
        _rms_kernel,
        grid=(n // tm,),
        in_specs=[pl.BlockSpec((tm, d), lambda i: (i, 0)), pl.BlockSpec((1, d), lambda i: (0, 0))],
        out_specs=pl.BlockSpec((tm, d), lambda i: (i, 0)),
        out_shape=jax.ShapeDtypeStruct((n, d), out_dtype),
        compiler_params=_params("parallel"),
        name="rmsnorm",
    )(x, gain.reshape(1, d))


def _mm_kernel(a_ref, w_ref, o_ref):
    o_ref[...] = jnp.dot(a_ref[...], w_ref[...], preferred_element_type=F32).astype(o_ref.dtype)


def matmul_cols_outer(a, w, li, n_out, tm, tn, out_dtype=F32):
    m, k = a.shape
    return pl.pallas_call(
        _mm_kernel,
        grid=(n_out // tn, m // tm),
        in_specs=[pl.BlockSpec((tm, k), lambda j, i: (i, 0)),
                  pl.BlockSpec((None, k, tn), lambda j, i: (li, 0, j))],
        out_specs=pl.BlockSpec((tm, tn), lambda j, i: (i, j)),
        out_shape=jax.ShapeDtypeStruct((m, n_out), out_dtype),
        compiler_params=_params("parallel", "parallel"),
        name="in_proj",
    )(a, w)


def _attn_kernel(sink_ref, q_ref, kc_ref, kp_ref, vc_ref, vp_ref, bias_ref, o_ref):
    n = pl.program_id(1)
    kj = lax.broadcasted_iota(jnp.int32, (SWA_BLOCK, SWA_BLOCK), 0)
    qi = lax.broadcasted_iota(jnp.int32, (SWA_BLOCK, SWA_BLOCK), 1)
    in_cur = kj <= qi
    valid = in_cur | (n > 0)
    scale = SWA_HEAD_DIM ** -0.5
    for g in range(SWA_KV_HEADS):
        cols = slice(g * SWA_HEAD_DIM, (g + 1) * SWA_HEAD_DIM)
        kw = jnp.concatenate([kp_ref[:, cols], kc_ref[:, cols]], axis=0).astype(BF16)
        vt = jnp.concatenate([vp_ref[:, cols], vc_ref[:, cols]], axis=0).T.astype(BF16)
        heads = [g * SWA_GROUP + hh for hh in range(SWA_GROUP)]
        qs = [(q_ref[:, h * SWA_HEAD_DIM:(h + 1) * SWA_HEAD_DIM] * scale).astype(BF16) for h in heads]
        boths = [lax.dot_general(kw, qh, _NT, preferred_element_type=F32) for qh in qs]
        logits = [jnp.where(valid, jnp.where(in_cur, b[SWA_BLOCK:], b[:SWA_BLOCK]) + bias_ref[h], NEG_INF)
                  for h, b in zip(heads, boths)]
        ms = [jnp.maximum(jnp.max(l, axis=0, keepdims=True), sink_ref[h]) for h, l in zip(heads, logits)]
        es = [jnp.exp(l - m) for l, m in zip(logits, ms)]
        denoms = [jnp.sum(e, axis=0, keepdims=True) + jnp.exp(sink_ref[h] - m) for h, e, m in zip(heads, es, ms)]
        unfolded = [jnp.concatenate([jnp.where(in_cur, 0.0, e), jnp.where(in_cur, e, 0.0)], axis=0).astype(BF16)
                    for e in es]
        outs = [jnp.dot(vt, u, preferred_element_type=F32) * (1.0 / d) for u, d in zip(unfolded, denoms)]
        for pair in range(SWA_GROUP // 2):
            h0 = heads[2 * pair]
            both_heads = jnp.concatenate(outs[2 * pair:2 * pair + 2], axis=0).T
            o_ref[:, h0 * SWA_HEAD_DIM:(h0 + 2) * SWA_HEAD_DIM] = both_heads.astype(o_ref.dtype)


def folded_relative_bias(rel_table):
    kj = jnp.arange(SWA_BLOCK, dtype=jnp.int32)[:, None]
    qi = jnp.arange(SWA_BLOCK, dtype=jnp.int32)[None, :]
    dist = jnp.where(kj <= qi, qi - kj, qi + SWA_BLOCK - kj)
    max_exact = REL_BUCKETS // 2
    d = jnp.maximum(dist, 0)
    log_ratio = jnp.log(jnp.maximum(d, 1).astype(F32) / max_exact) / math.log(REL_MAX_DIST / max_exact)
    large = jnp.minimum(max_exact + (log_ratio * (REL_BUCKETS - max_exact)).astype(jnp.int32), REL_BUCKETS - 1)
    bucket = jnp.where(d < max_exact, d, large)
    onehot = (bucket[None] == jnp.arange(REL_BUCKETS, dtype=jnp.int32)[:, None, None]).astype(F32)
    return jnp.einsum('bh,bjq->hjq', rel_table.astype(F32), onehot, precision=lax.Precision.HIGHEST)


def swa_attention(proj, sinks, bias, batch, seq):
    n = proj.shape[0]
    nb = seq // SWA_BLOCK
    cur = lambda b, i, s: b * nb + i
    prev = lambda b, i, s: jnp.maximum(b * nb + i - 1, 0)
    blk = (SWA_BLOCK, LANES)
    return pl.pallas_call(
        _attn_kernel,
        grid_spec=pltpu.PrefetchScalarGridSpec(
            num_scalar_prefetch=1,
            grid=(batch, nb),
            in_specs=[
                pl.BlockSpec((SWA_BLOCK, SWA_Q_W), lambda b, i, s: (cur(b, i, s), 0)),
                pl.BlockSpec(blk, lambda b, i, s: (cur(b, i, s), COL_K)),
                pl.BlockSpec(blk, lambda b, i, s: (prev(b, i, s), COL_K)),
                pl.BlockSpec(blk, lambda b, i, s: (cur(b, i, s), COL_V)),
                pl.BlockSpec(blk, lambda b, i, s: (prev(b, i, s), COL_V)),
                pl.BlockSpec((SWA_HEADS, SWA_BLOCK, SWA_BLOCK), lambda b, i, s: (0, 0, 0)),
            ],
            out_specs=pl.BlockSpec((SWA_BLOCK, SWA_Q_W), lambda b, i, s: (cur(b, i, s), 0)),
        ),
        out_shape=jax.ShapeDtypeStruct((n, SWA_Q_W), BF16),
        compiler_params=_params("parallel", "parallel"),
        name="swa_attention",
    )(sinks, proj, proj, proj, proj, proj, bias)


def _causal_conv_silu(cur, prev, w):
    tm = cur.shape[0]
    xp = jnp.concatenate([prev, cur], axis=0)
    first = SUBLANES - (GDN_CONV - 1)
    acc = xp[first:first + tm] * w[0:1]
    for k in range(1, GDN_CONV):
        acc = acc + xp[first + k:first + k + tm] * w[k:k + 1]
    return acc * jax.nn.sigmoid(acc)


def _l2norm(y):
    return y * lax.rsqrt(jnp.sum(y * y, axis=-1, keepdims=True) + EPS)


def _gdn_gate_kernel(h_ref, w_ref, alog_ref, dtb_ref, o_ref):
    tm = h_ref.shape[0]
    t = jnp.dot(h_ref[...], w_ref[...], preferred_element_type=F32)
    lane = lax.broadcasted_iota(jnp.int32, t.shape, 1)
    beta = jax.nn.sigmoid(t)
    g = -jnp.exp(alog_ref[...]) * jax.nn.softplus(t + dtb_ref[...])
    g = jnp.where((lane >= GDN_V_HEADS) & (lane < 2 * GDN_V_HEADS), g, 0.0)
    blk = GDN_BLOCK
    r = lax.broadcasted_iota(jnp.int32, (blk, blk), 0)
    s = lax.broadcasted_iota(jnp.int32, (blk, blk), 1)
    tri = jnp.where(s <= r, 1.0, 0.0).astype(BF16)
    sums = []
    for b in range(tm // blk):
        gb = g[b * blk:(b + 1) * blk]
        hi = gb.astype(BF16)
        rest = gb - hi.astype(F32)
        mid = rest.astype(BF16)
        lo = (rest - mid.astype(F32)).astype(BF16)
        sums.append(jnp.dot(tri, hi, preferred_element_type=F32)
                    + (jnp.dot(tri, mid, preferred_element_type=F32) + jnp.dot(tri, lo, preferred_element_type=F32)))
    gcum = jnp.concatenate(sums, axis=0)
    o_ref[...] = jnp.where(lane < GDN_V_HEADS, beta, gcum)


def gdn_gates(h, w_tail, a_log, dt_bias, tm=512):
    n, k = h.shape
    pad = lambda v: jnp.zeros((1, LANES), F32).at[0, GDN_V_HEADS:2 * GDN_V_HEADS].set(v.astype(F32))
    return pl.pallas_call(
        _gdn_gate_kernel,
        grid=(n // tm,),
        in_specs=[pl.BlockSpec((tm, k), lambda i: (i, 0)),
                  pl.BlockSpec((k, LANES), lambda i: (0, 0)),
                  pl.BlockSpec((1, LANES), lambda i: (0, 0)),
                  pl.BlockSpec((1, LANES), lambda i: (0, 0))],
        out_specs=pl.BlockSpec((tm, LANES), lambda i: (i, 0)),
        out_shape=jax.ShapeDtypeStruct((n, LANES), F32),
        compiler_params=_params("parallel"),
        name="gdn_gates",
    )(h, w_tail, pad(a_log), pad(dt_bias))


def _bdot(a, b, dims=(((1,), (0,)), ((), ()))):
    return lax.dot_general(a.astype(BF16), b.astype(BF16), dims, preferred_element_type=F32)


_NT = (((1,), (1,)), ((), ()))
_TN = (((0,), (0,)), ((), ()))


def _unit_lower_inverses(lows, block):
    r_dim = lows[0].shape[0]
    r = lax.broadcasted_iota(jnp.int32, (r_dim, r_dim), 0)
    s = lax.broadcasted_iota(jnp.int32, (r_dim, r_dim), 1)
    level = 31 - lax.clz(r ^ s)
    size = 1
    step = 0
    xs = None
    while size < block:
        sel = level == step
        step += 1
        offs = [jnp.where(sel, low, 0.0) for low in lows]
        if size == 1:
            eye = jnp.where(r == s, 1.0, 0.0).astype(F32)
            xs = [eye - off for off in offs]
        else:
            xo = [_bdot(x, off) for x, off in zip(xs, offs)]
            xs = [x - _bdot(y, x) for x, y in zip(xs, xo)]
        size *= 2
    return xs


def _gdn_kernel(qin_ref, qprev_ref, kin_ref, kprev_ref, vin_ref, vprev_ref, wq_ref, wk_ref, wv_ref,
                z_ref, gb_ref, grow_ref, norm_ref, o_ref,
                state_ref, q_ref, k_ref, v_ref, w_s, u_s, qd_s, kd_s, qk_s):
    hp = pl.program_id(1)
    t = pl.program_id(2)
    tb = qin_ref.shape[0]
    rt = GDN_TILE
    blk = GDN_BLOCK
    pair = GDN_V_HEADS // GDN_QK_HEADS
    d = GDN_HEAD_DIM

    @pl.when(t == 0)
    def _():
        state_ref[...] = jnp.zeros_like(state_ref)

    halo = lambda ref: jnp.where(t == 0, 0.0, ref[...])
    q_ref[...] = _l2norm(_causal_conv_silu(qin_ref[...], halo(qprev_ref), wq_ref[...]))
    k_ref[...] = _l2norm(_causal_conv_silu(kin_ref[...], halo(kprev_ref), wk_ref[...]))
    v_all = _causal_conv_silu(vin_ref[...], halo(vprev_ref), wv_ref[...])
    for hh in range(pair):
        v_ref[hh] = v_all[:, hh * d:(hh + 1) * d]

    lane = lax.broadcasted_iota(jnp.int32, (tb, LANES), 1)
    gb = gb_ref[...]
    r = lax.broadcasted_iota(jnp.int32, (rt, rt), 0)
    s = lax.broadcasted_iota(jnp.int32, (rt, rt), 1)
    same = (r // blk) == (s // blk)
    causal = (s <= r) & same
    strict = (s < r) & same

    g_heads = []
    beta_heads = []
    for hh in range(pair):
        h = pair * hp + hh
        beta_heads.append(jnp.sum(jnp.where(lane == h, gb, 0.0), axis=-1, keepdims=True))
        g_heads.append(jnp.sum(jnp.where(lane == h + GDN_V_HEADS, gb, 0.0), axis=-1, keepdims=True))

    probs = [(hh, ti, slice(ti * rt, (ti + 1) * rt)) for hh in range(pair) for ti in range(tb // rt)]
    qs = {ti: q_ref[ti * rt:(ti + 1) * rt, :] * (d ** -0.5) for ti in range(tb // rt)}
    ks = {ti: k_ref[ti * rt:(ti + 1) * rt, :] for ti in range(tb // rt)}
    betas = [beta_heads[hh][rows] for hh, ti, rows in probs]
    gs = [g_heads[hh][rows] for hh, ti, rows in probs]
    egs = [jnp.exp(g) for g in gs]
    kbs = [ks[ti] * beta for (hh, ti, rows), beta in zip(probs, betas)]
    kqs = [_bdot(jnp.concatenate([kb, qs[ti]], axis=0), ks[ti], _NT) for (hh, ti, rows), kb in zip(probs, kbs)]
    decays = [jnp.where(causal, jnp.exp(jnp.where(causal, g - grow_ref[ti, hh:hh + 1, :], 0.0)), 0.0)
              for (hh, ti, rows), g in zip(probs, gs)]
    lows = [jnp.where(strict, kq[:rt] * decay, 0.0) for kq, decay in zip(kqs, decays)]
    for (hh, ti, rows), kq, decay, g, eg in zip(probs, kqs, decays, gs, egs):
        qk_s[hh, rows, :] = (kq[rt:] * decay).astype(BF16)
        g_last = jnp.concatenate(
            [jnp.broadcast_to(g[(c + 1) * blk - 1:(c + 1) * blk], (blk, 1)) for c in range(rt // blk)], axis=0)
        qd_s[hh, rows, :] = (qs[ti] * eg).astype(BF16)
        kd_s[hh, rows, :] = (ks[ti] * jnp.exp(g_last - g)).astype(BF16)
    rhss = [jnp.concatenate([v_ref[hh, rows, :] * beta, kb * eg], axis=1).astype(BF16)
            for (hh, ti, rows), beta, kb, eg in zip(probs, betas, kbs, egs)]

    tinvs = _unit_lower_inverses(lows, blk)
    uws = [_bdot(tinv, rhs) for tinv, rhs in zip(tinvs, rhss)]
    for (hh, ti, rows), uw in zip(probs, uws):
        u_s[hh, rows, :] = uw[:, :d]
        w_s[hh, rows, :] = uw[:, d:].astype(BF16)

    heads = range(pair)
    for c in range(tb // blk):
        rows = slice(c * blk, (c + 1) * blk)
        off = (c * blk) % rt
        states = [state_ref[hh] for hh in heads]
        ws = [_bdot(jnp.concatenate([w_s[hh, rows, :], qd_s[hh, rows, :]], axis=0), states[hh]) for hh in heads]
        v_new = [u_s[hh, rows, :] - ws[hh][:blk] for hh in heads]
        outs = [ws[hh][blk:] + _bdot(qk_s[hh, rows, off:off + blk], v_new[hh]) for hh in heads]
        for hh in heads:
            g_end = g_heads[hh][(c + 1) * blk - 1:(c + 1) * blk]
            state_ref[hh] = states[hh] * jnp.exp(g_end) + _bdot(kd_s[hh, rows, :], v_new[hh], _TN)
        for hh in heads:
            o = outs[hh]
            o = o * lax.rsqrt(jnp.mean(o * o, axis=-1, keepdims=True) + EPS) * norm_ref[...]
            zc = z_ref[rows, hh * d:(hh + 1) * d]
            o_ref[rows, hh * d:(hh + 1) * d] = (o * (zc * jax.nn.sigmoid(zc))).astype(o_ref.dtype)


def gdn_mixer(proj, conv_w, gates, g_rows, gdn_norm, batch, seq, tb=512):
    n = proj.shape[0]
    nt = seq // tb
    pair = GDN_V_HEADS // GDN_QK_HEADS
    d = GDN_HEAD_DIM
    row = lambda b, hp, t: b * nt + t
    before = lambda b, hp, t: jnp.maximum(row(b, hp, t) * (tb // SUBLANES) - 1, 0)
    wide = (tb, pair * d)
    q_col = COL_GDN
    k_col = COL_GDN + GDN_QK_HEADS
    v_col = (COL_GDN + 2 * GDN_QK_HEADS) // pair
    z_col = COL_Z // pair
    return pl.pallas_call(
        _gdn_kernel,
        grid=(batch, GDN_QK_HEADS, nt),
        in_specs=[
            pl.BlockSpec((tb, d), lambda b, hp, t: (row(b, hp, t), q_col + hp)),
            pl.BlockSpec((SUBLANES, d), lambda b, hp, t: (before(b, hp, t), q_col + hp)),
            pl.BlockSpec((tb, d), lambda b, hp, t: (row(b, hp, t), k_col + hp)),
            pl.BlockSpec((SUBLANES, d), lambda b, hp, t: (before(b, hp, t), k_col + hp)),
            pl.BlockSpec(wide, lambda b, hp, t: (row(b, hp, t), v_col + hp)),
            pl.BlockSpec((SUBLANES, pair * d), lambda b, hp, t: (before(b, hp, t), v_col + hp)),
            pl.BlockSpec((GDN_CONV, d), lambda b, hp, t: (0, hp)),
            pl.BlockSpec((GDN_CONV, d), lambda b, hp, t: (0, GDN_QK_HEADS + hp)),
            pl.BlockSpec((GDN_CONV, pair * d), lambda b, hp, t: (0, 2 * GDN_QK_HEADS // pair + hp)),
            pl.BlockSpec(wide, lambda b, hp, t: (row(b, hp, t), z_col + hp)),
            pl.BlockSpec((tb, LANES), lambda b, hp, t: (row(b, hp, t), 0)),
            pl.BlockSpec((None, tb // GDN_TILE, pair, GDN_TILE), lambda b, hp, t: (hp, row(b, hp, t), 0, 0)),
            pl.BlockSpec((1, LANES), lambda b, hp, t: (0, 0)),
        ],
        out_specs=pl.BlockSpec(wide, lambda b, hp, t: (row(b, hp, t), hp)),
        out_shape=jax.ShapeDtypeStruct((n, GDN_V_W), BF16),
        scratch_shapes=[
            pltpu.VMEM((pair, d, d), F32),
            pltpu.VMEM((tb, d), F32),
            pltpu.VMEM((tb, d), F32),
            pltpu.VMEM((pair, tb, d), F32),
            pltpu.VMEM((pair, tb, d), BF16),
            pltpu.VMEM((pair, tb, d), F32),
            pltpu.VMEM((pair, tb, d), BF16),
            pltpu.VMEM((pair, tb, d), BF16),
            pltpu.VMEM((pair, tb, GDN_TILE), BF16),
        ],
        compiler_params=_params("parallel", "parallel", "arbitrary"),
        name="gated_delta_rule",
    )(proj, proj, proj, proj, proj, proj, conv_w, conv_w, conv_w, proj, gates, g_rows,
      gdn_norm.reshape(1, LANES).astype(F32))


def _rms(x, gain):
    return x * lax.rsqrt(jnp.mean(x * x, axis=-1, keepdims=True) + EPS) * gain


def _out_proj_kernel(a_ref, b_ref, w_ref, x_ref, g_ref, o_ref, h_ref):
    ka = a_ref.shape[1]
    acc = jnp.dot(a_ref[...], w_ref[:ka, :], preferred_element_type=F32)
    acc += jnp.dot(b_ref[...], w_ref[ka:, :], preferred_element_type=F32)
    x = x_ref[...] + acc
    o_ref[...] = x
    h_ref[...] = _rms(x, g_ref[...]).astype(h_ref.dtype)


def out_proj_residual(attn, gdn, w_out, li, x, next_gain, tm=512):
    n, d = x.shape
    ka, kb = attn.shape[1], gdn.shape[1]
    row = lambda w: pl.BlockSpec((tm, w), lambda i: (i, 0))
    return pl.pallas_call(
        _out_proj_kernel,
        grid=(n // tm,),
        in_specs=[row(ka), row(kb),
                  pl.BlockSpec((None, ka + kb, d), lambda i: (li, 0, 0)),
                  row(d),
                  pl.BlockSpec((1, d), lambda i: (0, 0))],
        out_specs=[row(d), row(d)],
        out_shape=[jax.ShapeDtypeStruct((n, d), F32), jax.ShapeDtypeStruct((n, d), BF16)],
        compiler_params=_params("parallel"),
        name="out_proj_residual",
    )(attn, gdn, w_out, x, next_gain.reshape(1, d))


def _swiglu_tile(h, wg, wu):
    g = jnp.dot(h, wg, preferred_element_type=F32)
    u = jnp.dot(h, wu, preferred_element_type=F32)
    return g * jax.nn.sigmoid(g) * u


def _gate_up_kernel(h_ref, wg_ref, wu_ref, o_ref):
    o_ref[...] = _swiglu_tile(h_ref[...], wg_ref[...], wu_ref[...]).astype(o_ref.dtype)


def gate_up(h, w_gate, w_up, li, tm=512, tn=1024):
    m, k = h.shape
    f = w_gate.shape[2]
    return pl.pallas_call(
        _gate_up_kernel,
        grid=(f // tn, m // tm),
        in_specs=[
            pl.BlockSpec((tm, k), lambda j, i: (i, 0)),
            pl.BlockSpec((None, k, tn), lambda j, i: (li, 0, j)),
            pl.BlockSpec((None, k, tn), lambda j, i: (li, 0, j)),
        ],
        out_specs=pl.BlockSpec((tm, tn), lambda j, i: (i, j)),
        out_shape=jax.ShapeDtypeStruct((m, f), BF16),
        compiler_params=_params("parallel", "parallel"),
        name="swiglu_gate_up",
    )(h, w_gate, w_up)


def _gate_up_grouped_kernel(meta_ref, h_ref, wg_ref, wu_ref, o_ref):
    i = pl.program_id(1)
    used = meta_ref[pl.num_programs(1)]

    @pl.when(i < used)
    def _():
        o_ref[...] = _swiglu_tile(h_ref[...].astype(BF16), wg_ref[...].astype(BF16),
                                  wu_ref[...].astype(BF16)).astype(o_ref.dtype)

    @pl.when(i >= used)
    def _():
        o_ref[...] = jnp.zeros_like(o_ref)


def gate_up_grouped(meta, h, w_gate, w_up, li, tm, tn=1024):
    m, k = h.shape
    f = w_gate.shape[3]
    nb = m // tm
    row = lambda j, i, meta: (jnp.minimum(i, meta[nb] - 1), 0)
    wmap = lambda j, i, meta: (li, meta[i], 0, j)
    return pl.pallas_call(
        _gate_up_grouped_kernel,
        grid_spec=pltpu.PrefetchScalarGridSpec(
            num_scalar_prefetch=1,
            grid=(f // tn, nb),
            in_specs=[
                pl.BlockSpec((tm, k), row),
                pl.BlockSpec((None, None, k, tn), wmap),
                pl.BlockSpec((None, None, k, tn), wmap),
            ],
            out_specs=pl.BlockSpec((tm, tn), lambda j, i, meta: (i, j)),
        ),
        out_shape=jax.ShapeDtypeStruct((m, f), BF16),
        compiler_params=_params("parallel", "arbitrary"),
        name="moe_gate_up",
    )(meta, h, w_gate, w_up)


def _down_kernel(a_ref, w_ref, x_ref, o_ref):
    o_ref[...] = x_ref[...] + jnp.dot(a_ref[...], w_ref[...], preferred_element_type=F32)


def down_residual(a, w_down, li, x, tm=512, tn=512):
    n, d = x.shape
    f = a.shape[1]
    return pl.pallas_call(
        _down_kernel,
        grid=(d // tn, n // tm),
        in_specs=[
            pl.BlockSpec((tm, f), lambda j, i: (i, 0)),
            pl.BlockSpec((None, f, tn), lambda j, i: (li, 0, j)),
            pl.BlockSpec((tm, tn), lambda j, i: (i, j)),
        ],
        out_specs=pl.BlockSpec((tm, tn), lambda j, i: (i, j)),
        out_shape=jax.ShapeDtypeStruct((n, d), F32),
        compiler_params=_params("parallel", "parallel"),
        name="swiglu_down_residual",
    )(a, w_down, x)


def _down_grouped_kernel(meta_ref, a_ref, w_ref, o_ref):
    i = pl.program_id(1)
    used = meta_ref[pl.num_programs(1)]

    @pl.when(i < used)
    def _():
        o_ref[...] = jnp.dot(a_ref[...], w_ref[...].astype(BF16), preferred_element_type=F32)

    @pl.when(i >= used)
    def _():
        o_ref[...] = jnp.zeros_like(o_ref)


def down_grouped(meta, a, w_down, li, tm, tn=512):
    m, f = a.shape
    d = w_down.shape[3]
    nb = m // tm
    return pl.pallas_call(
        _down_grouped_kernel,
        grid_spec=pltpu.PrefetchScalarGridSpec(
            num_scalar_prefetch=1,
            grid=(d // tn, nb),
            in_specs=[
                pl.BlockSpec((tm, f), lambda j, i, meta: (jnp.minimum(i, meta[nb] - 1), 0)),
                pl.BlockSpec((None, None, f, tn), lambda j, i, meta: (li, meta[i], 0, j)),
            ],
            out_specs=pl.BlockSpec((tm, tn), lambda j, i, meta: (i, j)),
        ),
        out_shape=jax.ShapeDtypeStruct((m, d), F32),
        compiler_params=_params("parallel", "arbitrary"),
        name="moe_down",
    )(meta, a, w_down)


def _router_kernel(x_ref, g_ref, w_ref, h_ref, idx_ref, gate_ref):
    x = x_ref[...]
    h = x * lax.rsqrt(jnp.mean(x * x, axis=-1, keepdims=True) + EPS) * g_ref[...]
    h_ref[...] = h.astype(h_ref.dtype)
    w = w_ref[...]
    hh, wh = h.astype(BF16), w.astype(BF16)
    hl, wl = (h - hh.astype(F32)).astype(BF16), (w - wh.astype(F32)).astype(BF16)
    logits = jnp.dot(hh, wh, preferred_element_type=F32) + (
        jnp.dot(hh, wl, preferred_element_type=F32) + jnp.dot(hl, wh, preferred_element_type=F32))
    lane = lax.broadcasted_iota(jnp.int32, logits.shape, 1)
    logits = jnp.where(lane < N_EXPERTS, logits, -jnp.inf)
    m1 = jnp.max(logits, axis=-1, keepdims=True)
    i1 = jnp.min(jnp.where(logits == m1, lane, LANES), axis=-1, keepdims=True)
    rest = jnp.where(lane == i1, -jnp.inf, logits)
    m2 = jnp.max(rest, axis=-1, keepdims=True)
    i2 = jnp.min(jnp.where(rest == m2, lane, LANES), axis=-1, keepdims=True)
    e2 = jnp.exp(m2 - m1)
    denom = 1.0 + e2
    idx_ref[...] = jnp.where(lane == 0, i1, jnp.where(lane == 1, i2, 0))
    gate_ref[...] = jnp.where(lane == 0, 1.0 / denom, jnp.where(lane == 1, e2 / denom, 0.0))


def moe_router(x, gain, w_router, tm=256):
    n, d = x.shape
    w = jnp.zeros((d, LANES), F32).at[:, :N_EXPERTS].set(w_router.astype(F32))
    return pl.pallas_call(
        _router_kernel,
        grid=(n // tm,),
        in_specs=[pl.BlockSpec((tm, d), lambda i: (i, 0)),
                  pl.BlockSpec((1, d), lambda i: (0, 0)),
                  pl.BlockSpec((d, LANES), lambda i: (0, 0))],
        out_specs=[pl.BlockSpec((tm, d), lambda i: (i, 0)),
                   pl.BlockSpec((tm, LANES), lambda i: (i, 0)),
                   pl.BlockSpec((tm, LANES), lambda i: (i, 0))],
        out_shape=[jax.ShapeDtypeStruct((n, d), F32),
                   jax.ShapeDtypeStruct((n, LANES), jnp.int32),
                   jax.ShapeDtypeStruct((n, LANES), F32)],
        compiler_params=_params("parallel"),
        name="moe_router",
    )(x, gain.reshape(1, d), w)


def _combine_kernel(x_ref, a_ref, b_ref, g_ref, gain_ref, o_ref, h_ref):
    g = g_ref[...]
    x = x_ref[...] + (a_ref[...] * g[:, 0:1] + b_ref[...] * g[:, 1:2])
    o_ref[...] = x
    h_ref[...] = _rms(x, gain_ref[...]).astype(h_ref.dtype)


def moe_combine(x, ya, yb, gates, next_gain, tm=512):
    n, d = x.shape
    row = pl.BlockSpec((tm, d), lambda i: (i, 0))
    return pl.pallas_call(
        _combine_kernel,
        grid=(n // tm,),
        in_specs=[row, row, row, pl.BlockSpec((tm, LANES), lambda i: (i, 0)), pl.BlockSpec((1, d), lambda i: (0, 0))],
        out_specs=[row, row],
        out_shape=[jax.ShapeDtypeStruct((n, d), F32), jax.ShapeDtypeStruct((n, d), BF16)],
        compiler_params=_params("parallel"),
        name="moe_combine",
    )(x, ya, yb, gates, next_gain.reshape(1, d))


def _dispatch_kernel(fill_ref, dest_ref, h_ref, xs_ref, zero_ref, sem):
    i = pl.program_id(0)
    tm = h_ref.shape[0]

    def run(count, make_copy, unroll=1):
        lax.fori_loop(0, count, lambda a, c: (make_copy(a).start(), c)[1], 0, unroll=unroll)
        lax.fori_loop(0, count, lambda a, c: (make_copy(a).wait(), c)[1], 0, unroll=unroll)

    @pl.when(i == 0)
    def _():
        zero_ref[...] = jnp.zeros_like(zero_ref)
        groups = fill_ref.shape[0] // 2
        for e in range(groups):
            first = fill_ref[e]
            run(fill_ref[groups + e],
                lambda a: pltpu.make_async_copy(zero_ref.at[pl.ds(0, 1)], xs_ref.at[pl.ds(first + a, 1)], sem))

    run(TOP_K * tm,
        lambda a: pltpu.make_async_copy(h_ref.at[pl.ds(a // TOP_K, 1)], xs_ref.at[pl.ds(dest_ref[0, a], 1)], sem),
        unroll=8)


def moe_dispatch(h, dest, fill, cap, tm=256):
    n, d = h.shape
    return pl.pallas_call(
        _dispatch_kernel,
        grid_spec=pltpu.PrefetchScalarGridSpec(
            num_scalar_prefetch=1,
            grid=(n // tm,),
            in_specs=[
                pl.BlockSpec((None, 1, TOP_K * tm), lambda i, fill: (i, 0, 0), memory_space=pltpu.SMEM),
                pl.BlockSpec((tm, d), lambda i, fill: (i, 0)),
            ],
            out_specs=pl.BlockSpec(memory_space=pl.ANY),
            scratch_shapes=[pltpu.VMEM((SUBLANES, d), h.dtype), pltpu.SemaphoreType.DMA(())],
        ),
        out_shape=jax.ShapeDtypeStruct((cap, d), h.dtype),
        compiler_params=_params("arbitrary"),
        name="moe_dispatch",
    )(fill, dest.reshape(n // tm, 1, TOP_K * tm), h)


def moe_block(x, gain, w_router, w_gate, w_up, w_down, li, next_gain, tm=512):
    n, d = x.shape
    h, idx, gates = moe_router(x, gain, w_router)
    flat_e = idx[:, :TOP_K].reshape(-1)
    na = n * TOP_K
    onehot = (flat_e[None, :] == jnp.arange(N_EXPERTS, dtype=jnp.int32)[:, None]).astype(jnp.int32)
    csum = jnp.cumsum(onehot, axis=1)
    counts = csum[:, -1]
    padded = (counts + tm - 1) // tm * tm
    pad_end = jnp.cumsum(padded)
    pad_start = pad_end - padded
    dest = jnp.sum(onehot * (csum - 1 + pad_start[:, None]), axis=0)
    n_blocks = na // tm + N_EXPERTS
    cap = n_blocks * tm
    fill = jnp.concatenate([pad_start + counts, pad_end[-1:], padded - counts, cap - pad_end[-1:]]).astype(jnp.int32)
    block_e = jnp.minimum(
        jnp.searchsorted(pad_end, jnp.arange(n_blocks, dtype=jnp.int32) * tm, side='right'),
        N_EXPERTS - 1).astype(jnp.int32)
    meta = jnp.concatenate([block_e, (pad_end[-1:] // tm).astype(jnp.int32)])
    rows = lambda a, idx: a.at[idx].get(mode="promise_in_bounds")
    xs = moe_dispatch(h, dest, fill, cap)
    act = gate_up_grouped(meta, xs, w_gate, w_up, li, tm)
    ys = down_grouped(meta, act, w_down, li, tm)
    dest2 = dest.reshape(n, TOP_K)
    return moe_combine(x, rows(ys, dest2[:, 0]), rows(ys, dest2[:, 1]), gates, next_gain)


def _ple_kernel(h_ref, wg_ref, p_ref, wp_ref, x_ref, gain_ref, o_ref, hn_ref):
    gate = jax.nn.sigmoid(jnp.dot(h_ref[...], wg_ref[...], preferred_element_type=F32))
    proj = jnp.dot(p_ref[...].astype(BF16), wp_ref[...].astype(BF16), preferred_element_type=F32)
    x = x_ref[...] + proj * gate
    o_ref[...] = x
    hn_ref[...] = _rms(x, gain_ref[...]).astype(hn_ref.dtype)


def ple_residual(h, w_gate, p, w_proj, li, x, next_gain, next_dtype, tm=512):
    n, d = x.shape
    kp = p.shape[2]
    row = lambda w: pl.BlockSpec((tm, w), lambda i: (i, 0))
    return pl.pallas_call(
        _ple_kernel,
        grid=(n // tm,),
        in_specs=[
            row(d),
            pl.BlockSpec((None, d, d), lambda i: (li, 0, 0)),
            pl.BlockSpec((None, tm, kp), lambda i: (li, i, 0)),
            pl.BlockSpec((None, kp, d), lambda i: (li, 0, 0)),
            row(d),
            pl.BlockSpec((1, d), lambda i: (0, 0)),
        ],
        out_specs=[row(d), row(d)],
        out_shape=[jax.ShapeDtypeStruct((n, d), F32), jax.ShapeDtypeStruct((n, d), next_dtype)],
        compiler_params=_params("parallel"),
        name="ple_residual",
    )(h, w_gate, p, w_proj, x, next_gain.reshape(1, d))


def hybrid_mixer_residual(x, h, w_in, li, conv_w, a_log, dt_bias, gdn_norm, sinks, bias, w_out, next_gain,
                          batch, seq):
    n = x.shape[0]
    proj = matmul_cols_outer(h, w_in, li, IN_MAIN_W, tm=512, tn=IN_MAIN_W // 2)
    w_tail = jnp.zeros((D_MODEL, LANES), BF16).at[:, :IN_TAIL_W].set(w_in[li, :, IN_MAIN_W:])
    attn = swa_attention(proj, sinks.astype(F32), bias, batch, seq)
    gates = gdn_gates(h, w_tail, a_log, dt_bias)
    pair = GDN_V_HEADS // GDN_QK_HEADS
    g_rows = gates[:, GDN_V_HEADS:2 * GDN_V_HEADS].T.reshape(GDN_QK_HEADS, pair, n // GDN_TILE, GDN_TILE)
    g_rows = g_rows.transpose(0, 2, 1, 3)
    gdn = gdn_mixer(proj, conv_w, gates, g_rows, gdn_norm, batch, seq)
    return out_proj_residual(attn, gdn, w_out, li, x, next_gain)


def kernel(x, p, w_in, conv_w, a_log, dt_bias, gdn_norm, attn_sinks, rel_bias_table, w_out, norm_mix, norm_ffn, w_dense_gate, w_dense_up, w_dense_down, w_router, w_exp_gate, w_exp_up, w_exp_down, norm_ple, w_ple_gate, w_ple_proj, norm_final):
    batch, seq, d = x.shape
    depth = w_in.shape[0]
    n = batch * seq
    out_dtype = x.dtype
    x = x.reshape(n, d).astype(F32)
    assert WINDOW == SWA_BLOCK
    bias = folded_relative_bias(rel_bias_table)
    p = p.reshape(depth, n, PLE_DIM)
    w_in, w_out, w_ple_gate = w_in.astype(BF16), w_out.astype(BF16), w_ple_gate.astype(BF16)
    w_dense_gate, w_dense_up, w_dense_down = (w.astype(BF16) for w in (w_dense_gate, w_dense_up, w_dense_down))
    h = rmsnorm_rows(x, norm_mix[0], BF16)
    for i in range(depth):
        x, hf = hybrid_mixer_residual(x, h, w_in, i, conv_w[i], a_log[i], dt_bias[i], gdn_norm[i],
                                      attn_sinks[i], bias, w_out, norm_ffn[i], batch, seq)
        if i % 2 == 0:
            act = gate_up(hf, w_dense_gate, w_dense_up, i // 2)
            x = down_residual(act, w_dense_down, i // 2, x)
            hp = rmsnorm_rows(x, norm_ple[i], BF16)
        else:
            x, hp = moe_block(x, norm_ffn[i], w_router[i // 2], w_exp_gate, w_exp_up, w_exp_down, i // 2,
                              norm_ple[i])
        last = i == depth - 1
        x, h = ple_residual(hp, w_ple_gate, p, w_ple_proj, i, x, norm_final if last else norm_mix[i + 1],
                            out_dtype if last else BF16)
    return h.reshape(batch, seq, d)
```

```python
import functools
import math

import jax
import jax.numpy as jnp
from jax import lax
from jax.experimental import pallas as pl
from jax.experimental.pallas import tpu as pltpu

F32 = jnp.float32
BF16 = jnp.bfloat16

D_MODEL = 2048
PLE_DIM = 256
EPS = 1e-6
NEG_INF = -1e30

SWA_HEADS = 16
SWA_KV_HEADS = 2
SWA_HEAD_DIM = 64
SWA_GROUP = SWA_HEADS // SWA_KV_HEADS
WINDOW = 128
SWA_BLOCK = 128
REL_BUCKETS = 32
REL_MAX_DIST = 128

GDN_QK_HEADS = 4
GDN_V_HEADS = 8
GDN_HEAD_DIM = 128
GDN_CONV = 4
GDN_TILE = 256
GDN_BLOCK = 256

SWA_Q_W = SWA_HEADS * SWA_HEAD_DIM
SWA_KV_W = SWA_KV_HEADS * SWA_HEAD_DIM
GDN_QK_W = GDN_QK_HEADS * GDN_HEAD_DIM
GDN_V_W = GDN_V_HEADS * GDN_HEAD_DIM
GDN_CONV_CH = 2 * GDN_QK_W + GDN_V_W
MIX_WIDTH = SWA_Q_W + GDN_V_W
IN_MAIN_W = SWA_Q_W + 2 * SWA_KV_W + GDN_CONV_CH + GDN_V_W
IN_TAIL_W = 2 * GDN_V_HEADS

D_FF = 7 * D_MODEL // 2
N_EXPERTS = 8
TOP_K = 2

LANES = 128
SUBLANES = 8
VMEM_LIMIT = 56 * 1024 * 1024

COL_K = SWA_Q_W // LANES
COL_V = COL_K + SWA_KV_W // LANES
COL_GDN = COL_V + SWA_KV_W // LANES
COL_Z = COL_GDN + GDN_CONV_CH // LANES


def _params(*sem):
    return pltpu.CompilerParams(dimension_semantics=sem, vmem_limit_bytes=VMEM_LIMIT)


def _rms_kernel(x_ref, g_ref, o_ref):
    x = x_ref[...]
    y = x * lax.rsqrt(jnp.mean(x * x, axis=-1, keepdims=True) + EPS)
    o_ref[...] = (y * g_ref[...]).astype(o_ref.dtype)


def rmsnorm_rows(x, gain, out_dtype, tm=512):
    n, d = x.shape
    return pl.pallas_call(
        _rms_kernel,
        grid=(n // tm,),
        in_specs=[pl.BlockSpec((tm, d), lambda i: (i, 0)), pl.BlockSpec((1, d), lambda i: (0, 0))],
        out_specs=pl.BlockSpec((tm, d), lambda i: (i, 0)),
        out_shape=jax.ShapeDtypeStruct((n, d), out_dtype),
        compiler_params=_params("parallel"),
        name="rmsnorm",
    )(x, gain.reshape(1, d))


def _mm_kernel(a_ref, w_ref, o_ref):
    o_ref[...] = jnp.dot(a_ref[...], w_ref[...], preferred_element_type=F32).astype(o_ref.dtype)


def matmul_cols_outer(a, w, li, n_out, tm, tn, out_dtype=F32):
    m, k = a.shape
    return pl.pallas_call(
        _mm_kernel,
        grid=(n_out // tn, m // tm),
        in_specs=[pl.BlockSpec((tm, k), lambda j, i: (i, 0)),
                  pl.BlockSpec((None, k, tn), lambda j, i: (li, 0, j))],
        out_specs=pl.BlockSpec((tm, tn), lambda j, i: (i, j)),
        out_shape=jax.ShapeDtypeStruct((m, n_out), out_dtype),
        compiler_params=_params("parallel", "parallel"),
        name="in_proj",
    )(a, w)


def _attn_kernel(sink_ref, q_ref, kc_ref, kp_ref, vc_ref, vp_ref, bias_ref, o_ref):
    n = pl.program_id(1)
    kj = lax.broadcasted_iota(jnp.int32, (SWA_BLOCK, SWA_BLOCK), 0)
    qi = lax.broadcasted_iota(jnp.int32, (SWA_BLOCK, SWA_BLOCK), 1)
    in_cur = kj <= qi
    valid = in_cur | (n > 0)
    scale = SWA_HEAD_DIM ** -0.5
    for g in range(SWA_KV_HEADS):
        cols = slice(g * SWA_HEAD_DIM, (g + 1) * SWA_HEAD_DIM)
        kw = jnp.concatenate([kp_ref[:, cols], kc_ref[:, cols]], axis=0).astype(BF16)
        vt = jnp.concatenate([vp_ref[:, cols], vc_ref[:, cols]], axis=0).T.astype(BF16)
        heads = [g * SWA_GROUP + hh for hh in range(SWA_GROUP)]
        qs = [(q_ref[:, h * SWA_HEAD_DIM:(h + 1) * SWA_HEAD_DIM] * scale).astype(BF16) for h in heads]
        boths = [lax.dot_general(kw, qh, _NT, preferred_element_type=F32) for qh in qs]
        logits = [jnp.where(valid, jnp.where(in_cur, b[SWA_BLOCK:], b[:SWA_BLOCK]) + bias_ref[h], NEG_INF)
                  for h, b in zip(heads, boths)]
        ms = [jnp.maximum(jnp.max(l, axis=0, keepdims=True), sink_ref[h]) for h, l in zip(heads, logits)]
        es = [jnp.exp(l - m) for l, m in zip(logits, ms)]
        denoms = [jnp.sum(e, axis=0, keepdims=True) + jnp.exp(sink_ref[h] - m) for h, e, m in zip(heads, es, ms)]
        unfolded = [jnp.concatenate([jnp.where(in_cur, 0.0, e), jnp.where(in_cur, e, 0.0)], axis=0).astype(BF16)
                    for e in es]
        outs = [jnp.dot(vt, u, preferred_element_type=F32) * (1.0 / d) for u, d in zip(unfolded, denoms)]
        for pair in range(SWA_GROUP // 2):
            h0 = heads[2 * pair]
            both_heads = jnp.concatenate(outs[2 * pair:2 * pair + 2], axis=0).T
            o_ref[:, h0 * SWA_HEAD_DIM:(h0 + 2) * SWA_HEAD_DIM] = both_heads.astype(o_ref.dtype)


def folded_relative_bias(rel_table):
    kj = jnp.arange(SWA_BLOCK, dtype=jnp.int32)[:, None]
    qi = jnp.arange(SWA_BLOCK, dtype=jnp.int32)[None, :]
    dist = jnp.where(kj <= qi, qi - kj, qi + SWA_BLOCK - kj)
    max_exact = REL_BUCKETS // 2
    d = jnp.maximum(dist, 0)
    log_ratio = jnp.log(jnp.maximum(d, 1).astype(F32) / max_exact) / math.log(REL_MAX_DIST / max_exact)
    large = jnp.minimum(max_exact + (log_ratio * (REL_BUCKETS - max_exact)).astype(jnp.int32), REL_BUCKETS - 1)
    bucket = jnp.where(d < max_exact, d, large)
    onehot = (bucket[None] == jnp.arange(REL_BUCKETS, dtype=jnp.int32)[:, None, None]).astype(F32)
    return jnp.einsum('bh,bjq->hjq', rel_table.astype(F32), onehot, precision=lax.Precision.HIGHEST)


def swa_attention(proj, sinks, bias, batch, seq):
    n = proj.shape[0]
    nb = seq // SWA_BLOCK
    cur = lambda b, i, s: b * nb + i
    prev = lambda b, i, s: jnp.maximum(b * nb + i - 1, 0)
    blk = (SWA_BLOCK, LANES)
    return pl.pallas_call(
        _attn_kernel,
        grid_spec=pltpu.PrefetchScalarGridSpec(
            num_scalar_prefetch=1,
            grid=(batch, nb),
            in_specs=[
                pl.BlockSpec((SWA_BLOCK, SWA_Q_W), lambda b, i, s: (cur(b, i, s), 0)),
                pl.BlockSpec(blk, lambda b, i, s: (cur(b, i, s), COL_K)),
                pl.BlockSpec(blk, lambda b, i, s: (prev(b, i, s), COL_K)),
                pl.BlockSpec(blk, lambda b, i, s: (cur(b, i, s), COL_V)),
                pl.BlockSpec(blk, lambda b, i, s: (prev(b, i, s), COL_V)),
                pl.BlockSpec((SWA_HEADS, SWA_BLOCK, SWA_BLOCK), lambda b, i, s: (0, 0, 0)),
            ],
            out_specs=pl.BlockSpec((SWA_BLOCK, SWA_Q_W), lambda b, i, s: (cur(b, i, s), 0)),
        ),
        out_shape=jax.ShapeDtypeStruct((n, SWA_Q_W), BF16),
        compiler_params=_params("parallel", "parallel"),
        name="swa_attention",
    )(sinks, proj, proj, proj, proj, proj, bias)


def _causal_conv_silu(cur, prev, w):
    tm = cur.shape[0]
    xp = jnp.concatenate([prev, cur], axis=0)
    first = SUBLANES - (GDN_CONV - 1)
    acc = xp[first:first + tm] * w[0:1]
    for k in range(1, GDN_CONV):
        acc = acc + xp[first + k:first + k + tm] * w[k:k + 1]
    return acc * jax.nn.sigmoid(acc)


def _l2norm(y):
    return y * lax.rsqrt(jnp.sum(y * y, axis=-1, keepdims=True) + EPS)


def _gdn_gate_kernel(h_ref, w_ref, alog_ref, dtb_ref, o_ref):
    tm = h_ref.shape[0]
    t = jnp.dot(h_ref[...], w_ref[...], preferred_element_type=F32)
    lane = lax.broadcasted_iota(jnp.int32, t.shape, 1)
    beta = jax.nn.sigmoid(t)
    g = -jnp.exp(alog_ref[...]) * jax.nn.softplus(t + dtb_ref[...])
    g = jnp.where((lane >= GDN_V_HEADS) & (lane < 2 * GDN_V_HEADS), g, 0.0)
    blk = GDN_BLOCK
    r = lax.broadcasted_iota(jnp.int32, (blk, blk), 0)
    s = lax.broadcasted_iota(jnp.int32, (blk, blk), 1)
    tri = jnp.where(s <= r, 1.0, 0.0).astype(BF16)
    sums = []
    for b in range(tm // blk):
        gb = g[b * blk:(b + 1) * blk]
        hi = gb.astype(BF16)
        rest = gb - hi.astype(F32)
        mid = rest.astype(BF16)
        lo = (rest - mid.astype(F32)).astype(BF16)
        sums.append(jnp.dot(tri, hi, preferred_element_type=F32)
                    + (jnp.dot(tri, mid, preferred_element_type=F32) + jnp.dot(tri, lo, preferred_element_type=F32)))
    gcum = jnp.concatenate(sums, axis=0)
    o_ref[...] = jnp.where(lane < GDN_V_HEADS, beta, gcum)


def gdn_gates(h, w_tail, a_log, dt_bias, tm=512):
    n, k = h.shape
    pad = lambda v: jnp.zeros((1, LANES), F32).at[0, GDN_V_HEADS:2 * GDN_V_HEADS].set(v.astype(F32))
    return pl.pallas_call(
        _gdn_gate_kernel,
        grid=(n // tm,),
        in_specs=[pl.BlockSpec((tm, k), lambda i: (i, 0)),
                  pl.BlockSpec((k, LANES), lambda i: (0, 0)),
                  pl.BlockSpec((1, LANES), lambda i: (0, 0)),
                  pl.BlockSpec((1, LANES), lambda i: (0, 0))],
        out_specs=pl.BlockSpec((tm, LANES), lambda i: (i, 0)),
        out_shape=jax.ShapeDtypeStruct((n, LANES), F32),
        compiler_params=_params("parallel"),
        name="gdn_gates",
    )(h, w_tail, pad(a_log), pad(dt_bias))


def _bdot(a, b, dims=(((1,), (0,)), ((), ()))):
    return lax.dot_general(a.astype(BF16), b.astype(BF16), dims, preferred_element_type=F32)


_NT = (((1,), (1,)), ((), ()))
_TN = (((0,), (0,)), ((), ()))


def _unit_lower_inverses(lows, block):
    r_dim = lows[0].shape[0]
    r = lax.broadcasted_iota(jnp.int32, (r_dim, r_dim), 0)
    s = lax.broadcasted_iota(jnp.int32, (r_dim, r_dim), 1)
    level = 31 - lax.clz(r ^ s)
    size = 1
    step = 0
    xs = None
    while size < block:
        sel = level == step
        step += 1
        offs = [jnp.where(sel, low, 0.0) for low in lows]
        if size == 1:
            eye = jnp.where(r == s, 1.0, 0.0).astype(F32)
            xs = [eye - off for off in offs]
        else:
            xo = [_bdot(x, off) for x, off in zip(xs, offs)]
            xs = [x - _bdot(y, x) for x, y in zip(xs, xo)]
        size *= 2
    return xs


def _gdn_kernel(qin_ref, qprev_ref, kin_ref, kprev_ref, vin_ref, vprev_ref, wq_ref, wk_ref, wv_ref,
                z_ref, gb_ref, grow_ref, norm_ref, o_ref,
                state_ref, q_ref, k_ref, v_ref, w_s, u_s, qd_s, kd_s, qk_s):
    hp = pl.program_id(1)
    t = pl.program_id(2)
    tb = qin_ref.shape[0]
    rt = GDN_TILE
    blk = GDN_BLOCK
    pair = GDN_V_HEADS // GDN_QK_HEADS
    d = GDN_HEAD_DIM

    @pl.when(t == 0)
    def _():
        state_ref[...] = jnp.zeros_like(state_ref)

    halo = lambda ref: jnp.where(t == 0, 0.0, ref[...])
    q_ref[...] = _l2norm(_causal_conv_silu(qin_ref[...], halo(qprev_ref), wq_ref[...]))
    k_ref[...] = _l2norm(_causal_conv_silu(kin_ref[...], halo(kprev_ref), wk_ref[...]))
    v_all = _causal_conv_silu(vin_ref[...], halo(vprev_ref), wv_ref[...])
    for hh in range(pair):
        v_ref[hh] = v_all[:, hh * d:(hh + 1) * d]

    lane = lax.broadcasted_iota(jnp.int32, (tb, LANES), 1)
    gb = gb_ref[...]
    r = lax.broadcasted_iota(jnp.int32, (rt, rt), 0)
    s = lax.broadcasted_iota(jnp.int32, (rt, rt), 1)
    same = (r // blk) == (s // blk)
    causal = (s <= r) & same
    strict = (s < r) & same

    g_heads = []
    beta_heads = []
    for hh in range(pair):
        h = pair * hp + hh
        beta_heads.append(jnp.sum(jnp.where(lane == h, gb, 0.0), axis=-1, keepdims=True))
        g_heads.append(jnp.sum(jnp.where(lane == h + GDN_V_HEADS, gb, 0.0), axis=-1, keepdims=True))

    probs = [(hh, ti, slice(ti * rt, (ti + 1) * rt)) for hh in range(pair) for ti in range(tb // rt)]
    qs = {ti: q_ref[ti * rt:(ti + 1) * rt, :] * (d ** -0.5) for ti in range(tb // rt)}
    ks = {ti: k_ref[ti * rt:(ti + 1) * rt, :] for ti in range(tb // rt)}
    betas = [beta_heads[hh][rows] for hh, ti, rows in probs]
    gs = [g_heads[hh][rows] for hh, ti, rows in probs]
    egs = [jnp.exp(g) for g in gs]
    kbs = [ks[ti] * beta for (hh, ti, rows), beta in zip(probs, betas)]
    kqs = [_bdot(jnp.concatenate([kb, qs[ti]], axis=0), ks[ti], _NT) for (hh, ti, rows), kb in zip(probs, kbs)]
    decays = [jnp.where(causal, jnp.exp(jnp.where(causal, g - grow_ref[ti, hh:hh + 1, :], 0.0)), 0.0)
              for (hh, ti, rows), g in zip(probs, gs)]
    lows = [jnp.where(strict, kq[:rt] * decay, 0.0) for kq, decay in zip(kqs, decays)]
    for (hh, ti, rows), kq, decay, g, eg in zip(probs, kqs, decays, gs, egs):
        qk_s[hh, rows, :] = (kq[rt:] * decay).astype(BF16)
        g_last = jnp.concatenate(
            [jnp.broadcast_to(g[(c + 1) * blk - 1:(c + 1) * blk], (blk, 1)) for c in range(rt // blk)], axis=0)
        qd_s[hh, rows, :] = (qs[ti] * eg).astype(BF16)
        kd_s[hh, rows, :] = (ks[ti] * jnp.exp(g_last - g)).astype(BF16)
    rhss = [jnp.concatenate([v_ref[hh, rows, :] * beta, kb * eg], axis=1).astype(BF16)
            for (hh, ti, rows), beta, kb, eg in zip(probs, betas, kbs, egs)]

    tinvs = _unit_lower_inverses(lows, blk)
    uws = [_bdot(tinv, rhs) for tinv, rhs in zip(tinvs, rhss)]
    for (hh, ti, rows), uw in zip(probs, uws):
        u_s[hh, rows, :] = uw[:, :d]
        w_s[hh, rows, :] = uw[:, d:].astype(BF16)

    heads = range(pair)
    for c in range(tb // blk):
        rows = slice(c * blk, (c + 1) * blk)
        off = (c * blk) % rt
        states = [state_ref[hh] for hh in heads]
        ws = [_bdot(jnp.concatenate([w_s[hh, rows, :], qd_s[hh, rows, :]], axis=0), states[hh]) for hh in heads]
        v_new = [u_s[hh, rows, :] - ws[hh][:blk] for hh in heads]
        outs = [ws[hh][blk:] + _bdot(qk_s[hh, rows, off:off + blk], v_new[hh]) for hh in heads]
        for hh in heads:
            g_end = g_heads[hh][(c + 1) * blk - 1:(c + 1) * blk]
            state_ref[hh] = states[hh] * jnp.exp(g_end) + _bdot(kd_s[hh, rows, :], v_new[hh], _TN)
        for hh in heads:
            o = outs[hh]
            o = o * lax.rsqrt(jnp.mean(o * o, axis=-1, keepdims=True) + EPS) * norm_ref[...]
            zc = z_ref[rows, hh * d:(hh + 1) * d]
            o_ref[rows, hh * d:(hh + 1) * d] = (o * (zc * jax.nn.sigmoid(zc))).astype(o_ref.dtype)


def gdn_mixer(proj, conv_w, gates, g_rows, gdn_norm, batch, seq, tb=512):
    n = proj.shape[0]
    nt = seq // tb
    pair = GDN_V_HEADS // GDN_QK_HEADS
    d = GDN_HEAD_DIM
    row = lambda b, hp, t: b * nt + t
    before = lambda b, hp, t: jnp.maximum(row(b, hp, t) * (tb // SUBLANES) - 1, 0)
    wide = (tb, pair * d)
    q_col = COL_GDN
    k_col = COL_GDN + GDN_QK_HEADS
    v_col = (COL_GDN + 2 * GDN_QK_HEADS) // pair
    z_col = COL_Z // pair
    return pl.pallas_call(
        _gdn_kernel,
        grid=(batch, GDN_QK_HEADS, nt),
        in_specs=[
            pl.BlockSpec((tb, d), lambda b, hp, t: (row(b, hp, t), q_col + hp)),
            pl.BlockSpec((SUBLANES, d), lambda b, hp, t: (before(b, hp, t), q_col + hp)),
            pl.BlockSpec((tb, d), lambda b, hp, t: (row(b, hp, t), k_col + hp)),
            pl.BlockSpec((SUBLANES, d), lambda b, hp, t: (before(b, hp, t), k_col + hp)),
            pl.BlockSpec(wide, lambda b, hp, t: (row(b, hp, t), v_col + hp)),
            pl.BlockSpec((SUBLANES, pair * d), lambda b, hp, t: (before(b, hp, t), v_col + hp)),
            pl.BlockSpec((GDN_CONV, d), lambda b, hp, t: (0, hp)),
            pl.BlockSpec((GDN_CONV, d), lambda b, hp, t: (0, GDN_QK_HEADS + hp)),
            pl.BlockSpec((GDN_CONV, pair * d), lambda b, hp, t: (0, 2 * GDN_QK_HEADS // pair + hp)),
            pl.BlockSpec(wide, lambda b, hp, t: (row(b, hp, t), z_col + hp)),
            pl.BlockSpec((tb, LANES), lambda b, hp, t: (row(b, hp, t), 0)),
            pl.BlockSpec((None, tb // GDN_TILE, pair, GDN_TILE), lambda b, hp, t: (hp, row(b, hp, t), 0, 0)),
            pl.BlockSpec((1, LANES), lambda b, hp, t: (0, 0)),
        ],
        out_specs=pl.BlockSpec(wide, lambda b, hp, t: (row(b, hp, t), hp)),
        out_shape=jax.ShapeDtypeStruct((n, GDN_V_W), BF16),
        scratch_shapes=[
            pltpu.VMEM((pair, d, d), F32),
            pltpu.VMEM((tb, d), F32),
            pltpu.VMEM((tb, d), F32),
            pltpu.VMEM((pair, tb, d), F32),
            pltpu.VMEM((pair, tb, d), BF16),
            pltpu.VMEM((pair, tb, d), F32),
            pltpu.VMEM((pair, tb, d), BF16),
            pltpu.VMEM((pair, tb, d), BF16),
            pltpu.VMEM((pair, tb, GDN_TILE), BF16),
        ],
        compiler_params=_params("parallel", "parallel", "arbitrary"),
        name="gated_delta_rule",
    )(proj, proj, proj, proj, proj, proj, conv_w, conv_w, conv_w, proj, gates, g_rows,
      gdn_norm.reshape(1, LANES).astype(F32))


def _rms(x, gain):
    return x * lax.rsqrt(jnp.mean(x * x, axis=-1, keepdims=True) + EPS) * gain


def _out_proj_kernel(a_ref, b_ref, w_ref, x_ref, g_ref, o_ref, h_ref):
    ka = a_ref.shape[1]
    acc = jnp.dot(a_ref[...], w_ref[:ka, :], preferred_element_type=F32)
    acc += jnp.dot(b_ref[...], w_ref[ka:, :], preferred_element_type=F32)
    x = x_ref[...] + acc
    o_ref[...] = x
    h_ref[...] = _rms(x, g_ref[...]).astype(h_ref.dtype)


def out_proj_residual(attn, gdn, w_out, li, x, next_gain, tm=512):
    n, d = x.shape
    ka, kb = attn.shape[1], gdn.shape[1]
    row = lambda w: pl.BlockSpec((tm, w), lambda i: (i, 0))
    return pl.pallas_call(
        _out_proj_kernel,
        grid=(n // tm,),
        in_specs=[row(ka), row(kb),
                  pl.BlockSpec((None, ka + kb, d), lambda i: (li, 0, 0)),
                  row(d),
                  pl.BlockSpec((1, d), lambda i: (0, 0))],
        out_specs=[row(d), row(d)],
        out_shape=[jax.ShapeDtypeStruct((n, d), F32), jax.ShapeDtypeStruct((n, d), BF16)],
        compiler_params=_params("parallel"),
        name="out_proj_residual",
    )(attn, gdn, w_out, x, next_gain.reshape(1, d))


def _swiglu_tile(h, wg, wu):
    g = jnp.dot(h, wg, preferred_element_type=F32)
    u = jnp.dot(h, wu, preferred_element_type=F32)
    return g * jax.nn.sigmoid(g) * u


def _gate_up_kernel(h_ref, wg_ref, wu_ref, o_ref):
    o_ref[...] = _swiglu_tile(h_ref[...], wg_ref[...], wu_ref[...]).astype(o_ref.dtype)


def gate_up(h, w_gate, w_up, li, tm=512, tn=1024):
    m, k = h.shape
    f = w_gate.shape[2]
    return pl.pallas_call(
        _gate_up_kernel,
        grid=(f // tn, m // tm),
        in_specs=[
            pl.BlockSpec((tm, k), lambda j, i: (i, 0)),
            pl.BlockSpec((None, k, tn), lambda j, i: (li, 0, j)),
            pl.BlockSpec((None, k, tn), lambda j, i: (li, 0, j)),
        ],
        out_specs=pl.BlockSpec((tm, tn), lambda j, i: (i, j)),
        out_shape=jax.ShapeDtypeStruct((m, f), BF16),
        compiler_params=_params("parallel", "parallel"),
        name="swiglu_gate_up",
    )(h, w_gate, w_up)


def _rows_in_use(meta_ref, o_ref, compute):
    i = pl.program_id(1)
    valid = meta_ref[pl.num_programs(1) + i]
    tm = o_ref.shape[0]
    half = tm // 2

    @pl.when(valid > half)
    def _():
        o_ref[...] = compute(slice(None))

    @pl.when((valid > 0) & (valid <= half))
    def _():
        o_ref[:half, :] = compute(slice(0, half))
        o_ref[half:, :] = jnp.zeros((tm - half, o_ref.shape[1]), o_ref.dtype)

    @pl.when(valid == 0)
    def _():
        o_ref[...] = jnp.zeros_like(o_ref)


def _gate_up_grouped_kernel(meta_ref, h_ref, wg_ref, wu_ref, o_ref):
    _rows_in_use(meta_ref, o_ref, lambda rows: _swiglu_tile(
        h_ref[rows, :].astype(BF16), wg_ref[...].astype(BF16), wu_ref[...].astype(BF16)).astype(o_ref.dtype))


def gate_up_grouped(meta, h, w_gate, w_up, li, tm, tn=1024):
    m, k = h.shape
    f = w_gate.shape[3]
    nb = m // tm
    row = lambda j, i, meta: (jnp.minimum(i, meta[2 * nb] - 1), 0)
    wmap = lambda j, i, meta: (li, meta[i], 0, j)
    return pl.pallas_call(
        _gate_up_grouped_kernel,
        grid_spec=pltpu.PrefetchScalarGridSpec(
            num_scalar_prefetch=1,
            grid=(f // tn, nb),
            in_specs=[
                pl.BlockSpec((tm, k), row),
                pl.BlockSpec((None, None, k, tn), wmap),
                pl.BlockSpec((None, None, k, tn), wmap),
            ],
            out_specs=pl.BlockSpec((tm, tn), lambda j, i, meta: (i, j)),
        ),
        out_shape=jax.ShapeDtypeStruct((m, f), BF16),
        compiler_params=_params("parallel", "arbitrary"),
        name="moe_gate_up",
    )(meta, h, w_gate, w_up)


def _down_kernel(a_ref, w_ref, x_ref, o_ref):
    o_ref[...] = x_ref[...] + jnp.dot(a_ref[...], w_ref[...], preferred_element_type=F32)


def down_residual(a, w_down, li, x, tm=512, tn=512):
    n, d = x.shape
    f = a.shape[1]
    return pl.pallas_call(
        _down_kernel,
        grid=(d // tn, n // tm),
        in_specs=[
            pl.BlockSpec((tm, f), lambda j, i: (i, 0)),
            pl.BlockSpec((None, f, tn), lambda j, i: (li, 0, j)),
            pl.BlockSpec((tm, tn), lambda j, i: (i, j)),
        ],
        out_specs=pl.BlockSpec((tm, tn), lambda j, i: (i, j)),
        out_shape=jax.ShapeDtypeStruct((n, d), F32),
        compiler_params=_params("parallel", "parallel"),
        name="swiglu_down_residual",
    )(a, w_down, x)


def _down_grouped_kernel(meta_ref, a_ref, w_ref, o_ref):
    _rows_in_use(meta_ref, o_ref, lambda rows: jnp.dot(
        a_ref[rows, :], w_ref[...].astype(BF16), preferred_element_type=F32))


def down_grouped(meta, a, w_down, li, tm, tn=512):
    m, f = a.shape
    d = w_down.shape[3]
    nb = m // tm
    return pl.pallas_call(
        _down_grouped_kernel,
        grid_spec=pltpu.PrefetchScalarGridSpec(
            num_scalar_prefetch=1,
            grid=(d // tn, nb),
            in_specs=[
                pl.BlockSpec((tm, f), lambda j, i, meta: (jnp.minimum(i, meta[2 * nb] - 1), 0)),
                pl.BlockSpec((None, None, f, tn), lambda j, i, meta: (li, meta[i], 0, j)),
            ],
            out_specs=pl.BlockSpec((tm, tn), lambda j, i, meta: (i, j)),
        ),
        out_shape=jax.ShapeDtypeStruct((m, d), F32),
        compiler_params=_params("parallel", "arbitrary"),
        name="moe_down",
    )(meta, a, w_down)


def _router_kernel(x_ref, g_ref, w_ref, h_ref, idx_ref, gate_ref):
    x = x_ref[...]
    h = x * lax.rsqrt(jnp.mean(x * x, axis=-1, keepdims=True) + EPS) * g_ref[...]
    h_ref[...] = h.astype(h_ref.dtype)
    w = w_ref[...]
    hh, wh = h.astype(BF16), w.astype(BF16)
    hl, wl = (h - hh.astype(F32)).astype(BF16), (w - wh.astype(F32)).astype(BF16)
    logits = jnp.dot(hh, wh, preferred_element_type=F32) + (
        jnp.dot(hh, wl, preferred_element_type=F32) + jnp.dot(hl, wh, preferred_element_type=F32))
    lane = lax.broadcasted_iota(jnp.int32, logits.shape, 1)
    logits = jnp.where(lane < N_EXPERTS, logits, -jnp.inf)
    m1 = jnp.max(logits, axis=-1, keepdims=True)
    i1 = jnp.min(jnp.where(logits == m1, lane, LANES), axis=-1, keepdims=True)
    rest = jnp.where(lane == i1, -jnp.inf, logits)
    m2 = jnp.max(rest, axis=-1, keepdims=True)
    i2 = jnp.min(jnp.where(rest == m2, lane, LANES), axis=-1, keepdims=True)
    e2 = jnp.exp(m2 - m1)
    denom = 1.0 + e2
    idx_ref[...] = jnp.where(lane == 0, i1, jnp.where(lane == 1, i2, 0))
    gate_ref[...] = jnp.where(lane == 0, 1.0 / denom, jnp.where(lane == 1, e2 / denom, 0.0))


def moe_router(x, gain, w_router, tm=256):
    n, d = x.shape
    w = jnp.zeros((d, LANES), F32).at[:, :N_EXPERTS].set(w_router.astype(F32))
    return pl.pallas_call(
        _router_kernel,
        grid=(n // tm,),
        in_specs=[pl.BlockSpec((tm, d), lambda i: (i, 0)),
                  pl.BlockSpec((1, d), lambda i: (0, 0)),
                  pl.BlockSpec((d, LANES), lambda i: (0, 0))],
        out_specs=[pl.BlockSpec((tm, d), lambda i: (i, 0)),
                   pl.BlockSpec((tm, LANES), lambda i: (i, 0)),
                   pl.BlockSpec((tm, LANES), lambda i: (i, 0))],
        out_shape=[jax.ShapeDtypeStruct((n, d), F32),
                   jax.ShapeDtypeStruct((n, LANES), jnp.int32),
                   jax.ShapeDtypeStruct((n, LANES), F32)],
        compiler_params=_params("parallel"),
        name="moe_router",
    )(x, gain.reshape(1, d), w)


def _combine_kernel(x_ref, a_ref, b_ref, g_ref, o_ref):
    g = g_ref[...]
    o_ref[...] = x_ref[...] + (a_ref[...] * g[:, 0:1] + b_ref[...] * g[:, 1:2])


def moe_combine(x, ya, yb, gates, tm=512):
    n, d = x.shape
    row = pl.BlockSpec((tm, d), lambda i: (i, 0))
    return pl.pallas_call(
        _combine_kernel,
        grid=(n // tm,),
        in_specs=[row, row, row, pl.BlockSpec((tm, LANES), lambda i: (i, 0))],
        out_specs=row,
        out_shape=jax.ShapeDtypeStruct((n, d), F32),
        compiler_params=_params("parallel"),
        name="moe_combine",
    )(x, ya, yb, gates)


def _dispatch_kernel(fill_ref, dest_ref, h_ref, xs_ref, zero_ref, sem):
    i = pl.program_id(0)
    tm = h_ref.shape[0]

    def run(count, make_copy, unroll=1):
        lax.fori_loop(0, count, lambda a, c: (make_copy(a).start(), c)[1], 0, unroll=unroll)
        lax.fori_loop(0, count, lambda a, c: (make_copy(a).wait(), c)[1], 0, unroll=unroll)

    @pl.when(i == 0)
    def _():
        zero_ref[...] = jnp.zeros_like(zero_ref)
        groups = fill_ref.shape[0] // 2
        for e in range(groups):
            first = fill_ref[e]
            run(fill_ref[groups + e],
                lambda a: pltpu.make_async_copy(zero_ref.at[pl.ds(0, 1)], xs_ref.at[pl.ds(first + a, 1)], sem))

    run(TOP_K * tm,
        lambda a: pltpu.make_async_copy(h_ref.at[pl.ds(a // TOP_K, 1)], xs_ref.at[pl.ds(dest_ref[0, a], 1)], sem),
        unroll=8)


def moe_dispatch(h, dest, fill, cap, tm=256):
    n, d = h.shape
    return pl.pallas_call(
        _dispatch_kernel,
        grid_spec=pltpu.PrefetchScalarGridSpec(
            num_scalar_prefetch=1,
            grid=(n // tm,),
            in_specs=[
                pl.BlockSpec((None, 1, TOP_K * tm), lambda i, fill: (i, 0, 0), memory_space=pltpu.SMEM),
                pl.BlockSpec((tm, d), lambda i, fill: (i, 0)),
            ],
            out_specs=pl.BlockSpec(memory_space=pl.ANY),
            scratch_shapes=[pltpu.VMEM((SUBLANES, d), h.dtype), pltpu.SemaphoreType.DMA(())],
        ),
        out_shape=jax.ShapeDtypeStruct((cap, d), h.dtype),
        compiler_params=_params("arbitrary"),
        name="moe_dispatch",
    )(fill, dest.reshape(n // tm, 1, TOP_K * tm), h)


def moe_block(x, gain, w_router, w_gate, w_up, w_down, li, tm=512):
    n, d = x.shape
    h, idx, gates = moe_router(x, gain, w_router)
    flat_e = idx[:, :TOP_K].reshape(-1)
    na = n * TOP_K
    onehot = (flat_e[None, :] == jnp.arange(N_EXPERTS, dtype=jnp.int32)[:, None]).astype(jnp.int32)
    csum = jnp.cumsum(onehot, axis=1)
    counts = csum[:, -1]
    padded = (counts + tm - 1) // tm * tm
    pad_end = jnp.cumsum(padded)
    pad_start = pad_end - padded
    dest = jnp.sum(onehot * (csum - 1 + pad_start[:, None]), axis=0)
    n_blocks = na // tm + N_EXPERTS
    cap = n_blocks * tm
    fill = jnp.concatenate([pad_start + counts, pad_end[-1:], padded - counts, cap - pad_end[-1:]]).astype(jnp.int32)
    block_e = jnp.minimum(
        jnp.searchsorted(pad_end, jnp.arange(n_blocks, dtype=jnp.int32) * tm, side='right'),
        N_EXPERTS - 1).astype(jnp.int32)
    block_rows = jnp.clip((pad_start + counts)[block_e] - jnp.arange(n_blocks, dtype=jnp.int32) * tm, 0, tm)
    meta = jnp.concatenate([block_e, block_rows, pad_end[-1:] // tm]).astype(jnp.int32)
    rows = lambda a, idx: a.at[idx].get(mode="promise_in_bounds")
    xs = moe_dispatch(h, dest, fill, cap)
    act = gate_up_grouped(meta, xs, w_gate, w_up, li, tm)
    ys = down_grouped(meta, act, w_down, li, tm)
    dest2 = dest.reshape(n, TOP_K)
    return moe_combine(x, rows(ys, dest2[:, 0]), rows(ys, dest2[:, 1]), gates)


def _ple_kernel(gain_ref, wg_ref, p_ref, wp_ref, x_ref, next_gain_ref, o_ref, hn_ref):
    x = x_ref[...]
    h = _rms(x, gain_ref[...]).astype(BF16)
    gate = jax.nn.sigmoid(jnp.dot(h, wg_ref[...], preferred_element_type=F32))
    proj = jnp.dot(p_ref[...].astype(BF16), wp_ref[...].astype(BF16), preferred_element_type=F32)
    x = x + proj * gate
    o_ref[...] = x
    hn_ref[...] = _rms(x, next_gain_ref[...]).astype(hn_ref.dtype)


def ple_residual(gain, w_gate, p, w_proj, li, x, next_gain, next_dtype, tm=512):
    n, d = x.shape
    kp = p.shape[2]
    row = lambda w: pl.BlockSpec((tm, w), lambda i: (i, 0))
    vec = pl.BlockSpec((1, d), lambda i: (0, 0))
    return pl.pallas_call(
        _ple_kernel,
        grid=(n // tm,),
        in_specs=[
            vec,
            pl.BlockSpec((None, d, d), lambda i: (li, 0, 0)),
            pl.BlockSpec((None, tm, kp), lambda i: (li, i, 0)),
            pl.BlockSpec((None, kp, d), lambda i: (li, 0, 0)),
            row(d),
            vec,
        ],
        out_specs=[row(d), row(d)],
        out_shape=[jax.ShapeDtypeStruct((n, d), F32), jax.ShapeDtypeStruct((n, d), next_dtype)],
        compiler_params=_params("parallel"),
        name="ple_residual",
    )(gain.reshape(1, d), w_gate, p, w_proj, x, next_gain.reshape(1, d))


def hybrid_mixer_residual(x, h, w_in, li, conv_w, a_log, dt_bias, gdn_norm, sinks, bias, w_out, next_gain,
                          batch, seq):
    n = x.shape[0]
    proj = matmul_cols_outer(h, w_in, li, IN_MAIN_W, tm=512, tn=IN_MAIN_W // 2)
    w_tail = jnp.zeros((D_MODEL, LANES), BF16).at[:, :IN_TAIL_W].set(w_in[li, :, IN_MAIN_W:])
    attn = swa_attention(proj, sinks.astype(F32), bias, batch, seq)
    gates = gdn_gates(h, w_tail, a_log, dt_bias)
    pair = GDN_V_HEADS // GDN_QK_HEADS
    g_rows = gates[:, GDN_V_HEADS:2 * GDN_V_HEADS].T.reshape(GDN_QK_HEADS, pair, n // GDN_TILE, GDN_TILE)
    g_rows = g_rows.transpose(0, 2, 1, 3)
    gdn = gdn_mixer(proj, conv_w, gates, g_rows, gdn_norm, batch, seq)
    return out_proj_residual(attn, gdn, w_out, li, x, next_gain)


def kernel(x, p, w_in, conv_w, a_log, dt_bias, gdn_norm, attn_sinks, rel_bias_table, w_out, norm_mix, norm_ffn, w_dense_gate, w_dense_up, w_dense_down, w_router, w_exp_gate, w_exp_up, w_exp_down, norm_ple, w_ple_gate, w_ple_proj, norm_final):
    batch, seq, d = x.shape
    depth = w_in.shape[0]
    n = batch * seq
    out_dtype = x.dtype
    x = x.reshape(n, d).astype(F32)
    assert WINDOW == SWA_BLOCK
    bias = folded_relative_bias(rel_bias_table)
    p = p.reshape(depth, n, PLE_DIM)
    w_in, w_out, w_ple_gate = w_in.astype(BF16), w_out.astype(BF16), w_ple_gate.astype(BF16)
    w_dense_gate, w_dense_up, w_dense_down = (w.astype(BF16) for w in (w_dense_gate, w_dense_up, w_dense_down))
    h = rmsnorm_rows(x, norm_mix[0], BF16)
    for i in range(depth):
        x, hf = hybrid_mixer_residual(x, h, w_in, i, conv_w[i], a_log[i], dt_bias[i], gdn_norm[i],
                                      attn_sinks[i], bias, w_out, norm_ffn[i], batch, seq)
        if i % 2 == 0:
            act = gate_up(hf, w_dense_gate, w_dense_up, i // 2)
            x = down_residual(act, w_dense_down, i // 2, x)
        else:
            x = moe_block(x, norm_ffn[i], w_router[i // 2], w_exp_gate, w_exp_up, w_exp_down, i // 2)
        last = i == depth - 1
        x, h = ple_residual(norm_ple[i], w_ple_gate, p, w_ple_proj, i, x,
                            norm_final if last else norm_mix[i + 1], out_dtype if last else BF16)
    return h.reshape(batch, seq, d)
```

```python
import functools
import math

import jax
import jax.numpy as jnp
from jax import lax
from jax.experimental import pallas as pl
from jax.experimental.pallas import tpu as pltpu

F32 = jnp.float32
BF16 = jnp.bfloat16

D_MODEL = 2048
PLE_DIM = 256
EPS = 1e-6
NEG_INF = -1e30

SWA_HEADS = 16
SWA_KV_HEADS = 2
SWA_HEAD_DIM = 64
SWA_GROUP = SWA_HEADS // SWA_KV_HEADS
WINDOW = 128
SWA_BLOCK = 128
REL_BUCKETS = 32
REL_MAX_DIST = 128

GDN_QK_HEADS = 4
GDN_V_HEADS = 8
GDN_HEAD_DIM = 128
GDN_CONV = 4
GDN_TILE = 256
GDN_BLOCK = 256

SWA_Q_W = SWA_HEADS * SWA_HEAD_DIM
SWA_KV_W = SWA_KV_HEADS * SWA_HEAD_DIM
GDN_QK_W = GDN_QK_HEADS * GDN_HEAD_DIM
GDN_V_W = GDN_V_HEADS * GDN_HEAD_DIM
GDN_CONV_CH = 2 * GDN_QK_W + GDN_V_W
MIX_WIDTH = SWA_Q_W + GDN_V_W
IN_MAIN_W = SWA_Q_W + 2 * SWA_KV_W + GDN_CONV_CH + GDN_V_W
IN_TAIL_W = 2 * GDN_V_HEADS

D_FF = 7 * D_MODEL // 2
N_EXPERTS = 8
TOP_K = 2

LANES = 128
SUBLANES = 8
VMEM_LIMIT = 56 * 1024 * 1024

COL_K = SWA_Q_W // LANES
COL_V = COL_K + SWA_KV_W // LANES
COL_GDN = COL_V + SWA_KV_W // LANES
COL_Z = COL_GDN + GDN_CONV_CH // LANES


def _params(*sem):
    return pltpu.CompilerParams(dimension_semantics=sem, vmem_limit_bytes=VMEM_LIMIT)


def _rms_kernel(x_ref, g_ref, o_ref):
    x = x_ref[...]
    y = x * lax.rsqrt(jnp.mean(x * x, axis=-1, keepdims=True) + EPS)
    o_ref[...] = (y * g_ref[...]).astype(o_ref.dtype)


def rmsnorm_rows(x, gain, out_dtype, tm=512):
    n, d = x.shape
    return pl.pallas_call(
        _rms_kernel,
        grid=(n // tm,),
        in_specs=[pl.BlockSpec((tm, d), lambda i: (i, 0)), pl.BlockSpec((1, d), lambda i: (0, 0))],
        out_specs=pl.BlockSpec((tm, d), lambda i: (i, 0)),
        out_shape=jax.ShapeDtypeStruct((n, d), out_dtype),
        compiler_params=_params("parallel"),
        name="rmsnorm",
    )(x, gain.reshape(1, d))


def _mm_kernel(a_ref, w_ref, o_ref):
    o_ref[...] = jnp.dot(a_ref[...], w_ref[...], preferred_element_type=F32).astype(o_ref.dtype)


def matmul_cols_outer(a, w, li, n_out, tm, tn, out_dtype=F32):
    m, k = a.shape
    return pl.pallas_call(
        _mm_kernel,
        grid=(n_out // tn, m // tm),
        in_specs=[pl.BlockSpec((tm, k), lambda j, i: (i, 0)),
                  pl.BlockSpec((None, k, tn), lambda j, i: (li, 0, j))],
        out_specs=pl.BlockSpec((tm, tn), lambda j, i: (i, j)),
        out_shape=jax.ShapeDtypeStruct((m, n_out), out_dtype),
        compiler_params=_params("parallel", "parallel"),
        name="in_proj",
    )(a, w)


def _attn_kernel(sink_ref, q_ref, kc_ref, kp_ref, vc_ref, vp_ref, bias_ref, o_ref):
    n = pl.program_id(1)
    kj = lax.broadcasted_iota(jnp.int32, (SWA_BLOCK, SWA_BLOCK), 0)
    qi = lax.broadcasted_iota(jnp.int32, (SWA_BLOCK, SWA_BLOCK), 1)
    in_cur = kj <= qi
    valid = in_cur | (n > 0)
    scale = SWA_HEAD_DIM ** -0.5
    for g in range(SWA_KV_HEADS):
        cols = slice(g * SWA_HEAD_DIM, (g + 1) * SWA_HEAD_DIM)
        kw = jnp.concatenate([kp_ref[:, cols], kc_ref[:, cols]], axis=0).astype(BF16)
        vt = jnp.concatenate([vp_ref[:, cols], vc_ref[:, cols]], axis=0).T.astype(BF16)
        heads = [g * SWA_GROUP + hh for hh in range(SWA_GROUP)]
        qs = [(q_ref[:, h * SWA_HEAD_DIM:(h + 1) * SWA_HEAD_DIM] * scale).astype(BF16) for h in heads]
        boths = [lax.dot_general(kw, qh, _NT, preferred_element_type=F32) for qh in qs]
        logits = [jnp.where(valid, jnp.where(in_cur, b[SWA_BLOCK:], b[:SWA_BLOCK]) + bias_ref[h], NEG_INF)
                  for h, b in zip(heads, boths)]
        ms = [jnp.maximum(jnp.max(l, axis=0, keepdims=True), sink_ref[h]) for h, l in zip(heads, logits)]
        es = [jnp.exp(l - m) for l, m in zip(logits, ms)]
        denoms = [jnp.sum(e, axis=0, keepdims=True) + jnp.exp(sink_ref[h] - m) for h, e, m in zip(heads, es, ms)]
        unfolded = [jnp.concatenate([jnp.where(in_cur, 0.0, e), jnp.where(in_cur, e, 0.0)], axis=0).astype(BF16)
                    for e in es]
        outs = [jnp.dot(vt, u, preferred_element_type=F32) * (1.0 / d) for u, d in zip(unfolded, denoms)]
        for pair in range(SWA_GROUP // 2):
            h0 = heads[2 * pair]
            both_heads = jnp.concatenate(outs[2 * pair:2 * pair + 2], axis=0).T
            o_ref[:, h0 * SWA_HEAD_DIM:(h0 + 2) * SWA_HEAD_DIM] = both_heads.astype(o_ref.dtype)


def folded_relative_bias(rel_table):
    kj = jnp.arange(SWA_BLOCK, dtype=jnp.int32)[:, None]
    qi = jnp.arange(SWA_BLOCK, dtype=jnp.int32)[None, :]
    dist = jnp.where(kj <= qi, qi - kj, qi + SWA_BLOCK - kj)
    max_exact = REL_BUCKETS // 2
    d = jnp.maximum(dist, 0)
    log_ratio = jnp.log(jnp.maximum(d, 1).astype(F32) / max_exact) / math.log(REL_MAX_DIST / max_exact)
    large = jnp.minimum(max_exact + (log_ratio * (REL_BUCKETS - max_exact)).astype(jnp.int32), REL_BUCKETS - 1)
    bucket = jnp.where(d < max_exact, d, large)
    onehot = (bucket[None] == jnp.arange(REL_BUCKETS, dtype=jnp.int32)[:, None, None]).astype(F32)
    return jnp.einsum('bh,bjq->hjq', rel_table.astype(F32), onehot, precision=lax.Precision.HIGHEST)


def swa_attention(proj, sinks, bias, batch, seq):
    n = proj.shape[0]
    nb = seq // SWA_BLOCK
    cur = lambda b, i, s: b * nb + i
    prev = lambda b, i, s: jnp.maximum(b * nb + i - 1, 0)
    blk = (SWA_BLOCK, LANES)
    return pl.pallas_call(
        _attn_kernel,
        grid_spec=pltpu.PrefetchScalarGridSpec(
            num_scalar_prefetch=1,
            grid=(batch, nb),
            in_specs=[
                pl.BlockSpec((SWA_BLOCK, SWA_Q_W), lambda b, i, s: (cur(b, i, s), 0)),
                pl.BlockSpec(blk, lambda b, i, s: (cur(b, i, s), COL_K)),
                pl.BlockSpec(blk, lambda b, i, s: (prev(b, i, s), COL_K)),
                pl.BlockSpec(blk, lambda b, i, s: (cur(b, i, s), COL_V)),
                pl.BlockSpec(blk, lambda b, i, s: (prev(b, i, s), COL_V)),
                pl.BlockSpec((SWA_HEADS, SWA_BLOCK, SWA_BLOCK), lambda b, i, s: (0, 0, 0)),
            ],
            out_specs=pl.BlockSpec((SWA_BLOCK, SWA_Q_W), lambda b, i, s: (cur(b, i, s), 0)),
        ),
        out_shape=jax.ShapeDtypeStruct((n, SWA_Q_W), BF16),
        compiler_params=_params("parallel", "parallel"),
        name="swa_attention",
    )(sinks, proj, proj, proj, proj, proj, bias)


def _causal_conv_silu(cur, prev, w):
    tm = cur.shape[0]
    xp = jnp.concatenate([prev, cur], axis=0)
    first = SUBLANES - (GDN_CONV - 1)
    acc = xp[first:first + tm] * w[0:1]
    for k in range(1, GDN_CONV):
        acc = acc + xp[first + k:first + k + tm] * w[k:k + 1]
    return acc * jax.nn.sigmoid(acc)


def _l2norm(y):
    return y * lax.rsqrt(jnp.sum(y * y, axis=-1, keepdims=True) + EPS)


def _gdn_gate_kernel(h_ref, w_ref, alog_ref, dtb_ref, o_ref):
    tm = h_ref.shape[0]
    t = jnp.dot(h_ref[...], w_ref[...], preferred_element_type=F32)
    lane = lax.broadcasted_iota(jnp.int32, t.shape, 1)
    beta = jax.nn.sigmoid(t)
    g = -jnp.exp(alog_ref[...]) * jax.nn.softplus(t + dtb_ref[...])
    g = jnp.where((lane >= GDN_V_HEADS) & (lane < 2 * GDN_V_HEADS), g, 0.0)
    blk = GDN_BLOCK
    r = lax.broadcasted_iota(jnp.int32, (blk, blk), 0)
    s = lax.broadcasted_iota(jnp.int32, (blk, blk), 1)
    tri = jnp.where(s <= r, 1.0, 0.0).astype(BF16)
    sums = []
    for b in range(tm // blk):
        gb = g[b * blk:(b + 1) * blk]
        hi = gb.astype(BF16)
        rest = gb - hi.astype(F32)
        mid = rest.astype(BF16)
        lo = (rest - mid.astype(F32)).astype(BF16)
        sums.append(jnp.dot(tri, hi, preferred_element_type=F32)
                    + (jnp.dot(tri, mid, preferred_element_type=F32) + jnp.dot(tri, lo, preferred_element_type=F32)))
    gcum = jnp.concatenate(sums, axis=0)
    o_ref[...] = jnp.where(lane < GDN_V_HEADS, beta, gcum)


def gdn_gates(h, w_tail, a_log, dt_bias, tm=512):
    n, k = h.shape
    pad = lambda v: jnp.zeros((1, LANES), F32).at[0, GDN_V_HEADS:2 * GDN_V_HEADS].set(v.astype(F32))
    return pl.pallas_call(
        _gdn_gate_kernel,
        grid=(n // tm,),
        in_specs=[pl.BlockSpec((tm, k), lambda i: (i, 0)),
                  pl.BlockSpec((k, LANES), lambda i: (0, 0)),
                  pl.BlockSpec((1, LANES), lambda i: (0, 0)),
                  pl.BlockSpec((1, LANES), lambda i: (0, 0))],
        out_specs=pl.BlockSpec((tm, LANES), lambda i: (i, 0)),
        out_shape=jax.ShapeDtypeStruct((n, LANES), F32),
        compiler_params=_params("parallel"),
        name="gdn_gates",
    )(h, w_tail, pad(a_log), pad(dt_bias))


def _bdot(a, b, dims=(((1,), (0,)), ((), ()))):
    return lax.dot_general(a.astype(BF16), b.astype(BF16), dims, preferred_element_type=F32)


_NT = (((1,), (1,)), ((), ()))
_TN = (((0,), (0,)), ((), ()))


def _unit_lower_inverses(lows, block):
    r_dim = lows[0].shape[0]
    r = lax.broadcasted_iota(jnp.int32, (r_dim, r_dim), 0)
    s = lax.broadcasted_iota(jnp.int32, (r_dim, r_dim), 1)
    level = 31 - lax.clz(r ^ s)
    size = 1
    step = 0
    xs = None
    while size < block:
        sel = level == step
        step += 1
        offs = [jnp.where(sel, low, 0.0) for low in lows]
        if size == 1:
            eye = jnp.where(r == s, 1.0, 0.0).astype(F32)
            xs = [eye - off for off in offs]
        else:
            xo = [_bdot(x, off) for x, off in zip(xs, offs)]
            xs = [x - _bdot(y, x) for x, y in zip(xs, xo)]
        size *= 2
    return xs


def _gdn_kernel(qin_ref, qprev_ref, kin_ref, kprev_ref, vin_ref, vprev_ref, wq_ref, wk_ref, wv_ref,
                z_ref, gb_ref, grow_ref, norm_ref, o_ref,
                state_ref, q_ref, k_ref, v_ref, w_s, u_s, qd_s, kd_s, qk_s):
    hp = pl.program_id(1)
    t = pl.program_id(2)
    tb = qin_ref.shape[0]
    rt = GDN_TILE
    blk = GDN_BLOCK
    pair = GDN_V_HEADS // GDN_QK_HEADS
    d = GDN_HEAD_DIM

    @pl.when(t == 0)
    def _():
        state_ref[...] = jnp.zeros_like(state_ref)

    halo = lambda ref: jnp.where(t == 0, 0.0, ref[...])
    q_ref[...] = _l2norm(_causal_conv_silu(qin_ref[...], halo(qprev_ref), wq_ref[...]))
    k_ref[...] = _l2norm(_causal_conv_silu(kin_ref[...], halo(kprev_ref), wk_ref[...]))
    v_all = _causal_conv_silu(vin_ref[...], halo(vprev_ref), wv_ref[...])
    for hh in range(pair):
        v_ref[hh] = v_all[:, hh * d:(hh + 1) * d]

    lane = lax.broadcasted_iota(jnp.int32, (tb, LANES), 1)
    gb = gb_ref[...]
    r = lax.broadcasted_iota(jnp.int32, (rt, rt), 0)
    s = lax.broadcasted_iota(jnp.int32, (rt, rt), 1)
    same = (r // blk) == (s // blk)
    causal = (s <= r) & same
    strict = (s < r) & same

    g_heads = []
    beta_heads = []
    for hh in range(pair):
        h = pair * hp + hh
        beta_heads.append(jnp.sum(jnp.where(lane == h, gb, 0.0), axis=-1, keepdims=True))
        g_heads.append(jnp.sum(jnp.where(lane == h + GDN_V_HEADS, gb, 0.0), axis=-1, keepdims=True))

    probs = [(hh, ti, slice(ti * rt, (ti + 1) * rt)) for hh in range(pair) for ti in range(tb // rt)]
    qs = {ti: q_ref[ti * rt:(ti + 1) * rt, :] * (d ** -0.5) for ti in range(tb // rt)}
    ks = {ti: k_ref[ti * rt:(ti + 1) * rt, :] for ti in range(tb // rt)}
    betas = [beta_heads[hh][rows] for hh, ti, rows in probs]
    gs = [g_heads[hh][rows] for hh, ti, rows in probs]
    egs = [jnp.exp(g) for g in gs]
    kbs = [ks[ti] * beta for (hh, ti, rows), beta in zip(probs, betas)]
    kqs = [_bdot(jnp.concatenate([kb, qs[ti]], axis=0), ks[ti], _NT) for (hh, ti, rows), kb in zip(probs, kbs)]
    decays = [jnp.where(causal, jnp.exp(jnp.where(causal, g - grow_ref[ti, hh:hh + 1, :], 0.0)), 0.0)
              for (hh, ti, rows), g in zip(probs, gs)]
    lows = [jnp.where(strict, kq[:rt] * decay, 0.0) for kq, decay in zip(kqs, decays)]
    for (hh, ti, rows), kq, decay, g, eg in zip(probs, kqs, decays, gs, egs):
        qk_s[hh, rows, :] = (kq[rt:] * decay).astype(BF16)
        g_last = jnp.concatenate(
            [jnp.broadcast_to(g[(c + 1) * blk - 1:(c + 1) * blk], (blk, 1)) for c in range(rt // blk)], axis=0)
        qd_s[hh, rows, :] = (qs[ti] * eg).astype(BF16)
        kd_s[hh, rows, :] = (ks[ti] * jnp.exp(g_last - g)).astype(BF16)
    rhss = [jnp.concatenate([v_ref[hh, rows, :] * beta, kb * eg], axis=1).astype(BF16)
            for (hh, ti, rows), beta, kb, eg in zip(probs, betas, kbs, egs)]

    tinvs = _unit_lower_inverses(lows, blk)
    uws = [_bdot(tinv, rhs) for tinv, rhs in zip(tinvs, rhss)]
    for (hh, ti, rows), uw in zip(probs, uws):
        u_s[hh, rows, :] = uw[:, :d]
        w_s[hh, rows, :] = uw[:, d:].astype(BF16)

    heads = range(pair)
    for c in range(tb // blk):
        rows = slice(c * blk, (c + 1) * blk)
        off = (c * blk) % rt
        states = [state_ref[hh] for hh in heads]
        ws = [_bdot(jnp.concatenate([w_s[hh, rows, :], qd_s[hh, rows, :]], axis=0), states[hh]) for hh in heads]
        v_new = [u_s[hh, rows, :] - ws[hh][:blk] for hh in heads]
        outs = [ws[hh][blk:] + _bdot(qk_s[hh, rows, off:off + blk], v_new[hh]) for hh in heads]
        for hh in heads:
            g_end = g_heads[hh][(c + 1) * blk - 1:(c + 1) * blk]
            state_ref[hh] = states[hh] * jnp.exp(g_end) + _bdot(kd_s[hh, rows, :], v_new[hh], _TN)
        for hh in heads:
            o = outs[hh]
            o = o * lax.rsqrt(jnp.mean(o * o, axis=-1, keepdims=True) + EPS) * norm_ref[...]
            zc = z_ref[rows, hh * d:(hh + 1) * d]
            o_ref[rows, hh * d:(hh + 1) * d] = (o * (zc * jax.nn.sigmoid(zc))).astype(o_ref.dtype)


def gdn_mixer(proj, conv_w, gates, g_rows, gdn_norm, batch, seq, tb=512):
    n = proj.shape[0]
    nt = seq // tb
    pair = GDN_V_HEADS // GDN_QK_HEADS
    d = GDN_HEAD_DIM
    row = lambda b, hp, t: b * nt + t
    before = lambda b, hp, t: jnp.maximum(row(b, hp, t) * (tb // SUBLANES) - 1, 0)
    wide = (tb, pair * d)
    q_col = COL_GDN
    k_col = COL_GDN + GDN_QK_HEADS
    v_col = (COL_GDN + 2 * GDN_QK_HEADS) // pair
    z_col = COL_Z // pair
    return pl.pallas_call(
        _gdn_kernel,
        grid=(batch, GDN_QK_HEADS, nt),
        in_specs=[
            pl.BlockSpec((tb, d), lambda b, hp, t: (row(b, hp, t), q_col + hp)),
            pl.BlockSpec((SUBLANES, d), lambda b, hp, t: (before(b, hp, t), q_col + hp)),
            pl.BlockSpec((tb, d), lambda b, hp, t: (row(b, hp, t), k_col + hp)),
            pl.BlockSpec((SUBLANES, d), lambda b, hp, t: (before(b, hp, t), k_col + hp)),
            pl.BlockSpec(wide, lambda b, hp, t: (row(b, hp, t), v_col + hp)),
            pl.BlockSpec((SUBLANES, pair * d), lambda b, hp, t: (before(b, hp, t), v_col + hp)),
            pl.BlockSpec((GDN_CONV, d), lambda b, hp, t: (0, hp)),
            pl.BlockSpec((GDN_CONV, d), lambda b, hp, t: (0, GDN_QK_HEADS + hp)),
            pl.BlockSpec((GDN_CONV, pair * d), lambda b, hp, t: (0, 2 * GDN_QK_HEADS // pair + hp)),
            pl.BlockSpec(wide, lambda b, hp, t: (row(b, hp, t), z_col + hp)),
            pl.BlockSpec((tb, LANES), lambda b, hp, t: (row(b, hp, t), 0)),
            pl.BlockSpec((None, tb // GDN_TILE, pair, GDN_TILE), lambda b, hp, t: (hp, row(b, hp, t), 0, 0)),
            pl.BlockSpec((1, LANES), lambda b, hp, t: (0, 0)),
        ],
        out_specs=pl.BlockSpec(wide, lambda b, hp, t: (row(b, hp, t), hp)),
        out_shape=jax.ShapeDtypeStruct((n, GDN_V_W), BF16),
        scratch_shapes=[
            pltpu.VMEM((pair, d, d), F32),
            pltpu.VMEM((tb, d), F32),
            pltpu.VMEM((tb, d), F32),
            pltpu.VMEM((pair, tb, d), F32),
            pltpu.VMEM((pair, tb, d), BF16),
            pltpu.VMEM((pair, tb, d), F32),
            pltpu.VMEM((pair, tb, d), BF16),
            pltpu.VMEM((pair, tb, d), BF16),
            pltpu.VMEM((pair, tb, GDN_TILE), BF16),
        ],
        compiler_params=_params("parallel", "parallel", "arbitrary"),
        name="gated_delta_rule",
    )(proj, proj, proj, proj, proj, proj, conv_w, conv_w, conv_w, proj, gates, g_rows,
      gdn_norm.reshape(1, LANES).astype(F32))


def _rms(x, gain):
    return x * lax.rsqrt(jnp.mean(x * x, axis=-1, keepdims=True) + EPS) * gain


def _out_proj_kernel(a_ref, b_ref, w_ref, x_ref, g_ref, o_ref, h_ref):
    ka = a_ref.shape[1]
    acc = jnp.dot(a_ref[...], w_ref[:ka, :], preferred_element_type=F32)
    acc += jnp.dot(b_ref[...], w_ref[ka:, :], preferred_element_type=F32)
    x = x_ref[...] + acc
    o_ref[...] = x
    h_ref[...] = _rms(x, g_ref[...]).astype(h_ref.dtype)


def out_proj_residual(attn, gdn, w_out, li, x, next_gain, tm=512):
    n, d = x.shape
    ka, kb = attn.shape[1], gdn.shape[1]
    row = lambda w: pl.BlockSpec((tm, w), lambda i: (i, 0))
    return pl.pallas_call(
        _out_proj_kernel,
        grid=(n // tm,),
        in_specs=[row(ka), row(kb),
                  pl.BlockSpec((None, ka + kb, d), lambda i: (li, 0, 0)),
                  row(d),
                  pl.BlockSpec((1, d), lambda i: (0, 0))],
        out_specs=[row(d), row(d)],
        out_shape=[jax.ShapeDtypeStruct((n, d), F32), jax.ShapeDtypeStruct((n, d), BF16)],
        compiler_params=_params("parallel"),
        name="out_proj_residual",
    )(attn, gdn, w_out, x, next_gain.reshape(1, d))


def _swiglu_tile(h, wg, wu):
    g = jnp.dot(h, wg, preferred_element_type=F32)
    u = jnp.dot(h, wu, preferred_element_type=F32)
    return g * jax.nn.sigmoid(g) * u


def _gate_up_kernel(h_ref, wg_ref, wu_ref, o_ref):
    o_ref[...] = _swiglu_tile(h_ref[...], wg_ref[...], wu_ref[...]).astype(o_ref.dtype)


def gate_up(h, w_gate, w_up, li, tm=512, tn=1024):
    m, k = h.shape
    f = w_gate.shape[2]
    return pl.pallas_call(
        _gate_up_kernel,
        grid=(f // tn, m // tm),
        in_specs=[
            pl.BlockSpec((tm, k), lambda j, i: (i, 0)),
            pl.BlockSpec((None, k, tn), lambda j, i: (li, 0, j)),
            pl.BlockSpec((None, k, tn), lambda j, i: (li, 0, j)),
        ],
        out_specs=pl.BlockSpec((tm, tn), lambda j, i: (i, j)),
        out_shape=jax.ShapeDtypeStruct((m, f), BF16),
        compiler_params=_params("parallel", "parallel"),
        name="swiglu_gate_up",
    )(h, w_gate, w_up)


def _rows_in_use(meta_ref, o_ref, compute):
    i = pl.program_id(1)
    valid = meta_ref[pl.num_programs(1) + i]

    @pl.when(valid > 0)
    def _():
        o_ref[...] = compute(slice(None))

    @pl.when(valid == 0)
    def _():
        o_ref[...] = jnp.zeros_like(o_ref)


def _gate_up_grouped_kernel(meta_ref, h_ref, wg_ref, wu_ref, o_ref):
    _rows_in_use(meta_ref, o_ref, lambda rows: _swiglu_tile(
        h_ref[rows, :].astype(BF16), wg_ref[...].astype(BF16), wu_ref[...].astype(BF16)).astype(o_ref.dtype))


def gate_up_grouped(meta, h, w_gate, w_up, li, tm, tn=1024):
    m, k = h.shape
    f = w_gate.shape[3]
    nb = m // tm
    row = lambda j, i, meta: (jnp.minimum(i, meta[2 * nb] - 1), 0)
    wmap = lambda j, i, meta: (li, meta[i], 0, j)
    return pl.pallas_call(
        _gate_up_grouped_kernel,
        grid_spec=pltpu.PrefetchScalarGridSpec(
            num_scalar_prefetch=1,
            grid=(f // tn, nb),
            in_specs=[
                pl.BlockSpec((tm, k), row),
                pl.BlockSpec((None, None, k, tn), wmap),
                pl.BlockSpec((None, None, k, tn), wmap),
            ],
            out_specs=pl.BlockSpec((tm, tn), lambda j, i, meta: (i, j)),
        ),
        out_shape=jax.ShapeDtypeStruct((m, f), BF16),
        compiler_params=_params("parallel", "arbitrary"),
        name="moe_gate_up",
    )(meta, h, w_gate, w_up)


def _down_kernel(a_ref, w_ref, x_ref, o_ref):
    o_ref[...] = x_ref[...] + jnp.dot(a_ref[...], w_ref[...], preferred_element_type=F32)


def down_residual(a, w_down, li, x, tm=512, tn=512):
    n, d = x.shape
    f = a.shape[1]
    return pl.pallas_call(
        _down_kernel,
        grid=(d // tn, n // tm),
        in_specs=[
            pl.BlockSpec((tm, f), lambda j, i: (i, 0)),
            pl.BlockSpec((None, f, tn), lambda j, i: (li, 0, j)),
            pl.BlockSpec((tm, tn), lambda j, i: (i, j)),
        ],
        out_specs=pl.BlockSpec((tm, tn), lambda j, i: (i, j)),
        out_shape=jax.ShapeDtypeStruct((n, d), F32),
        compiler_params=_params("parallel", "parallel"),
        name="swiglu_down_residual",
    )(a, w_down, x)


def _down_grouped_kernel(meta_ref, a_ref, w_ref, o_ref):
    _rows_in_use(meta_ref, o_ref, lambda rows: jnp.dot(
        a_ref[rows, :], w_ref[...].astype(BF16), preferred_element_type=F32))


def down_grouped(meta, a, w_down, li, tm, tn=512):
    m, f = a.shape
    d = w_down.shape[3]
    nb = m // tm
    return pl.pallas_call(
        _down_grouped_kernel,
        grid_spec=pltpu.PrefetchScalarGridSpec(
            num_scalar_prefetch=1,
            grid=(d // tn, nb),
            in_specs=[
                pl.BlockSpec((tm, f), lambda j, i, meta: (jnp.minimum(i, meta[2 * nb] - 1), 0)),
                pl.BlockSpec((None, None, f, tn), lambda j, i, meta: (li, meta[i], 0, j)),
            ],
            out_specs=pl.BlockSpec((tm, tn), lambda j, i, meta: (i, j)),
        ),
        out_shape=jax.ShapeDtypeStruct((m, d), F32),
        compiler_params=_params("parallel", "arbitrary"),
        name="moe_down",
    )(meta, a, w_down)


def _router_kernel(x_ref, g_ref, w_ref, h_ref, idx_ref, gate_ref):
    x = x_ref[...]
    h = x * lax.rsqrt(jnp.mean(x * x, axis=-1, keepdims=True) + EPS) * g_ref[...]
    h_ref[...] = h.astype(h_ref.dtype)
    w = w_ref[...]
    hh, wh = h.astype(BF16), w.astype(BF16)
    hl, wl = (h - hh.astype(F32)).astype(BF16), (w - wh.astype(F32)).astype(BF16)
    logits = jnp.dot(hh, wh, preferred_element_type=F32) + (
        jnp.dot(hh, wl, preferred_element_type=F32) + jnp.dot(hl, wh, preferred_element_type=F32))
    lane = lax.broadcasted_iota(jnp.int32, logits.shape, 1)
    logits = jnp.where(lane < N_EXPERTS, logits, -jnp.inf)
    m1 = jnp.max(logits, axis=-1, keepdims=True)
    i1 = jnp.min(jnp.where(logits == m1, lane, LANES), axis=-1, keepdims=True)
    rest = jnp.where(lane == i1, -jnp.inf, logits)
    m2 = jnp.max(rest, axis=-1, keepdims=True)
    i2 = jnp.min(jnp.where(rest == m2, lane, LANES), axis=-1, keepdims=True)
    e2 = jnp.exp(m2 - m1)
    denom = 1.0 + e2
    idx_ref[...] = jnp.where(lane == 0, i1, jnp.where(lane == 1, i2, 0))
    gate_ref[...] = jnp.where(lane == 0, 1.0 / denom, jnp.where(lane == 1, e2 / denom, 0.0))


def moe_router(x, gain, w_router, tm=256):
    n, d = x.shape
    w = jnp.zeros((d, LANES), F32).at[:, :N_EXPERTS].set(w_router.astype(F32))
    return pl.pallas_call(
        _router_kernel,
        grid=(n // tm,),
        in_specs=[pl.BlockSpec((tm, d), lambda i: (i, 0)),
                  pl.BlockSpec((1, d), lambda i: (0, 0)),
                  pl.BlockSpec((d, LANES), lambda i: (0, 0))],
        out_specs=[pl.BlockSpec((tm, d), lambda i: (i, 0)),
                   pl.BlockSpec((tm, LANES), lambda i: (i, 0)),
                   pl.BlockSpec((tm, LANES), lambda i: (i, 0))],
        out_shape=[jax.ShapeDtypeStruct((n, d), F32),
                   jax.ShapeDtypeStruct((n, LANES), jnp.int32),
                   jax.ShapeDtypeStruct((n, LANES), F32)],
        compiler_params=_params("parallel"),
        name="moe_router",
    )(x, gain.reshape(1, d), w)


def _combine_kernel(x_ref, a_ref, b_ref, g_ref, o_ref):
    g = g_ref[...]
    o_ref[...] = x_ref[...] + (a_ref[...] * g[:, 0:1] + b_ref[...] * g[:, 1:2])


def moe_combine(x, ya, yb, gates, tm=512):
    n, d = x.shape
    row = pl.BlockSpec((tm, d), lambda i: (i, 0))
    return pl.pallas_call(
        _combine_kernel,
        grid=(n // tm,),
        in_specs=[row, row, row, pl.BlockSpec((tm, LANES), lambda i: (i, 0))],
        out_specs=row,
        out_shape=jax.ShapeDtypeStruct((n, d), F32),
        compiler_params=_params("parallel"),
        name="moe_combine",
    )(x, ya, yb, gates)


def _dispatch_kernel(fill_ref, dest_ref, h_ref, xs_ref, zero_ref, sem):
    i = pl.program_id(0)
    groups_per_tile = h_ref.shape[0]

    def slot(r):
        shift = SUBLANES.bit_length() - 1
        return xs_ref.at[lax.shift_right_logical(r, shift), pl.ds(lax.bitwise_and(r, SUBLANES - 1), 1)]

    def run(count, make_copies):
        def start(g, c):
            for cp in make_copies(g):
                cp.start()
            return c

        def wait(g, c):
            for cp in make_copies(g):
                cp.wait()
            return c

        lax.fori_loop(0, count, start, 0)
        lax.fori_loop(0, count, wait, 0)

    @pl.when(i == 0)
    def _():
        zero_ref[...] = jnp.zeros_like(zero_ref)
        runs = fill_ref.shape[0] // 2
        for e in range(runs):
            first = fill_ref[e]
            run(fill_ref[runs + e], lambda a: [pltpu.make_async_copy(zero_ref.at[pl.ds(0, 1)], slot(first + a), sem)])

    def group_copies(g):
        return [pltpu.make_async_copy(h_ref.at[g, pl.ds(u, 1)],
                                      slot(dest_ref[0, (g * SUBLANES + u) * TOP_K + k]), sem)
                for u in range(SUBLANES) for k in range(TOP_K)]

    run(groups_per_tile, group_copies)


def moe_dispatch(h, dest, fill, cap, tm=256):
    n, d = h.shape
    xs = pl.pallas_call(
        _dispatch_kernel,
        grid_spec=pltpu.PrefetchScalarGridSpec(
            num_scalar_prefetch=1,
            grid=(n // tm,),
            in_specs=[
                pl.BlockSpec((None, 1, TOP_K * tm), lambda i, fill: (i, 0, 0), memory_space=pltpu.SMEM),
                pl.BlockSpec((tm // SUBLANES, SUBLANES, d), lambda i, fill: (i, 0, 0)),
            ],
            out_specs=pl.BlockSpec(memory_space=pl.ANY),
            scratch_shapes=[pltpu.VMEM((SUBLANES, d), h.dtype), pltpu.SemaphoreType.DMA(())],
        ),
        out_shape=jax.ShapeDtypeStruct((cap // SUBLANES, SUBLANES, d), h.dtype),
        compiler_params=_params("arbitrary"),
        name="moe_dispatch",
    )(fill, dest.reshape(n // tm, 1, TOP_K * tm), h.reshape(n // SUBLANES, SUBLANES, d))
    return xs.reshape(cap, d)


def moe_block(x, gain, w_router, w_gate, w_up, w_down, li, tm=512):
    n, d = x.shape
    h, idx, gates = moe_router(x, gain, w_router)
    flat_e = idx[:, :TOP_K].reshape(-1)
    na = n * TOP_K
    onehot = (flat_e[None, :] == jnp.arange(N_EXPERTS, dtype=jnp.int32)[:, None]).astype(jnp.int32)
    csum = jnp.cumsum(onehot, axis=1)
    counts = csum[:, -1]
    padded = (counts + tm - 1) // tm * tm
    pad_end = jnp.cumsum(padded)
    pad_start = pad_end - padded
    dest = jnp.sum(onehot * (csum - 1 + pad_start[:, None]), axis=0)
    n_blocks = na // tm + N_EXPERTS
    cap = n_blocks * tm
    fill = jnp.concatenate([pad_start + counts, pad_end[-1:], padded - counts, cap - pad_end[-1:]]).astype(jnp.int32)
    block_e = jnp.minimum(
        jnp.searchsorted(pad_end, jnp.arange(n_blocks, dtype=jnp.int32) * tm, side='right'),
        N_EXPERTS - 1).astype(jnp.int32)
    block_rows = jnp.clip((pad_start + counts)[block_e] - jnp.arange(n_blocks, dtype=jnp.int32) * tm, 0, tm)
    meta = jnp.concatenate([block_e, block_rows, pad_end[-1:] // tm]).astype(jnp.int32)
    rows = lambda a, idx: a.at[idx].get(mode="promise_in_bounds")
    xs = moe_dispatch(h, dest, fill, cap)
    act = gate_up_grouped(meta, xs, w_gate, w_up, li, tm)
    ys = down_grouped(meta, act, w_down, li, tm)
    dest2 = dest.reshape(n, TOP_K)
    return moe_combine(x, rows(ys, dest2[:, 0]), rows(ys, dest2[:, 1]), gates)


def _ple_kernel(gain_ref, wg_ref, p_ref, wp_ref, x_ref, next_gain_ref, o_ref, hn_ref):
    x = x_ref[...]
    h = _rms(x, gain_ref[...]).astype(BF16)
    gate = jax.nn.sigmoid(jnp.dot(h, wg_ref[...], preferred_element_type=F32))
    proj = jnp.dot(p_ref[...].astype(BF16), wp_ref[...].astype(BF16), preferred_element_type=F32)
    x = x + proj * gate
    o_ref[...] = x
    hn_ref[...] = _rms(x, next_gain_ref[...]).astype(hn_ref.dtype)


def ple_residual(gain, w_gate, p, w_proj, li, x, next_gain, next_dtype, tm=512):
    n, d = x.shape
    kp = p.shape[2]
    row = lambda w: pl.BlockSpec((tm, w), lambda i: (i, 0))
    vec = pl.BlockSpec((1, d), lambda i: (0, 0))
    return pl.pallas_call(
        _ple_kernel,
        grid=(n // tm,),
        in_specs=[
            vec,
            pl.BlockSpec((None, d, d), lambda i: (li, 0, 0)),
            pl.BlockSpec((None, tm, kp), lambda i: (li, i, 0)),
            pl.BlockSpec((None, kp, d), lambda i: (li, 0, 0)),
            row(d),
            vec,
        ],
        out_specs=[row(d), row(d)],
        out_shape=[jax.ShapeDtypeStruct((n, d), F32), jax.ShapeDtypeStruct((n, d), next_dtype)],
        compiler_params=_params("parallel"),
        name="ple_residual",
    )(gain.reshape(1, d), w_gate, p, w_proj, x, next_gain.reshape(1, d))


def hybrid_mixer_residual(x, h, w_in, li, conv_w, a_log, dt_bias, gdn_norm, sinks, bias, w_out, next_gain,
                          batch, seq):
    n = x.shape[0]
    proj = matmul_cols_outer(h, w_in, li, IN_MAIN_W, tm=512, tn=IN_MAIN_W // 2)
    w_tail = jnp.zeros((D_MODEL, LANES), BF16).at[:, :IN_TAIL_W].set(w_in[li, :, IN_MAIN_W:])
    attn = swa_attention(proj, sinks.astype(F32), bias, batch, seq)
    gates = gdn_gates(h, w_tail, a_log, dt_bias)
    pair = GDN_V_HEADS // GDN_QK_HEADS
    g_rows = gates[:, GDN_V_HEADS:2 * GDN_V_HEADS].T.reshape(GDN_QK_HEADS, pair, n // GDN_TILE, GDN_TILE)
    g_rows = g_rows.transpose(0, 2, 1, 3)
    gdn = gdn_mixer(proj, conv_w, gates, g_rows, gdn_norm, batch, seq)
    return out_proj_residual(attn, gdn, w_out, li, x, next_gain)


def kernel(x, p, w_in, conv_w, a_log, dt_bias, gdn_norm, attn_sinks, rel_bias_table, w_out, norm_mix, norm_ffn, w_dense_gate, w_dense_up, w_dense_down, w_router, w_exp_gate, w_exp_up, w_exp_down, norm_ple, w_ple_gate, w_ple_proj, norm_final):
    batch, seq, d = x.shape
    depth = w_in.shape[0]
    n = batch * seq
    out_dtype = x.dtype
    x = x.reshape(n, d).astype(F32)
    assert WINDOW == SWA_BLOCK
    bias = folded_relative_bias(rel_bias_table)
    p = p.reshape(depth, n, PLE_DIM)
    w_in, w_out, w_ple_gate = w_in.astype(BF16), w_out.astype(BF16), w_ple_gate.astype(BF16)
    w_dense_gate, w_dense_up, w_dense_down = (w.astype(BF16) for w in (w_dense_gate, w_dense_up, w_dense_down))
    h = rmsnorm_rows(x, norm_mix[0], BF16)
    for i in range(depth):
        x, hf = hybrid_mixer_residual(x, h, w_in, i, conv_w[i], a_log[i], dt_bias[i], gdn_norm[i],
                                      attn_sinks[i], bias, w_out, norm_ffn[i], batch, seq)
        if i % 2 == 0:
            act = gate_up(hf, w_dense_gate, w_dense_up, i // 2)
            x = down_residual(act, w_dense_down, i // 2, x)
        else:
            x = moe_block(x, norm_ffn[i], w_router[i // 2], w_exp_gate, w_exp_up, w_exp_down, i // 2)
        last = i == depth - 1
        x, h = ple_residual(norm_ple[i], w_ple_gate, p, w_ple_proj, i, x,
                            norm_final if last else norm_mix[i + 1], out_dtype if last else BF16)
    return h.reshape(batch, seq, d)
```

```python
import functools
import math

import jax
import jax.numpy as jnp
from jax import lax
from jax.experimental import pallas as pl
from jax.experimental.pallas import tpu as pltpu

F32 = jnp.float32
BF16 = jnp.bfloat16

D_MODEL = 2048
PLE_DIM = 256
EPS = 1e-6
NEG_INF = -1e30

SWA_HEADS = 16
SWA_KV_HEADS = 2
SWA_HEAD_DIM = 64
SWA_GROUP = SWA_HEADS // SWA_KV_HEADS
WINDOW = 128
SWA_BLOCK = 128
REL_BUCKETS = 32
REL_MAX_DIST = 128

GDN_QK_HEADS = 4
GDN_V_HEADS = 8
GDN_HEAD_DIM = 128
GDN_CONV = 4
GDN_TILE = 256
GDN_BLOCK = 256

SWA_Q_W = SWA_HEADS * SWA_HEAD_DIM
SWA_KV_W = SWA_KV_HEADS * SWA_HEAD_DIM
GDN_QK_W = GDN_QK_HEADS * GDN_HEAD_DIM
GDN_V_W = GDN_V_HEADS * GDN_HEAD_DIM
GDN_CONV_CH = 2 * GDN_QK_W + GDN_V_W
MIX_WIDTH = SWA_Q_W + GDN_V_W
IN_MAIN_W = SWA_Q_W + 2 * SWA_KV_W + GDN_CONV_CH + GDN_V_W
IN_TAIL_W = 2 * GDN_V_HEADS

D_FF = 7 * D_MODEL // 2
N_EXPERTS = 8
TOP_K = 2

LANES = 128
SUBLANES = 8
VMEM_LIMIT = 56 * 1024 * 1024

COL_K = SWA_Q_W // LANES
COL_V = COL_K + SWA_KV_W // LANES
COL_GDN = COL_V + SWA_KV_W // LANES
COL_Z = COL_GDN + GDN_CONV_CH // LANES


def _params(*sem):
    return pltpu.CompilerParams(dimension_semantics=sem, vmem_limit_bytes=VMEM_LIMIT)


def _rms_kernel(x_ref, g_ref, o_ref):
    x = x_ref[...]
    y = x * lax.rsqrt(jnp.mean(x * x, axis=-1, keepdims=True) + EPS)
    o_ref[...] = (y * g_ref[...]).astype(o_ref.dtype)


def rmsnorm_rows(x, gain, out_dtype, tm=512):
    n, d = x.shape
    return pl.pallas_call(
        _rms_kernel,
        grid=(n // tm,),
        in_specs=[pl.BlockSpec((tm, d), lambda i: (i, 0)), pl.BlockSpec((1, d), lambda i: (0, 0))],
        out_specs=pl.BlockSpec((tm, d), lambda i: (i, 0)),
        out_shape=jax.ShapeDtypeStruct((n, d), out_dtype),
        compiler_params=_params("parallel"),
        name="rmsnorm",
    )(x, gain.reshape(1, d))


def _mm_kernel(a_ref, w_ref, o_ref):
    o_ref[...] = jnp.dot(a_ref[...], w_ref[...], preferred_element_type=F32).astype(o_ref.dtype)


def matmul_cols_outer(a, w, li, n_out, tm, tn, out_dtype=F32):
    m, k = a.shape
    return pl.pallas_call(
        _mm_kernel,
        grid=(n_out // tn, m // tm),
        in_specs=[pl.BlockSpec((tm, k), lambda j, i: (i, 0)),
                  pl.BlockSpec((None, k, tn), lambda j, i: (li, 0, j))],
        out_specs=pl.BlockSpec((tm, tn), lambda j, i: (i, j)),
        out_shape=jax.ShapeDtypeStruct((m, n_out), out_dtype),
        compiler_params=_params("parallel", "parallel"),
        name="in_proj",
    )(a, w)


def _attn_kernel(sink_ref, q_ref, kc_ref, kp_ref, vc_ref, vp_ref, bias_ref, o_ref):
    n = pl.program_id(1)
    kj = lax.broadcasted_iota(jnp.int32, (SWA_BLOCK, SWA_BLOCK), 0)
    qi = lax.broadcasted_iota(jnp.int32, (SWA_BLOCK, SWA_BLOCK), 1)
    in_cur = kj <= qi
    valid = in_cur | (n > 0)
    scale = SWA_HEAD_DIM ** -0.5
    for g in range(SWA_KV_HEADS):
        cols = slice(g * SWA_HEAD_DIM, (g + 1) * SWA_HEAD_DIM)
        kw = jnp.concatenate([kp_ref[:, cols], kc_ref[:, cols]], axis=0).astype(BF16)
        vt = jnp.concatenate([vp_ref[:, cols], vc_ref[:, cols]], axis=0).T.astype(BF16)
        heads = [g * SWA_GROUP + hh for hh in range(SWA_GROUP)]
        qs = [(q_ref[:, h * SWA_HEAD_DIM:(h + 1) * SWA_HEAD_DIM] * scale).astype(BF16) for h in heads]
        boths = [lax.dot_general(kw, qh, _NT, preferred_element_type=F32) for qh in qs]
        logits = [jnp.where(valid, jnp.where(in_cur, b[SWA_BLOCK:], b[:SWA_BLOCK]) + bias_ref[h], NEG_INF)
                  for h, b in zip(heads, boths)]
        ms = [jnp.maximum(jnp.max(l, axis=0, keepdims=True), sink_ref[h]) for h, l in zip(heads, logits)]
        es = [jnp.exp(l - m) for l, m in zip(logits, ms)]
        denoms = [jnp.sum(e, axis=0, keepdims=True) + jnp.exp(sink_ref[h] - m) for h, e, m in zip(heads, es, ms)]
        unfolded = [jnp.concatenate([jnp.where(in_cur, 0.0, e), jnp.where(in_cur, e, 0.0)], axis=0).astype(BF16)
                    for e in es]
        outs = [jnp.dot(vt, u, preferred_element_type=F32) * (1.0 / d) for u, d in zip(unfolded, denoms)]
        for pair in range(SWA_GROUP // 2):
            h0 = heads[2 * pair]
            both_heads = jnp.concatenate(outs[2 * pair:2 * pair + 2], axis=0).T
            o_ref[:, h0 * SWA_HEAD_DIM:(h0 + 2) * SWA_HEAD_DIM] = both_heads.astype(o_ref.dtype)


def folded_relative_bias(rel_table):
    kj = jnp.arange(SWA_BLOCK, dtype=jnp.int32)[:, None]
    qi = jnp.arange(SWA_BLOCK, dtype=jnp.int32)[None, :]
    dist = jnp.where(kj <= qi, qi - kj, qi + SWA_BLOCK - kj)
    max_exact = REL_BUCKETS // 2
    d = jnp.maximum(dist, 0)
    log_ratio = jnp.log(jnp.maximum(d, 1).astype(F32) / max_exact) / math.log(REL_MAX_DIST / max_exact)
    large = jnp.minimum(max_exact + (log_ratio * (REL_BUCKETS - max_exact)).astype(jnp.int32), REL_BUCKETS - 1)
    bucket = jnp.where(d < max_exact, d, large)
    onehot = (bucket[None] == jnp.arange(REL_BUCKETS, dtype=jnp.int32)[:, None, None]).astype(F32)
    return jnp.einsum('bh,bjq->hjq', rel_table.astype(F32), onehot, precision=lax.Precision.HIGHEST)


def swa_attention(proj, sinks, bias, batch, seq):
    n = proj.shape[0]
    nb = seq // SWA_BLOCK
    cur = lambda b, i, s: b * nb + i
    prev = lambda b, i, s: jnp.maximum(b * nb + i - 1, 0)
    blk = (SWA_BLOCK, LANES)
    return pl.pallas_call(
        _attn_kernel,
        grid_spec=pltpu.PrefetchScalarGridSpec(
            num_scalar_prefetch=1,
            grid=(batch, nb),
            in_specs=[
                pl.BlockSpec((SWA_BLOCK, SWA_Q_W), lambda b, i, s: (cur(b, i, s), 0)),
                pl.BlockSpec(blk, lambda b, i, s: (cur(b, i, s), COL_K)),
                pl.BlockSpec(blk, lambda b, i, s: (prev(b, i, s), COL_K)),
                pl.BlockSpec(blk, lambda b, i, s: (cur(b, i, s), COL_V)),
                pl.BlockSpec(blk, lambda b, i, s: (prev(b, i, s), COL_V)),
                pl.BlockSpec((SWA_HEADS, SWA_BLOCK, SWA_BLOCK), lambda b, i, s: (0, 0, 0)),
            ],
            out_specs=pl.BlockSpec((SWA_BLOCK, SWA_Q_W), lambda b, i, s: (cur(b, i, s), 0)),
        ),
        out_shape=jax.ShapeDtypeStruct((n, SWA_Q_W), BF16),
        compiler_params=_params("parallel", "parallel"),
        name="swa_attention",
    )(sinks, proj, proj, proj, proj, proj, bias)


def _causal_conv_silu(cur, prev, w):
    tm = cur.shape[0]
    xp = jnp.concatenate([prev, cur], axis=0)
    first = SUBLANES - (GDN_CONV - 1)
    acc = xp[first:first + tm] * w[0:1]
    for k in range(1, GDN_CONV):
        acc = acc + xp[first + k:first + k + tm] * w[k:k + 1]
    return acc * jax.nn.sigmoid(acc)


def _l2norm(y):
    return y * lax.rsqrt(jnp.sum(y * y, axis=-1, keepdims=True) + EPS)


def _gdn_gate_kernel(h_ref, w_ref, alog_ref, dtb_ref, o_ref):
    tm = h_ref.shape[0]
    t = jnp.dot(h_ref[...], w_ref[...], preferred_element_type=F32)
    lane = lax.broadcasted_iota(jnp.int32, t.shape, 1)
    beta = jax.nn.sigmoid(t)
    g = -jnp.exp(alog_ref[...]) * jax.nn.softplus(t + dtb_ref[...])
    g = jnp.where((lane >= GDN_V_HEADS) & (lane < 2 * GDN_V_HEADS), g, 0.0)
    blk = GDN_BLOCK
    r = lax.broadcasted_iota(jnp.int32, (blk, blk), 0)
    s = lax.broadcasted_iota(jnp.int32, (blk, blk), 1)
    tri = jnp.where(s <= r, 1.0, 0.0).astype(BF16)
    sums = []
    for b in range(tm // blk):
        gb = g[b * blk:(b + 1) * blk]
        hi = gb.astype(BF16)
        rest = gb - hi.astype(F32)
        mid = rest.astype(BF16)
        lo = (rest - mid.astype(F32)).astype(BF16)
        sums.append(jnp.dot(tri, hi, preferred_element_type=F32)
                    + (jnp.dot(tri, mid, preferred_element_type=F32) + jnp.dot(tri, lo, preferred_element_type=F32)))
    gcum = jnp.concatenate(sums, axis=0)
    o_ref[...] = jnp.where(lane < GDN_V_HEADS, beta, gcum)


def gdn_gates(h, w_tail, a_log, dt_bias, tm=512):
    n, k = h.shape
    pad = lambda v: jnp.zeros((1, LANES), F32).at[0, GDN_V_HEADS:2 * GDN_V_HEADS].set(v.astype(F32))
    return pl.pallas_call(
        _gdn_gate_kernel,
        grid=(n // tm,),
        in_specs=[pl.BlockSpec((tm, k), lambda i: (i, 0)),
                  pl.BlockSpec((k, LANES), lambda i: (0, 0)),
                  pl.BlockSpec((1, LANES), lambda i: (0, 0)),
                  pl.BlockSpec((1, LANES), lambda i: (0, 0))],
        out_specs=pl.BlockSpec((tm, LANES), lambda i: (i, 0)),
        out_shape=jax.ShapeDtypeStruct((n, LANES), F32),
        compiler_params=_params("parallel"),
        name="gdn_gates",
    )(h, w_tail, pad(a_log), pad(dt_bias))


def _bdot(a, b, dims=(((1,), (0,)), ((), ()))):
    return lax.dot_general(a.astype(BF16), b.astype(BF16), dims, preferred_element_type=F32)


_NT = (((1,), (1,)), ((), ()))
_TN = (((0,), (0,)), ((), ()))


def _unit_lower_inverses(lows, block):
    r_dim = lows[0].shape[0]
    r = lax.broadcasted_iota(jnp.int32, (r_dim, r_dim), 0)
    s = lax.broadcasted_iota(jnp.int32, (r_dim, r_dim), 1)
    level = 31 - lax.clz(r ^ s)
    size = 1
    step = 0
    xs = None
    while size < block:
        sel = level == step
        step += 1
        offs = [jnp.where(sel, low, 0.0) for low in lows]
        if size == 1:
            eye = jnp.where(r == s, 1.0, 0.0).astype(F32)
            xs = [eye - off for off in offs]
        else:
            xo = [_bdot(x, off) for x, off in zip(xs, offs)]
            xs = [x - _bdot(y, x) for x, y in zip(xs, xo)]
        size *= 2
    return xs


def _gdn_kernel(qin_ref, qprev_ref, kin_ref, kprev_ref, vin_ref, vprev_ref, wq_ref, wk_ref, wv_ref,
                z_ref, gb_ref, grow_ref, norm_ref, o_ref,
                state_ref, q_ref, k_ref, v_ref, w_s, u_s, qd_s, kd_s, qk_s):
    hp = pl.program_id(1)
    t = pl.program_id(2)
    tb = qin_ref.shape[0]
    rt = GDN_TILE
    blk = GDN_BLOCK
    pair = GDN_V_HEADS // GDN_QK_HEADS
    d = GDN_HEAD_DIM

    @pl.when(t == 0)
    def _():
        state_ref[...] = jnp.zeros_like(state_ref)

    halo = lambda ref: jnp.where(t == 0, 0.0, ref[...])
    q_ref[...] = _l2norm(_causal_conv_silu(qin_ref[...], halo(qprev_ref), wq_ref[...]))
    k_ref[...] = _l2norm(_causal_conv_silu(kin_ref[...], halo(kprev_ref), wk_ref[...]))
    v_all = _causal_conv_silu(vin_ref[...], halo(vprev_ref), wv_ref[...])
    for hh in range(pair):
        v_ref[hh] = v_all[:, hh * d:(hh + 1) * d]

    lane = lax.broadcasted_iota(jnp.int32, (tb, LANES), 1)
    gb = gb_ref[...]
    r = lax.broadcasted_iota(jnp.int32, (rt, rt), 0)
    s = lax.broadcasted_iota(jnp.int32, (rt, rt), 1)
    same = (r // blk) == (s // blk)
    causal = (s <= r) & same
    strict = (s < r) & same

    g_heads = []
    beta_heads = []
    for hh in range(pair):
        h = pair * hp + hh
        beta_heads.append(jnp.sum(jnp.where(lane == h, gb, 0.0), axis=-1, keepdims=True))
        g_heads.append(jnp.sum(jnp.where(lane == h + GDN_V_HEADS, gb, 0.0), axis=-1, keepdims=True))

    probs = [(hh, ti, slice(ti * rt, (ti + 1) * rt)) for hh in range(pair) for ti in range(tb // rt)]
    qs = {ti: q_ref[ti * rt:(ti + 1) * rt, :] * (d ** -0.5) for ti in range(tb // rt)}
    ks = {ti: k_ref[ti * rt:(ti + 1) * rt, :] for ti in range(tb // rt)}
    betas = [beta_heads[hh][rows] for hh, ti, rows in probs]
    gs = [g_heads[hh][rows] for hh, ti, rows in probs]
    egs = [jnp.exp(g) for g in gs]
    kbs = [ks[ti] * beta for (hh, ti, rows), beta in zip(probs, betas)]
    kqs = [_bdot(jnp.concatenate([kb, qs[ti]], axis=0), ks[ti], _NT) for (hh, ti, rows), kb in zip(probs, kbs)]
    decays = [jnp.where(causal, jnp.exp(jnp.where(causal, g - grow_ref[ti, hh:hh + 1, :], 0.0)), 0.0)
              for (hh, ti, rows), g in zip(probs, gs)]
    lows = [jnp.where(strict, kq[:rt] * decay, 0.0) for kq, decay in zip(kqs, decays)]
    for (hh, ti, rows), kq, decay, g, eg in zip(probs, kqs, decays, gs, egs):
        qk_s[hh, rows, :] = (kq[rt:] * decay).astype(BF16)
        g_last = jnp.concatenate(
            [jnp.broadcast_to(g[(c + 1) * blk - 1:(c + 1) * blk], (blk, 1)) for c in range(rt // blk)], axis=0)
        qd_s[hh, rows, :] = (qs[ti] * eg).astype(BF16)
        kd_s[hh, rows, :] = (ks[ti] * jnp.exp(g_last - g)).astype(BF16)
    rhss = [jnp.concatenate([v_ref[hh, rows, :] * beta, kb * eg], axis=1).astype(BF16)
            for (hh, ti, rows), beta, kb, eg in zip(probs, betas, kbs, egs)]

    tinvs = _unit_lower_inverses(lows, blk)
    uws = [_bdot(tinv, rhs) for tinv, rhs in zip(tinvs, rhss)]
    for (hh, ti, rows), uw in zip(probs, uws):
        u_s[hh, rows, :] = uw[:, :d]
        w_s[hh, rows, :] = uw[:, d:].astype(BF16)

    heads = range(pair)
    for c in range(tb // blk):
        rows = slice(c * blk, (c + 1) * blk)
        off = (c * blk) % rt
        states = [state_ref[hh] for hh in heads]
        ws = [_bdot(jnp.concatenate([w_s[hh, rows, :], qd_s[hh, rows, :]], axis=0), states[hh]) for hh in heads]
        v_new = [u_s[hh, rows, :] - ws[hh][:blk] for hh in heads]
        outs = [ws[hh][blk:] + _bdot(qk_s[hh, rows, off:off + blk], v_new[hh]) for hh in heads]
        for hh in heads:
            g_end = g_heads[hh][(c + 1) * blk - 1:(c + 1) * blk]
            state_ref[hh] = states[hh] * jnp.exp(g_end) + _bdot(kd_s[hh, rows, :], v_new[hh], _TN)
        for hh in heads:
            o = outs[hh]
            o = o * lax.rsqrt(jnp.mean(o * o, axis=-1, keepdims=True) + EPS) * norm_ref[...]
            zc = z_ref[rows, hh * d:(hh + 1) * d]
            o_ref[rows, hh * d:(hh + 1) * d] = (o * (zc * jax.nn.sigmoid(zc))).astype(o_ref.dtype)


def gdn_mixer(proj, conv_w, gates, g_rows, gdn_norm, batch, seq, tb=512):
    n = proj.shape[0]
    nt = seq // tb
    pair = GDN_V_HEADS // GDN_QK_HEADS
    d = GDN_HEAD_DIM
    row = lambda b, hp, t: b * nt + t
    before = lambda b, hp, t: jnp.maximum(row(b, hp, t) * (tb // SUBLANES) - 1, 0)
    wide = (tb, pair * d)
    q_col = COL_GDN
    k_col = COL_GDN + GDN_QK_HEADS
    v_col = (COL_GDN + 2 * GDN_QK_HEADS) // pair
    z_col = COL_Z // pair
    return pl.pallas_call(
        _gdn_kernel,
        grid=(batch, GDN_QK_HEADS, nt),
        in_specs=[
            pl.BlockSpec((tb, d), lambda b, hp, t: (row(b, hp, t), q_col + hp)),
            pl.BlockSpec((SUBLANES, d), lambda b, hp, t: (before(b, hp, t), q_col + hp)),
            pl.BlockSpec((tb, d), lambda b, hp, t: (row(b, hp, t), k_col + hp)),
            pl.BlockSpec((SUBLANES, d), lambda b, hp, t: (before(b, hp, t), k_col + hp)),
            pl.BlockSpec(wide, lambda b, hp, t: (row(b, hp, t), v_col + hp)),
            pl.BlockSpec((SUBLANES, pair * d), lambda b, hp, t: (before(b, hp, t), v_col + hp)),
            pl.BlockSpec((GDN_CONV, d), lambda b, hp, t: (0, hp)),
            pl.BlockSpec((GDN_CONV, d), lambda b, hp, t: (0, GDN_QK_HEADS + hp)),
            pl.BlockSpec((GDN_CONV, pair * d), lambda b, hp, t: (0, 2 * GDN_QK_HEADS // pair + hp)),
            pl.BlockSpec(wide, lambda b, hp, t: (row(b, hp, t), z_col + hp)),
            pl.BlockSpec((tb, LANES), lambda b, hp, t: (row(b, hp, t), 0)),
            pl.BlockSpec((None, tb // GDN_TILE, pair, GDN_TILE), lambda b, hp, t: (hp, row(b, hp, t), 0, 0)),
            pl.BlockSpec((1, LANES), lambda b, hp, t: (0, 0)),
        ],
        out_specs=pl.BlockSpec(wide, lambda b, hp, t: (row(b, hp, t), hp)),
        out_shape=jax.ShapeDtypeStruct((n, GDN_V_W), BF16),
        scratch_shapes=[
            pltpu.VMEM((pair, d, d), F32),
            pltpu.VMEM((tb, d), F32),
            pltpu.VMEM((tb, d), F32),
            pltpu.VMEM((pair, tb, d), F32),
            pltpu.VMEM((pair, tb, d), BF16),
            pltpu.VMEM((pair, tb, d), F32),
            pltpu.VMEM((pair, tb, d), BF16),
            pltpu.VMEM((pair, tb, d), BF16),
            pltpu.VMEM((pair, tb, GDN_TILE), BF16),
        ],
        compiler_params=_params("parallel", "parallel", "arbitrary"),
        name="gated_delta_rule",
    )(proj, proj, proj, proj, proj, proj, conv_w, conv_w, conv_w, proj, gates, g_rows,
      gdn_norm.reshape(1, LANES).astype(F32))


def _rms(x, gain):
    return x * lax.rsqrt(jnp.mean(x * x, axis=-1, keepdims=True) + EPS) * gain


def _out_proj_kernel(a_ref, b_ref, w_ref, x_ref, g_ref, o_ref, h_ref):
    ka = a_ref.shape[1]
    acc = jnp.dot(a_ref[...], w_ref[:ka, :], preferred_element_type=F32)
    acc += jnp.dot(b_ref[...], w_ref[ka:, :], preferred_element_type=F32)
    x = x_ref[...] + acc
    o_ref[...] = x
    h_ref[...] = _rms(x, g_ref[...]).astype(h_ref.dtype)


def out_proj_residual(attn, gdn, w_out, li, x, next_gain, tm=512):
    n, d = x.shape
    ka, kb = attn.shape[1], gdn.shape[1]
    row = lambda w: pl.BlockSpec((tm, w), lambda i: (i, 0))
    return pl.pallas_call(
        _out_proj_kernel,
        grid=(n // tm,),
        in_specs=[row(ka), row(kb),
                  pl.BlockSpec((None, ka + kb, d), lambda i: (li, 0, 0)),
                  row(d),
                  pl.BlockSpec((1, d), lambda i: (0, 0))],
        out_specs=[row(d), row(d)],
        out_shape=[jax.ShapeDtypeStruct((n, d), F32), jax.ShapeDtypeStruct((n, d), BF16)],
        compiler_params=_params("parallel"),
        name="out_proj_residual",
    )(attn, gdn, w_out, x, next_gain.reshape(1, d))


def _swiglu_tile(h, wg, wu):
    g = jnp.dot(h, wg, preferred_element_type=F32)
    u = jnp.dot(h, wu, preferred_element_type=F32)
    return g * jax.nn.sigmoid(g) * u


def _gate_up_kernel(h_ref, wg_ref, wu_ref, o_ref):
    o_ref[...] = _swiglu_tile(h_ref[...], wg_ref[...], wu_ref[...]).astype(o_ref.dtype)


def gate_up(h, w_gate, w_up, li, tm=512, tn=1024):
    m, k = h.shape
    f = w_gate.shape[2]
    return pl.pallas_call(
        _gate_up_kernel,
        grid=(f // tn, m // tm),
        in_specs=[
            pl.BlockSpec((tm, k), lambda j, i: (i, 0)),
            pl.BlockSpec((None, k, tn), lambda j, i: (li, 0, j)),
            pl.BlockSpec((None, k, tn), lambda j, i: (li, 0, j)),
        ],
        out_specs=pl.BlockSpec((tm, tn), lambda j, i: (i, j)),
        out_shape=jax.ShapeDtypeStruct((m, f), BF16),
        compiler_params=_params("parallel", "parallel"),
        name="swiglu_gate_up",
    )(h, w_gate, w_up)


def _rows_in_use(meta_ref, o_ref, compute):
    i = pl.program_id(1)
    valid = meta_ref[pl.num_programs(1) + i]

    @pl.when(valid > 0)
    def _():
        o_ref[...] = compute(slice(None))

    @pl.when(valid == 0)
    def _():
        o_ref[...] = jnp.zeros_like(o_ref)


def _gate_up_grouped_kernel(meta_ref, h_ref, wg_ref, wu_ref, o_ref):
    _rows_in_use(meta_ref, o_ref, lambda rows: _swiglu_tile(
        h_ref[rows, :].astype(BF16), wg_ref[...].astype(BF16), wu_ref[...].astype(BF16)).astype(o_ref.dtype))


def gate_up_grouped(meta, h, w_gate, w_up, li, tm, tn=1024):
    m, k = h.shape
    f = w_gate.shape[3]
    nb = m // tm
    row = lambda j, i, meta: (jnp.minimum(i, meta[2 * nb] - 1), 0)
    wmap = lambda j, i, meta: (li, meta[i], 0, j)
    return pl.pallas_call(
        _gate_up_grouped_kernel,
        grid_spec=pltpu.PrefetchScalarGridSpec(
            num_scalar_prefetch=1,
            grid=(f // tn, nb),
            in_specs=[
                pl.BlockSpec((tm, k), row),
                pl.BlockSpec((None, None, k, tn), wmap),
                pl.BlockSpec((None, None, k, tn), wmap),
            ],
            out_specs=pl.BlockSpec((tm, tn), lambda j, i, meta: (i, j)),
        ),
        out_shape=jax.ShapeDtypeStruct((m, f), BF16),
        compiler_params=_params("parallel", "arbitrary"),
        name="moe_gate_up",
    )(meta, h, w_gate, w_up)


def _down_kernel(a_ref, w_ref, x_ref, o_ref):
    o_ref[...] = x_ref[...] + jnp.dot(a_ref[...], w_ref[...], preferred_element_type=F32)


def down_residual(a, w_down, li, x, tm=512, tn=512):
    n, d = x.shape
    f = a.shape[1]
    return pl.pallas_call(
        _down_kernel,
        grid=(d // tn, n // tm),
        in_specs=[
            pl.BlockSpec((tm, f), lambda j, i: (i, 0)),
            pl.BlockSpec((None, f, tn), lambda j, i: (li, 0, j)),
            pl.BlockSpec((tm, tn), lambda j, i: (i, j)),
        ],
        out_specs=pl.BlockSpec((tm, tn), lambda j, i: (i, j)),
        out_shape=jax.ShapeDtypeStruct((n, d), F32),
        compiler_params=_params("parallel", "parallel"),
        name="swiglu_down_residual",
    )(a, w_down, x)


def _down_grouped_kernel(meta_ref, a_ref, w_ref, o_ref):
    _rows_in_use(meta_ref, o_ref, lambda rows: jnp.dot(
        a_ref[rows, :], w_ref[...].astype(BF16), preferred_element_type=F32))


def down_grouped(meta, a, w_down, li, tm, tn=512):
    m, f = a.shape
    d = w_down.shape[3]
    nb = m // tm
    return pl.pallas_call(
        _down_grouped_kernel,
        grid_spec=pltpu.PrefetchScalarGridSpec(
            num_scalar_prefetch=1,
            grid=(d // tn, nb),
            in_specs=[
                pl.BlockSpec((tm, f), lambda j, i, meta: (jnp.minimum(i, meta[2 * nb] - 1), 0)),
                pl.BlockSpec((None, None, f, tn), lambda j, i, meta: (li, meta[i], 0, j)),
            ],
            out_specs=pl.BlockSpec((tm, tn), lambda j, i, meta: (i, j)),
        ),
        out_shape=jax.ShapeDtypeStruct((m, d), F32),
        compiler_params=_params("parallel", "arbitrary"),
        name="moe_down",
    )(meta, a, w_down)


def _router_kernel(x_ref, g_ref, w_ref, h_ref, idx_ref, gate_ref):
    x = x_ref[...]
    h = x * lax.rsqrt(jnp.mean(x * x, axis=-1, keepdims=True) + EPS) * g_ref[...]
    h_ref[...] = h.astype(h_ref.dtype)
    w = w_ref[...]
    hh, wh = h.astype(BF16), w.astype(BF16)
    hl, wl = (h - hh.astype(F32)).astype(BF16), (w - wh.astype(F32)).astype(BF16)
    logits = jnp.dot(hh, wh, preferred_element_type=F32) + (
        jnp.dot(hh, wl, preferred_element_type=F32) + jnp.dot(hl, wh, preferred_element_type=F32))
    lane = lax.broadcasted_iota(jnp.int32, logits.shape, 1)
    logits = jnp.where(lane < N_EXPERTS, logits, -jnp.inf)
    m1 = jnp.max(logits, axis=-1, keepdims=True)
    i1 = jnp.min(jnp.where(logits == m1, lane, LANES), axis=-1, keepdims=True)
    rest = jnp.where(lane == i1, -jnp.inf, logits)
    m2 = jnp.max(rest, axis=-1, keepdims=True)
    i2 = jnp.min(jnp.where(rest == m2, lane, LANES), axis=-1, keepdims=True)
    e2 = jnp.exp(m2 - m1)
    denom = 1.0 + e2
    idx_ref[...] = jnp.where(lane == 0, i1, jnp.where(lane == 1, i2, 0))
    gate_ref[...] = jnp.where(lane == 0, 1.0 / denom, jnp.where(lane == 1, e2 / denom, 0.0))


def moe_router(x, gain, w_router, tm=256):
    n, d = x.shape
    w = jnp.zeros((d, LANES), F32).at[:, :N_EXPERTS].set(w_router.astype(F32))
    return pl.pallas_call(
        _router_kernel,
        grid=(n // tm,),
        in_specs=[pl.BlockSpec((tm, d), lambda i: (i, 0)),
                  pl.BlockSpec((1, d), lambda i: (0, 0)),
                  pl.BlockSpec((d, LANES), lambda i: (0, 0))],
        out_specs=[pl.BlockSpec((tm, d), lambda i: (i, 0)),
                   pl.BlockSpec((tm, LANES), lambda i: (i, 0)),
                   pl.BlockSpec((tm, LANES), lambda i: (i, 0))],
        out_shape=[jax.ShapeDtypeStruct((n, d), F32),
                   jax.ShapeDtypeStruct((n, LANES), jnp.int32),
                   jax.ShapeDtypeStruct((n, LANES), F32)],
        compiler_params=_params("parallel"),
        name="moe_router",
    )(x, gain.reshape(1, d), w)


def _slot_row(ref, r):
    shift = SUBLANES.bit_length() - 1
    return ref.at[lax.shift_right_logical(r, shift), pl.ds(lax.bitwise_and(r, SUBLANES - 1), 1)]


def _combine_kernel(dest_ref, x_ref, g_ref, ys_ref, o_ref, rows_ref, sems):
    tm = x_ref.shape[0]
    groups = tm // SUBLANES
    half = groups // 2

    def copies(g, sem):
        return [pltpu.make_async_copy(_slot_row(ys_ref, dest_ref[0, (g * SUBLANES + u) * TOP_K + k]),
                                      rows_ref.at[k, g, pl.ds(u, 1)], sem)
                for u in range(SUBLANES) for k in range(TOP_K)]

    def each(lo, hi, sem, action):
        def body(g, c):
            for cp in copies(g, sem):
                action(cp)
            return c
        lax.fori_loop(lo, hi, body, 0)

    parts = ((0, half, sems.at[0]), (half, groups, sems.at[1]))
    for lo, hi, sem in parts:
        each(lo, hi, sem, lambda cp: cp.start())
    for lo, hi, sem in parts:
        each(lo, hi, sem, lambda cp: cp.wait())
        rows = slice(lo * SUBLANES, hi * SUBLANES)
        g = g_ref[rows, :]
        mix = None
        for k in range(TOP_K):
            term = rows_ref[k, lo:hi].reshape((hi - lo) * SUBLANES, -1) * g[:, k:k + 1]
            mix = term if mix is None else mix + term
        o_ref[rows, :] = x_ref[rows, :] + mix


def moe_combine(x, ys, dest, gates, tm=256):
    n, d = x.shape
    cap = ys.shape[0]
    return pl.pallas_call(
        _combine_kernel,
        grid=(n // tm,),
        in_specs=[
            pl.BlockSpec((None, 1, TOP_K * tm), lambda i: (i, 0, 0), memory_space=pltpu.SMEM),
            pl.BlockSpec((tm, d), lambda i: (i, 0)),
            pl.BlockSpec((tm, LANES), lambda i: (i, 0)),
            pl.BlockSpec(memory_space=pl.ANY),
        ],
        out_specs=pl.BlockSpec((tm, d), lambda i: (i, 0)),
        out_shape=jax.ShapeDtypeStruct((n, d), F32),
        scratch_shapes=[pltpu.VMEM((TOP_K, tm // SUBLANES, SUBLANES, d), F32), pltpu.SemaphoreType.DMA((2,))],
        compiler_params=_params("arbitrary"),
        name="moe_combine",
    )(dest.reshape(n // tm, 1, TOP_K * tm), x, gates, ys.reshape(cap // SUBLANES, SUBLANES, d))


def _dispatch_kernel(fill_ref, dest_ref, h_ref, xs_ref, zero_ref, sem):
    i = pl.program_id(0)
    groups_per_tile = h_ref.shape[0]

    slot = functools.partial(_slot_row, xs_ref)

    def run(count, make_copies):
        def start(g, c):
            for cp in make_copies(g):
                cp.start()
            return c

        def wait(g, c):
            for cp in make_copies(g):
                cp.wait()
            return c

        lax.fori_loop(0, count, start, 0)
        lax.fori_loop(0, count, wait, 0)

    @pl.when(i == 0)
    def _():
        zero_ref[...] = jnp.zeros_like(zero_ref)
        runs = fill_ref.shape[0] // 2
        for e in range(runs):
            first = fill_ref[e]
            run(fill_ref[runs + e], lambda a: [pltpu.make_async_copy(zero_ref.at[pl.ds(0, 1)], slot(first + a), sem)])

    def group_copies(g):
        return [pltpu.make_async_copy(h_ref.at[g, pl.ds(u, 1)],
                                      slot(dest_ref[0, (g * SUBLANES + u) * TOP_K + k]), sem)
                for u in range(SUBLANES) for k in range(TOP_K)]

    run(groups_per_tile, group_copies)


def moe_dispatch(h, dest, fill, cap, tm=256):
    n, d = h.shape
    xs = pl.pallas_call(
        _dispatch_kernel,
        grid_spec=pltpu.PrefetchScalarGridSpec(
            num_scalar_prefetch=1,
            grid=(n // tm,),
            in_specs=[
                pl.BlockSpec((None, 1, TOP_K * tm), lambda i, fill: (i, 0, 0), memory_space=pltpu.SMEM),
                pl.BlockSpec((tm // SUBLANES, SUBLANES, d), lambda i, fill: (i, 0, 0)),
            ],
            out_specs=pl.BlockSpec(memory_space=pl.ANY),
            scratch_shapes=[pltpu.VMEM((SUBLANES, d), h.dtype), pltpu.SemaphoreType.DMA(())],
        ),
        out_shape=jax.ShapeDtypeStruct((cap // SUBLANES, SUBLANES, d), h.dtype),
        compiler_params=_params("arbitrary"),
        name="moe_dispatch",
    )(fill, dest.reshape(n // tm, 1, TOP_K * tm), h.reshape(n // SUBLANES, SUBLANES, d))
    return xs.reshape(cap, d)


def moe_block(x, gain, w_router, w_gate, w_up, w_down, li, tm=512):
    n, d = x.shape
    h, idx, gates = moe_router(x, gain, w_router)
    flat_e = idx[:, :TOP_K].reshape(-1)
    na = n * TOP_K
    onehot = (flat_e[None, :] == jnp.arange(N_EXPERTS, dtype=jnp.int32)[:, None]).astype(jnp.int32)
    csum = jnp.cumsum(onehot, axis=1)
    counts = csum[:, -1]
    padded = (counts + tm - 1) // tm * tm
    pad_end = jnp.cumsum(padded)
    pad_start = pad_end - padded
    dest = jnp.sum(onehot * (csum - 1 + pad_start[:, None]), axis=0)
    n_blocks = na // tm + N_EXPERTS
    cap = n_blocks * tm
    fill = jnp.concatenate([pad_start + counts, pad_end[-1:], padded - counts, cap - pad_end[-1:]]).astype(jnp.int32)
    block_e = jnp.minimum(
        jnp.searchsorted(pad_end, jnp.arange(n_blocks, dtype=jnp.int32) * tm, side='right'),
        N_EXPERTS - 1).astype(jnp.int32)
    block_rows = jnp.clip((pad_start + counts)[block_e] - jnp.arange(n_blocks, dtype=jnp.int32) * tm, 0, tm)
    meta = jnp.concatenate([block_e, block_rows, pad_end[-1:] // tm]).astype(jnp.int32)
    xs = moe_dispatch(h, dest, fill, cap)
    act = gate_up_grouped(meta, xs, w_gate, w_up, li, tm)
    ys = down_grouped(meta, act, w_down, li, tm)
    return moe_combine(x, ys, dest, gates)


def _ple_kernel(gain_ref, wg_ref, p_ref, wp_ref, x_ref, next_gain_ref, o_ref, hn_ref):
    x = x_ref[...]
    h = _rms(x, gain_ref[...]).astype(BF16)
    gate = jax.nn.sigmoid(jnp.dot(h, wg_ref[...], preferred_element_type=F32))
    proj = jnp.dot(p_ref[...].astype(BF16), wp_ref[...].astype(BF16), preferred_element_type=F32)
    x = x + proj * gate
    o_ref[...] = x
    hn_ref[...] = _rms(x, next_gain_ref[...]).astype(hn_ref.dtype)


def ple_residual(gain, w_gate, p, w_proj, li, x, next_gain, next_dtype, tm=512):
    n, d = x.shape
    kp = p.shape[2]
    row = lambda w: pl.BlockSpec((tm, w), lambda i: (i, 0))
    vec = pl.BlockSpec((1, d), lambda i: (0, 0))
    return pl.pallas_call(
        _ple_kernel,
        grid=(n // tm,),
        in_specs=[
            vec,
            pl.BlockSpec((None, d, d), lambda i: (li, 0, 0)),
            pl.BlockSpec((None, tm, kp), lambda i: (li, i, 0)),
            pl.BlockSpec((None, kp, d), lambda i: (li, 0, 0)),
            row(d),
            vec,
        ],
        out_specs=[row(d), row(d)],
        out_shape=[jax.ShapeDtypeStruct((n, d), F32), jax.ShapeDtypeStruct((n, d), next_dtype)],
        compiler_params=_params("parallel"),
        name="ple_residual",
    )(gain.reshape(1, d), w_gate, p, w_proj, x, next_gain.reshape(1, d))


def hybrid_mixer_residual(x, h, w_in, li, conv_w, a_log, dt_bias, gdn_norm, sinks, bias, w_out, next_gain,
                          batch, seq):
    n = x.shape[0]
    proj = matmul_cols_outer(h, w_in, li, IN_MAIN_W, tm=512, tn=IN_MAIN_W // 2)
    w_tail = jnp.zeros((D_MODEL, LANES), BF16).at[:, :IN_TAIL_W].set(w_in[li, :, IN_MAIN_W:])
    attn = swa_attention(proj, sinks.astype(F32), bias, batch, seq)
    gates = gdn_gates(h, w_tail, a_log, dt_bias)
    pair = GDN_V_HEADS // GDN_QK_HEADS
    g_rows = gates[:, GDN_V_HEADS:2 * GDN_V_HEADS].T.reshape(GDN_QK_HEADS, pair, n // GDN_TILE, GDN_TILE)
    g_rows = g_rows.transpose(0, 2, 1, 3)
    gdn = gdn_mixer(proj, conv_w, gates, g_rows, gdn_norm, batch, seq)
    return out_proj_residual(attn, gdn, w_out, li, x, next_gain)


def kernel(x, p, w_in, conv_w, a_log, dt_bias, gdn_norm, attn_sinks, rel_bias_table, w_out, norm_mix, norm_ffn, w_dense_gate, w_dense_up, w_dense_down, w_router, w_exp_gate, w_exp_up, w_exp_down, norm_ple, w_ple_gate, w_ple_proj, norm_final):
    batch, seq, d = x.shape
    depth = w_in.shape[0]
    n = batch * seq
    out_dtype = x.dtype
    x = x.reshape(n, d).astype(F32)
    assert WINDOW == SWA_BLOCK
    bias = folded_relative_bias(rel_bias_table)
    p = p.reshape(depth, n, PLE_DIM)
    w_in, w_out, w_ple_gate = w_in.astype(BF16), w_out.astype(BF16), w_ple_gate.astype(BF16)
    w_dense_gate, w_dense_up, w_dense_down = (w.astype(BF16) for w in (w_dense_gate, w_dense_up, w_dense_down))
    h = rmsnorm_rows(x, norm_mix[0], BF16)
    for i in range(depth):
        x, hf = hybrid_mixer_residual(x, h, w_in, i, conv_w[i], a_log[i], dt_bias[i], gdn_norm[i],
                                      attn_sinks[i], bias, w_out, norm_ffn[i], batch, seq)
        if i % 2 == 0:
            act = gate_up(hf, w_dense_gate, w_dense_up, i // 2)
            x = down_residual(act, w_dense_down, i // 2, x)
        else:
            x = moe_block(x, norm_ffn[i], w_router[i // 2], w_exp_gate, w_exp_up, w_exp_down, i // 2)
        last = i == depth - 1
        x, h = ple_residual(norm_ple[i], w_ple_gate, p, w_ple_proj, i, x,
                            norm_final if last else norm_mix[i + 1], out_dtype if last else BF16)
    return h.reshape(batch, seq, d)
```

```python
import functools
import math

import jax
import jax.numpy as jnp
from jax import lax
from jax.experimental import pallas as pl
from jax.experimental.pallas import tpu as pltpu

F32 = jnp.float32
BF16 = jnp.bfloat16

D_MODEL = 2048
PLE_DIM = 256
EPS = 1e-6
NEG_INF = -1e30

SWA_HEADS = 16
SWA_KV_HEADS = 2
SWA_HEAD_DIM = 64
SWA_GROUP = SWA_HEADS // SWA_KV_HEADS
WINDOW = 128
SWA_BLOCK = 128
REL_BUCKETS = 32
REL_MAX_DIST = 128

GDN_QK_HEADS = 4
GDN_V_HEADS = 8
GDN_HEAD_DIM = 128
GDN_CONV = 4
GDN_TILE = 256
GDN_BLOCK = 256

SWA_Q_W = SWA_HEADS * SWA_HEAD_DIM
SWA_KV_W = SWA_KV_HEADS * SWA_HEAD_DIM
GDN_QK_W = GDN_QK_HEADS * GDN_HEAD_DIM
GDN_V_W = GDN_V_HEADS * GDN_HEAD_DIM
GDN_CONV_CH = 2 * GDN_QK_W + GDN_V_W
MIX_WIDTH = SWA_Q_W + GDN_V_W
IN_MAIN_W = SWA_Q_W + 2 * SWA_KV_W + GDN_CONV_CH + GDN_V_W
IN_TAIL_W = 2 * GDN_V_HEADS

D_FF = 7 * D_MODEL // 2
N_EXPERTS = 8
TOP_K = 2

LANES = 128
SUBLANES = 8
VMEM_LIMIT = 56 * 1024 * 1024

COL_K = SWA_Q_W // LANES
COL_V = COL_K + SWA_KV_W // LANES
COL_GDN = COL_V + SWA_KV_W // LANES
COL_Z = COL_GDN + GDN_CONV_CH // LANES


def _params(*sem):
    return pltpu.CompilerParams(dimension_semantics=sem, vmem_limit_bytes=VMEM_LIMIT)


def _rms_kernel(x_ref, g_ref, o_ref):
    x = x_ref[...]
    y = x * lax.rsqrt(jnp.mean(x * x, axis=-1, keepdims=True) + EPS)
    o_ref[...] = (y * g_ref[...]).astype(o_ref.dtype)


def rmsnorm_rows(x, gain, out_dtype, tm=512):
    n, d = x.shape
    return pl.pallas_call(
        _rms_kernel,
        grid=(n // tm,),
        in_specs=[pl.BlockSpec((tm, d), lambda i: (i, 0)), pl.BlockSpec((1, d), lambda i: (0, 0))],
        out_specs=pl.BlockSpec((tm, d), lambda i: (i, 0)),
        out_shape=jax.ShapeDtypeStruct((n, d), out_dtype),
        compiler_params=_params("parallel"),
        name="rmsnorm",
    )(x, gain.reshape(1, d))


def _mm_kernel(a_ref, w_ref, o_ref):
    o_ref[...] = jnp.dot(a_ref[...], w_ref[...], preferred_element_type=F32).astype(o_ref.dtype)


def matmul_cols_outer(a, w, li, n_out, tm, tn, out_dtype=F32):
    m, k = a.shape
    return pl.pallas_call(
        _mm_kernel,
        grid=(n_out // tn, m // tm),
        in_specs=[pl.BlockSpec((tm, k), lambda j, i: (i, 0)),
                  pl.BlockSpec((None, k, tn), lambda j, i: (li, 0, j))],
        out_specs=pl.BlockSpec((tm, tn), lambda j, i: (i, j)),
        out_shape=jax.ShapeDtypeStruct((m, n_out), out_dtype),
        compiler_params=_params("parallel", "parallel"),
        name="in_proj",
    )(a, w)


def _attn_kernel(sink_ref, q_ref, kc_ref, kp_ref, vc_ref, vp_ref, bias_ref, o_ref):
    n = pl.program_id(1)
    kj = lax.broadcasted_iota(jnp.int32, (SWA_BLOCK, SWA_BLOCK), 0)
    qi = lax.broadcasted_iota(jnp.int32, (SWA_BLOCK, SWA_BLOCK), 1)
    in_cur = kj <= qi
    valid = in_cur | (n > 0)
    scale = SWA_HEAD_DIM ** -0.5
    kv_cols = [slice(g * SWA_HEAD_DIM, (g + 1) * SWA_HEAD_DIM) for g in range(SWA_KV_HEADS)]
    kws = [jnp.concatenate([kp_ref[:, c], kc_ref[:, c]], axis=0).astype(BF16) for c in kv_cols]
    vts = [jnp.concatenate([vp_ref[:, c], vc_ref[:, c]], axis=0).T.astype(BF16) for c in kv_cols]
    heads = range(SWA_HEADS)
    qs = [(q_ref[:, h * SWA_HEAD_DIM:(h + 1) * SWA_HEAD_DIM] * scale).astype(BF16) for h in heads]
    boths = [lax.dot_general(kws[h // SWA_GROUP], qs[h], _NT, preferred_element_type=F32) for h in heads]
    logits = [jnp.where(valid, jnp.where(in_cur, b[SWA_BLOCK:], b[:SWA_BLOCK]) + bias_ref[h], NEG_INF)
              for h, b in zip(heads, boths)]
    ms = [jnp.maximum(jnp.max(l, axis=0, keepdims=True), sink_ref[h]) for h, l in zip(heads, logits)]
    es = [jnp.exp(l - m) for l, m in zip(logits, ms)]
    denoms = [jnp.sum(e, axis=0, keepdims=True) + jnp.exp(sink_ref[h] - m) for h, e, m in zip(heads, es, ms)]
    unfolded = [jnp.concatenate([jnp.where(in_cur, 0.0, e), jnp.where(in_cur, e, 0.0)], axis=0).astype(BF16)
                for e in es]
    outs = [jnp.dot(vts[h // SWA_GROUP], u, preferred_element_type=F32) * (1.0 / d)
            for h, u, d in zip(heads, unfolded, denoms)]
    for h0 in range(0, SWA_HEADS, 2):
        both_heads = jnp.concatenate(outs[h0:h0 + 2], axis=0).T
        o_ref[:, h0 * SWA_HEAD_DIM:(h0 + 2) * SWA_HEAD_DIM] = both_heads.astype(o_ref.dtype)


def folded_relative_bias(rel_table):
    kj = jnp.arange(SWA_BLOCK, dtype=jnp.int32)[:, None]
    qi = jnp.arange(SWA_BLOCK, dtype=jnp.int32)[None, :]
    dist = jnp.where(kj <= qi, qi - kj, qi + SWA_BLOCK - kj)
    max_exact = REL_BUCKETS // 2
    d = jnp.maximum(dist, 0)
    log_ratio = jnp.log(jnp.maximum(d, 1).astype(F32) / max_exact) / math.log(REL_MAX_DIST / max_exact)
    large = jnp.minimum(max_exact + (log_ratio * (REL_BUCKETS - max_exact)).astype(jnp.int32), REL_BUCKETS - 1)
    bucket = jnp.where(d < max_exact, d, large)
    onehot = (bucket[None] == jnp.arange(REL_BUCKETS, dtype=jnp.int32)[:, None, None]).astype(F32)
    return jnp.einsum('bh,bjq->hjq', rel_table.astype(F32), onehot, precision=lax.Precision.HIGHEST)


def swa_attention(proj, sinks, bias, batch, seq):
    n = proj.shape[0]
    nb = seq // SWA_BLOCK
    cur = lambda b, i, s: b * nb + i
    prev = lambda b, i, s: jnp.maximum(b * nb + i - 1, 0)
    blk = (SWA_BLOCK, LANES)
    return pl.pallas_call(
        _attn_kernel,
        grid_spec=pltpu.PrefetchScalarGridSpec(
            num_scalar_prefetch=1,
            grid=(batch, nb),
            in_specs=[
                pl.BlockSpec((SWA_BLOCK, SWA_Q_W), lambda b, i, s: (cur(b, i, s), 0)),
                pl.BlockSpec(blk, lambda b, i, s: (cur(b, i, s), COL_K)),
                pl.BlockSpec(blk, lambda b, i, s: (prev(b, i, s), COL_K)),
                pl.BlockSpec(blk, lambda b, i, s: (cur(b, i, s), COL_V)),
                pl.BlockSpec(blk, lambda b, i, s: (prev(b, i, s), COL_V)),
                pl.BlockSpec((SWA_HEADS, SWA_BLOCK, SWA_BLOCK), lambda b, i, s: (0, 0, 0)),
            ],
            out_specs=pl.BlockSpec((SWA_BLOCK, SWA_Q_W), lambda b, i, s: (cur(b, i, s), 0)),
        ),
        out_shape=jax.ShapeDtypeStruct((n, SWA_Q_W), BF16),
        compiler_params=_params("parallel", "parallel"),
        name="swa_attention",
    )(sinks, proj, proj, proj, proj, proj, bias)


def _causal_conv_silu(cur, prev, w):
    tm = cur.shape[0]
    xp = jnp.concatenate([prev, cur], axis=0)
    first = SUBLANES - (GDN_CONV - 1)
    acc = xp[first:first + tm] * w[0:1]
    for k in range(1, GDN_CONV):
        acc = acc + xp[first + k:first + k + tm] * w[k:k + 1]
    return acc * jax.nn.sigmoid(acc)


def _l2norm(y):
    return y * lax.rsqrt(jnp.sum(y * y, axis=-1, keepdims=True) + EPS)


def _gdn_gate_kernel(h_ref, w_ref, alog_ref, dtb_ref, o_ref):
    tm = h_ref.shape[0]
    t = jnp.dot(h_ref[...], w_ref[...], preferred_element_type=F32)
    lane = lax.broadcasted_iota(jnp.int32, t.shape, 1)
    beta = jax.nn.sigmoid(t)
    g = -jnp.exp(alog_ref[...]) * jax.nn.softplus(t + dtb_ref[...])
    g = jnp.where((lane >= GDN_V_HEADS) & (lane < 2 * GDN_V_HEADS), g, 0.0)
    blk = GDN_BLOCK
    r = lax.broadcasted_iota(jnp.int32, (blk, blk), 0)
    s = lax.broadcasted_iota(jnp.int32, (blk, blk), 1)
    tri = jnp.where(s <= r, 1.0, 0.0).astype(BF16)
    sums = []
    for b in range(tm // blk):
        gb = g[b * blk:(b + 1) * blk]
        hi = gb.astype(BF16)
        rest = gb - hi.astype(F32)
        mid = rest.astype(BF16)
        lo = (rest - mid.astype(F32)).astype(BF16)
        sums.append(jnp.dot(tri, hi, preferred_element_type=F32)
                    + (jnp.dot(tri, mid, preferred_element_type=F32) + jnp.dot(tri, lo, preferred_element_type=F32)))
    gcum = jnp.concatenate(sums, axis=0)
    o_ref[...] = jnp.where(lane < GDN_V_HEADS, beta, gcum)


def gdn_gates(h, w_tail, a_log, dt_bias, tm=512):
    n, k = h.shape
    pad = lambda v: jnp.zeros((1, LANES), F32).at[0, GDN_V_HEADS:2 * GDN_V_HEADS].set(v.astype(F32))
    return pl.pallas_call(
        _gdn_gate_kernel,
        grid=(n // tm,),
        in_specs=[pl.BlockSpec((tm, k), lambda i: (i, 0)),
                  pl.BlockSpec((k, LANES), lambda i: (0, 0)),
                  pl.BlockSpec((1, LANES), lambda i: (0, 0)),
                  pl.BlockSpec((1, LANES), lambda i: (0, 0))],
        out_specs=pl.BlockSpec((tm, LANES), lambda i: (i, 0)),
        out_shape=jax.ShapeDtypeStruct((n, LANES), F32),
        compiler_params=_params("parallel"),
        name="gdn_gates",
    )(h, w_tail, pad(a_log), pad(dt_bias))


def _bdot(a, b, dims=(((1,), (0,)), ((), ()))):
    return lax.dot_general(a.astype(BF16), b.astype(BF16), dims, preferred_element_type=F32)


_NT = (((1,), (1,)), ((), ()))
_TN = (((0,), (0,)), ((), ()))


def _unit_lower_inverses(lows, block):
    r_dim = lows[0].shape[0]
    r = lax.broadcasted_iota(jnp.int32, (r_dim, r_dim), 0)
    s = lax.broadcasted_iota(jnp.int32, (r_dim, r_dim), 1)
    level = 31 - lax.clz(r ^ s)
    size = 1
    step = 0
    xs = None
    while size < block:
        sel = level == step
        step += 1
        offs = [jnp.where(sel, low, 0.0) for low in lows]
        if size == 1:
            eye = jnp.where(r == s, 1.0, 0.0).astype(F32)
            xs = [eye - off for off in offs]
        else:
            xo = [_bdot(x, off) for x, off in zip(xs, offs)]
            xs = [x - _bdot(y, x) for x, y in zip(xs, xo)]
        size *= 2
    return xs


def _gdn_kernel(qin_ref, qprev_ref, kin_ref, kprev_ref, vin_ref, vprev_ref, wq_ref, wk_ref, wv_ref,
                z_ref, gb_ref, grow_ref, norm_ref, o_ref,
                state_ref, q_ref, k_ref, v_ref, w_s, u_s, qd_s, kd_s, qk_s):
    hp = pl.program_id(1)
    t = pl.program_id(2)
    tb = qin_ref.shape[0]
    rt = GDN_TILE
    blk = GDN_BLOCK
    pair = GDN_V_HEADS // GDN_QK_HEADS
    d = GDN_HEAD_DIM

    @pl.when(t == 0)
    def _():
        state_ref[...] = jnp.zeros_like(state_ref)

    halo = lambda ref: jnp.where(t == 0, 0.0, ref[...])
    q_ref[...] = _l2norm(_causal_conv_silu(qin_ref[...], halo(qprev_ref), wq_ref[...]))
    k_ref[...] = _l2norm(_causal_conv_silu(kin_ref[...], halo(kprev_ref), wk_ref[...]))
    v_all = _causal_conv_silu(vin_ref[...], halo(vprev_ref), wv_ref[...])
    for hh in range(pair):
        v_ref[hh] = v_all[:, hh * d:(hh + 1) * d]

    lane = lax.broadcasted_iota(jnp.int32, (tb, LANES), 1)
    gb = gb_ref[...]
    r = lax.broadcasted_iota(jnp.int32, (rt, rt), 0)
    s = lax.broadcasted_iota(jnp.int32, (rt, rt), 1)
    same = (r // blk) == (s // blk)
    causal = (s <= r) & same
    strict = (s < r) & same

    g_heads = []
    beta_heads = []
    for hh in range(pair):
        h = pair * hp + hh
        beta_heads.append(jnp.sum(jnp.where(lane == h, gb, 0.0), axis=-1, keepdims=True))
        g_heads.append(jnp.sum(jnp.where(lane == h + GDN_V_HEADS, gb, 0.0), axis=-1, keepdims=True))

    probs = [(hh, ti, slice(ti * rt, (ti + 1) * rt)) for hh in range(pair) for ti in range(tb // rt)]
    qs = {ti: q_ref[ti * rt:(ti + 1) * rt, :] * (d ** -0.5) for ti in range(tb // rt)}
    ks = {ti: k_ref[ti * rt:(ti + 1) * rt, :] for ti in range(tb // rt)}
    betas = [beta_heads[hh][rows] for hh, ti, rows in probs]
    gs = [g_heads[hh][rows] for hh, ti, rows in probs]
    egs = [jnp.exp(g) for g in gs]
    kbs = [ks[ti] * beta for (hh, ti, rows), beta in zip(probs, betas)]
    kqs = [_bdot(jnp.concatenate([kb, qs[ti]], axis=0), ks[ti], _NT) for (hh, ti, rows), kb in zip(probs, kbs)]
    decays = [jnp.where(causal, jnp.exp(jnp.where(causal, g - grow_ref[ti, hh:hh + 1, :], 0.0)), 0.0)
              for (hh, ti, rows), g in zip(probs, gs)]
    lows = [jnp.where(strict, kq[:rt] * decay, 0.0) for kq, decay in zip(kqs, decays)]
    for (hh, ti, rows), kq, decay, g, eg in zip(probs, kqs, decays, gs, egs):
        qk_s[hh, rows, :] = (kq[rt:] * decay).astype(BF16)
        g_last = jnp.concatenate(
            [jnp.broadcast_to(g[(c + 1) * blk - 1:(c + 1) * blk], (blk, 1)) for c in range(rt // blk)], axis=0)
        qd_s[hh, rows, :] = (qs[ti] * eg).astype(BF16)
        kd_s[hh, rows, :] = (ks[ti] * jnp.exp(g_last - g)).astype(BF16)
    rhss = [jnp.concatenate([v_ref[hh, rows, :] * beta, kb * eg], axis=1).astype(BF16)
            for (hh, ti, rows), beta, kb, eg in zip(probs, betas, kbs, egs)]

    tinvs = _unit_lower_inverses(lows, blk)
    uws = [_bdot(tinv, rhs) for tinv, rhs in zip(tinvs, rhss)]
    for (hh, ti, rows), uw in zip(probs, uws):
        u_s[hh, rows, :] = uw[:, :d]
        w_s[hh, rows, :] = uw[:, d:].astype(BF16)

    heads = range(pair)
    for c in range(tb // blk):
        rows = slice(c * blk, (c + 1) * blk)
        off = (c * blk) % rt
        states = [state_ref[hh] for hh in heads]
        ws = [_bdot(jnp.concatenate([w_s[hh, rows, :], qd_s[hh, rows, :]], axis=0), states[hh]) for hh in heads]
        v_new = [u_s[hh, rows, :] - ws[hh][:blk] for hh in heads]
        outs = [ws[hh][blk:] + _bdot(qk_s[hh, rows, off:off + blk], v_new[hh]) for hh in heads]
        for hh in heads:
            g_end = g_heads[hh][(c + 1) * blk - 1:(c + 1) * blk]
            state_ref[hh] = states[hh] * jnp.exp(g_end) + _bdot(kd_s[hh, rows, :], v_new[hh], _TN)
        for hh in heads:
            o = outs[hh]
            o = o * lax.rsqrt(jnp.mean(o * o, axis=-1, keepdims=True) + EPS) * norm_ref[...]
            zc = z_ref[rows, hh * d:(hh + 1) * d]
            o_ref[rows, hh * d:(hh + 1) * d] = (o * (zc * jax.nn.sigmoid(zc))).astype(o_ref.dtype)


def gdn_mixer(proj, conv_w, gates, g_rows, gdn_norm, batch, seq, tb=512):
    n = proj.shape[0]
    nt = seq // tb
    pair = GDN_V_HEADS // GDN_QK_HEADS
    d = GDN_HEAD_DIM
    row = lambda b, hp, t: b * nt + t
    before = lambda b, hp, t: jnp.maximum(row(b, hp, t) * (tb // SUBLANES) - 1, 0)
    wide = (tb, pair * d)
    q_col = COL_GDN
    k_col = COL_GDN + GDN_QK_HEADS
    v_col = (COL_GDN + 2 * GDN_QK_HEADS) // pair
    z_col = COL_Z // pair
    return pl.pallas_call(
        _gdn_kernel,
        grid=(batch, GDN_QK_HEADS, nt),
        in_specs=[
            pl.BlockSpec((tb, d), lambda b, hp, t: (row(b, hp, t), q_col + hp)),
            pl.BlockSpec((SUBLANES, d), lambda b, hp, t: (before(b, hp, t), q_col + hp)),
            pl.BlockSpec((tb, d), lambda b, hp, t: (row(b, hp, t), k_col + hp)),
            pl.BlockSpec((SUBLANES, d), lambda b, hp, t: (before(b, hp, t), k_col + hp)),
            pl.BlockSpec(wide, lambda b, hp, t: (row(b, hp, t), v_col + hp)),
            pl.BlockSpec((SUBLANES, pair * d), lambda b, hp, t: (before(b, hp, t), v_col + hp)),
            pl.BlockSpec((GDN_CONV, d), lambda b, hp, t: (0, hp)),
            pl.BlockSpec((GDN_CONV, d), lambda b, hp, t: (0, GDN_QK_HEADS + hp)),
            pl.BlockSpec((GDN_CONV, pair * d), lambda b, hp, t: (0, 2 * GDN_QK_HEADS // pair + hp)),
            pl.BlockSpec(wide, lambda b, hp, t: (row(b, hp, t), z_col + hp)),
            pl.BlockSpec((tb, LANES), lambda b, hp, t: (row(b, hp, t), 0)),
            pl.BlockSpec((None, tb // GDN_TILE, pair, GDN_TILE), lambda b, hp, t: (hp, row(b, hp, t), 0, 0)),
            pl.BlockSpec((1, LANES), lambda b, hp, t: (0, 0)),
        ],
        out_specs=pl.BlockSpec(wide, lambda b, hp, t: (row(b, hp, t), hp)),
        out_shape=jax.ShapeDtypeStruct((n, GDN_V_W), BF16),
        scratch_shapes=[
            pltpu.VMEM((pair, d, d), F32),
            pltpu.VMEM((tb, d), F32),
            pltpu.VMEM((tb, d), F32),
            pltpu.VMEM((pair, tb, d), F32),
            pltpu.VMEM((pair, tb, d), BF16),
            pltpu.VMEM((pair, tb, d), F32),
            pltpu.VMEM((pair, tb, d), BF16),
            pltpu.VMEM((pair, tb, d), BF16),
            pltpu.VMEM((pair, tb, GDN_TILE), BF16),
        ],
        compiler_params=_params("parallel", "parallel", "arbitrary"),
        name="gated_delta_rule",
    )(proj, proj, proj, proj, proj, proj, conv_w, conv_w, conv_w, proj, gates, g_rows,
      gdn_norm.reshape(1, LANES).astype(F32))


def _rms(x, gain):
    return x * lax.rsqrt(jnp.mean(x * x, axis=-1, keepdims=True) + EPS) * gain


def _out_proj_kernel(a_ref, b_ref, w_ref, x_ref, g_ref, o_ref, h_ref):
    ka = a_ref.shape[1]
    acc = jnp.dot(a_ref[...], w_ref[:ka, :], preferred_element_type=F32)
    acc += jnp.dot(b_ref[...], w_ref[ka:, :], preferred_element_type=F32)
    x = x_ref[...] + acc
    o_ref[...] = x
    h_ref[...] = _rms(x, g_ref[...]).astype(h_ref.dtype)


def out_proj_residual(attn, gdn, w_out, li, x, next_gain, tm=512):
    n, d = x.shape
    ka, kb = attn.shape[1], gdn.shape[1]
    row = lambda w: pl.BlockSpec((tm, w), lambda i: (i, 0))
    return pl.pallas_call(
        _out_proj_kernel,
        grid=(n // tm,),
        in_specs=[row(ka), row(kb),
                  pl.BlockSpec((None, ka + kb, d), lambda i: (li, 0, 0)),
                  row(d),
                  pl.BlockSpec((1, d), lambda i: (0, 0))],
        out_specs=[row(d), row(d)],
        out_shape=[jax.ShapeDtypeStruct((n, d), F32), jax.ShapeDtypeStruct((n, d), BF16)],
        compiler_params=_params("parallel"),
        name="out_proj_residual",
    )(attn, gdn, w_out, x, next_gain.reshape(1, d))


def _swiglu_tile(h, wg, wu):
    g = jnp.dot(h, wg, preferred_element_type=F32)
    u = jnp.dot(h, wu, preferred_element_type=F32)
    return g * jax.nn.sigmoid(g) * u


def _gate_up_kernel(h_ref, wg_ref, wu_ref, o_ref):
    o_ref[...] = _swiglu_tile(h_ref[...], wg_ref[...].astype(BF16), wu_ref[...].astype(BF16)).astype(o_ref.dtype)


def gate_up(h, w_gate, w_up, li, tm=512, tn=1024):
    m, k = h.shape
    f = w_gate.shape[2]
    return pl.pallas_call(
        _gate_up_kernel,
        grid=(f // tn, m // tm),
        in_specs=[
            pl.BlockSpec((tm, k), lambda j, i: (i, 0)),
            pl.BlockSpec((None, k, tn), lambda j, i: (li, 0, j)),
            pl.BlockSpec((None, k, tn), lambda j, i: (li, 0, j)),
        ],
        out_specs=pl.BlockSpec((tm, tn), lambda j, i: (i, j)),
        out_shape=jax.ShapeDtypeStruct((m, f), BF16),
        compiler_params=_params("parallel", "parallel"),
        name="swiglu_gate_up",
    )(h, w_gate, w_up)


def _rows_in_use(meta_ref, o_ref, compute):
    i = pl.program_id(1)
    valid = meta_ref[pl.num_programs(1) + i]

    @pl.when(valid > 0)
    def _():
        o_ref[...] = compute(slice(None))

    @pl.when(valid == 0)
    def _():
        o_ref[...] = jnp.zeros_like(o_ref)


def _gate_up_grouped_kernel(meta_ref, h_ref, wg_ref, wu_ref, o_ref):
    _rows_in_use(meta_ref, o_ref, lambda rows: _swiglu_tile(
        h_ref[rows, :].astype(BF16), wg_ref[...].astype(BF16), wu_ref[...].astype(BF16)).astype(o_ref.dtype))


def gate_up_grouped(meta, h, w_gate, w_up, li, tm, tn=1024):
    m, k = h.shape
    f = w_gate.shape[3]
    nb = m // tm
    row = lambda j, i, meta: (jnp.minimum(i, meta[2 * nb] - 1), 0)
    wmap = lambda j, i, meta: (li, meta[i], 0, j)
    return pl.pallas_call(
        _gate_up_grouped_kernel,
        grid_spec=pltpu.PrefetchScalarGridSpec(
            num_scalar_prefetch=1,
            grid=(f // tn, nb),
            in_specs=[
                pl.BlockSpec((tm, k), row),
                pl.BlockSpec((None, None, k, tn), wmap),
                pl.BlockSpec((None, None, k, tn), wmap),
            ],
            out_specs=pl.BlockSpec((tm, tn), lambda j, i, meta: (i, j)),
        ),
        out_shape=jax.ShapeDtypeStruct((m, f), BF16),
        compiler_params=_params("parallel", "arbitrary"),
        name="moe_gate_up",
    )(meta, h, w_gate, w_up)


def _down_kernel(a_ref, w_ref, x_ref, o_ref):
    o_ref[...] = x_ref[...] + jnp.dot(a_ref[...], w_ref[...].astype(BF16), preferred_element_type=F32)


def down_residual(a, w_down, li, x, tm=512, tn=512):
    n, d = x.shape
    f = a.shape[1]
    return pl.pallas_call(
        _down_kernel,
        grid=(d // tn, n // tm),
        in_specs=[
            pl.BlockSpec((tm, f), lambda j, i: (i, 0)),
            pl.BlockSpec((None, f, tn), lambda j, i: (li, 0, j)),
            pl.BlockSpec((tm, tn), lambda j, i: (i, j)),
        ],
        out_specs=pl.BlockSpec((tm, tn), lambda j, i: (i, j)),
        out_shape=jax.ShapeDtypeStruct((n, d), F32),
        compiler_params=_params("parallel", "parallel"),
        name="swiglu_down_residual",
    )(a, w_down, x)


def _down_grouped_kernel(meta_ref, a_ref, w_ref, o_ref):
    _rows_in_use(meta_ref, o_ref, lambda rows: jnp.dot(
        a_ref[rows, :], w_ref[...].astype(BF16), preferred_element_type=F32))


def down_grouped(meta, a, w_down, li, tm, tn=512):
    m, f = a.shape
    d = w_down.shape[3]
    nb = m // tm
    return pl.pallas_call(
        _down_grouped_kernel,
        grid_spec=pltpu.PrefetchScalarGridSpec(
            num_scalar_prefetch=1,
            grid=(d // tn, nb),
            in_specs=[
                pl.BlockSpec((tm, f), lambda j, i, meta: (jnp.minimum(i, meta[2 * nb] - 1), 0)),
                pl.BlockSpec((None, None, f, tn), lambda j, i, meta: (li, meta[i], 0, j)),
            ],
            out_specs=pl.BlockSpec((tm, tn), lambda j, i, meta: (i, j)),
        ),
        out_shape=jax.ShapeDtypeStruct((m, d), F32),
        compiler_params=_params("parallel", "arbitrary"),
        name="moe_down",
    )(meta, a, w_down)


def _router_kernel(x_ref, g_ref, w_ref, h_ref, idx_ref, gate_ref):
    x = x_ref[...]
    h = x * lax.rsqrt(jnp.mean(x * x, axis=-1, keepdims=True) + EPS) * g_ref[...]
    h_ref[...] = h.astype(h_ref.dtype)
    w = w_ref[...]
    hh, wh = h.astype(BF16), w.astype(BF16)
    hl, wl = (h - hh.astype(F32)).astype(BF16), (w - wh.astype(F32)).astype(BF16)
    logits = jnp.dot(hh, wh, preferred_element_type=F32) + (
        jnp.dot(hh, wl, preferred_element_type=F32) + jnp.dot(hl, wh, preferred_element_type=F32))
    lane = lax.broadcasted_iota(jnp.int32, logits.shape, 1)
    logits = jnp.where(lane < N_EXPERTS, logits, -jnp.inf)
    m1 = jnp.max(logits, axis=-1, keepdims=True)
    i1 = jnp.min(jnp.where(logits == m1, lane, LANES), axis=-1, keepdims=True)
    rest = jnp.where(lane == i1, -jnp.inf, logits)
    m2 = jnp.max(rest, axis=-1, keepdims=True)
    i2 = jnp.min(jnp.where(rest == m2, lane, LANES), axis=-1, keepdims=True)
    e2 = jnp.exp(m2 - m1)
    denom = 1.0 + e2
    idx_ref[...] = jnp.where(lane == 0, i1, jnp.where(lane == 1, i2, 0))
    gate_ref[...] = jnp.where(lane == 0, 1.0 / denom, jnp.where(lane == 1, e2 / denom, 0.0))


def moe_router(x, gain, w_router, tm=256):
    n, d = x.shape
    w = jnp.zeros((d, LANES), F32).at[:, :N_EXPERTS].set(w_router.astype(F32))
    return pl.pallas_call(
        _router_kernel,
        grid=(n // tm,),
        in_specs=[pl.BlockSpec((tm, d), lambda i: (i, 0)),
                  pl.BlockSpec((1, d), lambda i: (0, 0)),
                  pl.BlockSpec((d, LANES), lambda i: (0, 0))],
        out_specs=[pl.BlockSpec((tm, d), lambda i: (i, 0)),
                   pl.BlockSpec((tm, LANES), lambda i: (i, 0)),
                   pl.BlockSpec((tm, LANES), lambda i: (i, 0))],
        out_shape=[jax.ShapeDtypeStruct((n, d), F32),
                   jax.ShapeDtypeStruct((n, LANES), jnp.int32),
                   jax.ShapeDtypeStruct((n, LANES), F32)],
        compiler_params=_params("parallel"),
        name="moe_router",
    )(x, gain.reshape(1, d), w)


def _slot_row(ref, r):
    shift = SUBLANES.bit_length() - 1
    return ref.at[lax.shift_right_logical(r, shift), pl.ds(lax.bitwise_and(r, SUBLANES - 1), 1)]


def _combine_kernel(dest_ref, x_ref, g_ref, ys_ref, o_ref, rows_ref, sems):
    tm = x_ref.shape[0]
    groups = tm // SUBLANES
    half = groups // 2

    def copies(g, sem):
        return [pltpu.make_async_copy(_slot_row(ys_ref, dest_ref[0, (g * SUBLANES + u) * TOP_K + k]),
                                      rows_ref.at[k, g, pl.ds(u, 1)], sem)
                for u in range(SUBLANES) for k in range(TOP_K)]

    def each(lo, hi, sem, action):
        def body(g, c):
            for cp in copies(g, sem):
                action(cp)
            return c
        lax.fori_loop(lo, hi, body, 0)

    parts = ((0, half, sems.at[0]), (half, groups, sems.at[1]))
    for lo, hi, sem in parts:
        each(lo, hi, sem, lambda cp: cp.start())
    for lo, hi, sem in parts:
        each(lo, hi, sem, lambda cp: cp.wait())
        rows = slice(lo * SUBLANES, hi * SUBLANES)
        g = g_ref[rows, :]
        mix = None
        for k in range(TOP_K):
            term = rows_ref[k, lo:hi].reshape((hi - lo) * SUBLANES, -1) * g[:, k:k + 1]
            mix = term if mix is None else mix + term
        o_ref[rows, :] = x_ref[rows, :] + mix


def moe_combine(x, ys, dest, gates, tm=256):
    n, d = x.shape
    cap = ys.shape[0]
    return pl.pallas_call(
        _combine_kernel,
        grid=(n // tm,),
        in_specs=[
            pl.BlockSpec((None, 1, TOP_K * tm), lambda i: (i, 0, 0), memory_space=pltpu.SMEM),
            pl.BlockSpec((tm, d), lambda i: (i, 0)),
            pl.BlockSpec((tm, LANES), lambda i: (i, 0)),
            pl.BlockSpec(memory_space=pl.ANY),
        ],
        out_specs=pl.BlockSpec((tm, d), lambda i: (i, 0)),
        out_shape=jax.ShapeDtypeStruct((n, d), F32),
        scratch_shapes=[pltpu.VMEM((TOP_K, tm // SUBLANES, SUBLANES, d), F32), pltpu.SemaphoreType.DMA((2,))],
        compiler_params=_params("arbitrary"),
        name="moe_combine",
    )(dest.reshape(n // tm, 1, TOP_K * tm), x, gates, ys.reshape(cap // SUBLANES, SUBLANES, d))


def _dispatch_kernel(fill_ref, dest_ref, h_ref, xs_ref, zero_ref, sem):
    i = pl.program_id(0)
    groups_per_tile = h_ref.shape[0]

    slot = functools.partial(_slot_row, xs_ref)

    def run(count, make_copies):
        def start(g, c):
            for cp in make_copies(g):
                cp.start()
            return c

        def wait(g, c):
            for cp in make_copies(g):
                cp.wait()
            return c

        lax.fori_loop(0, count, start, 0)
        lax.fori_loop(0, count, wait, 0)

    @pl.when(i == 0)
    def _():
        zero_ref[...] = jnp.zeros_like(zero_ref)
        runs = fill_ref.shape[0] // 2
        for e in range(runs):
            first = fill_ref[e]
            run(fill_ref[runs + e], lambda a: [pltpu.make_async_copy(zero_ref.at[pl.ds(0, 1)], slot(first + a), sem)])

    def group_copies(g):
        return [pltpu.make_async_copy(h_ref.at[g, pl.ds(u, 1)],
                                      slot(dest_ref[0, (g * SUBLANES + u) * TOP_K + k]), sem)
                for u in range(SUBLANES) for k in range(TOP_K)]

    run(groups_per_tile, group_copies)


def moe_dispatch(h, dest, fill, cap, tm=256):
    n, d = h.shape
    xs = pl.pallas_call(
        _dispatch_kernel,
        grid_spec=pltpu.PrefetchScalarGridSpec(
            num_scalar_prefetch=1,
            grid=(n // tm,),
            in_specs=[
                pl.BlockSpec((None, 1, TOP_K * tm), lambda i, fill: (i, 0, 0), memory_space=pltpu.SMEM),
                pl.BlockSpec((tm // SUBLANES, SUBLANES, d), lambda i, fill: (i, 0, 0)),
            ],
            out_specs=pl.BlockSpec(memory_space=pl.ANY),
            scratch_shapes=[pltpu.VMEM((SUBLANES, d), h.dtype), pltpu.SemaphoreType.DMA(())],
        ),
        out_shape=jax.ShapeDtypeStruct((cap // SUBLANES, SUBLANES, d), h.dtype),
        compiler_params=_params("arbitrary"),
        name="moe_dispatch",
    )(fill, dest.reshape(n // tm, 1, TOP_K * tm), h.reshape(n // SUBLANES, SUBLANES, d))
    return xs.reshape(cap, d)


def moe_block(x, gain, w_router, w_gate, w_up, w_down, li, tm=512):
    n, d = x.shape
    h, idx, gates = moe_router(x, gain, w_router)
    flat_e = idx[:, :TOP_K].reshape(-1)
    na = n * TOP_K
    onehot = (flat_e[None, :] == jnp.arange(N_EXPERTS, dtype=jnp.int32)[:, None]).astype(jnp.int32)
    csum = jnp.cumsum(onehot, axis=1)
    counts = csum[:, -1]
    padded = (counts + tm - 1) // tm * tm
    pad_end = jnp.cumsum(padded)
    pad_start = pad_end - padded
    dest = jnp.sum(onehot * (csum - 1 + pad_start[:, None]), axis=0)
    n_blocks = na // tm + N_EXPERTS
    cap = n_blocks * tm
    fill = jnp.concatenate([pad_start + counts, pad_end[-1:], padded - counts, cap - pad_end[-1:]]).astype(jnp.int32)
    block_e = jnp.minimum(
        jnp.searchsorted(pad_end, jnp.arange(n_blocks, dtype=jnp.int32) * tm, side='right'),
        N_EXPERTS - 1).astype(jnp.int32)
    block_rows = jnp.clip((pad_start + counts)[block_e] - jnp.arange(n_blocks, dtype=jnp.int32) * tm, 0, tm)
    meta = jnp.concatenate([block_e, block_rows, pad_end[-1:] // tm]).astype(jnp.int32)
    xs = moe_dispatch(h, dest, fill, cap)
    act = gate_up_grouped(meta, xs, w_gate, w_up, li, tm)
    ys = down_grouped(meta, act, w_down, li, tm)
    return moe_combine(x, ys, dest, gates)


def _ple_kernel(gain_ref, wg_ref, p_ref, wp_ref, x_ref, next_gain_ref, o_ref, hn_ref):
    x = x_ref[...]
    h = _rms(x, gain_ref[...]).astype(BF16)
    gate = jax.nn.sigmoid(jnp.dot(h, wg_ref[...], preferred_element_type=F32))
    proj = jnp.dot(p_ref[...].astype(BF16), wp_ref[...].astype(BF16), preferred_element_type=F32)
    x = x + proj * gate
    o_ref[...] = x
    hn_ref[...] = _rms(x, next_gain_ref[...]).astype(hn_ref.dtype)


def ple_residual(gain, w_gate, p, w_proj, li, x, next_gain, next_dtype, tm=512):
    n, d = x.shape
    kp = p.shape[2]
    row = lambda w: pl.BlockSpec((tm, w), lambda i: (i, 0))
    vec = pl.BlockSpec((1, d), lambda i: (0, 0))
    return pl.pallas_call(
        _ple_kernel,
        grid=(n // tm,),
        in_specs=[
            vec,
            pl.BlockSpec((None, d, d), lambda i: (li, 0, 0)),
            pl.BlockSpec((None, tm, kp), lambda i: (li, i, 0)),
            pl.BlockSpec((None, kp, d), lambda i: (li, 0, 0)),
            row(d),
            vec,
        ],
        out_specs=[row(d), row(d)],
        out_shape=[jax.ShapeDtypeStruct((n, d), F32), jax.ShapeDtypeStruct((n, d), next_dtype)],
        compiler_params=_params("parallel"),
        name="ple_residual",
    )(gain.reshape(1, d), w_gate, p, w_proj, x, next_gain.reshape(1, d))


def hybrid_mixer_residual(x, h, w_in, li, conv_w, a_log, dt_bias, gdn_norm, sinks, bias, w_out, next_gain,
                          batch, seq):
    n = x.shape[0]
    proj = matmul_cols_outer(h, w_in, li, IN_MAIN_W, tm=512, tn=IN_MAIN_W // 2)
    w_tail = jnp.zeros((D_MODEL, LANES), BF16).at[:, :IN_TAIL_W].set(w_in[li, :, IN_MAIN_W:])
    attn = swa_attention(proj, sinks.astype(F32), bias, batch, seq)
    gates = gdn_gates(h, w_tail, a_log, dt_bias)
    pair = GDN_V_HEADS // GDN_QK_HEADS
    g_rows = gates[:, GDN_V_HEADS:2 * GDN_V_HEADS].T.reshape(GDN_QK_HEADS, pair, n // GDN_TILE, GDN_TILE)
    g_rows = g_rows.transpose(0, 2, 1, 3)
    gdn = gdn_mixer(proj, conv_w, gates, g_rows, gdn_norm, batch, seq)
    return out_proj_residual(attn, gdn, w_out, li, x, next_gain)


def kernel(x, p, w_in, conv_w, a_log, dt_bias, gdn_norm, attn_sinks, rel_bias_table, w_out, norm_mix, norm_ffn, w_dense_gate, w_dense_up, w_dense_down, w_router, w_exp_gate, w_exp_up, w_exp_down, norm_ple, w_ple_gate, w_ple_proj, norm_final):
    batch, seq, d = x.shape
    depth = w_in.shape[0]
    n = batch * seq
    out_dtype = x.dtype
    x = x.reshape(n, d).astype(F32)
    assert WINDOW == SWA_BLOCK
    bias = folded_relative_bias(rel_bias_table)
    p = p.reshape(depth, n, PLE_DIM)
    w_in, w_out, w_ple_gate = w_in.astype(BF16), w_out.astype(BF16), w_ple_gate.astype(BF16)
    h = rmsnorm_rows(x, norm_mix[0], BF16)
    for i in range(depth):
        x, hf = hybrid_mixer_residual(x, h, w_in, i, conv_w[i], a_log[i], dt_bias[i], gdn_norm[i],
                                      attn_sinks[i], bias, w_out, norm_ffn[i], batch, seq)
        if i % 2 == 0:
            act = gate_up(hf, w_dense_gate, w_dense_up, i // 2)
            x = down_residual(act, w_dense_down, i // 2, x)
        else:
            x = moe_block(x, norm_ffn[i], w_router[i // 2], w_exp_gate, w_exp_up, w_exp_down, i // 2)
        last = i == depth - 1
        x, h = ple_residual(norm_ple[i], w_ple_gate, p, w_ple_proj, i, x,
                            norm_final if last else norm_mix[i + 1], out_dtype if last else BF16)
    return h.reshape(batch, seq, d)
```

```python
import functools
import math

import jax
import jax.numpy as jnp
from jax import lax
from jax.experimental import pallas as pl
from jax.experimental.pallas import tpu as pltpu

F32 = jnp.float32
BF16 = jnp.bfloat16

D_MODEL = 2048
PLE_DIM = 256
EPS = 1e-6
NEG_INF = -1e30

SWA_HEADS = 16
SWA_KV_HEADS = 2
SWA_HEAD_DIM = 64
SWA_GROUP = SWA_HEADS // SWA_KV_HEADS
WINDOW = 128
SWA_BLOCK = 128
REL_BUCKETS = 32
REL_MAX_DIST = 128

GDN_QK_HEADS = 4
GDN_V_HEADS = 8
GDN_HEAD_DIM = 128
GDN_CONV = 4
GDN_TILE = 256
GDN_BLOCK = 256

SWA_Q_W = SWA_HEADS * SWA_HEAD_DIM
SWA_KV_W = SWA_KV_HEADS * SWA_HEAD_DIM
GDN_QK_W = GDN_QK_HEADS * GDN_HEAD_DIM
GDN_V_W = GDN_V_HEADS * GDN_HEAD_DIM
GDN_CONV_CH = 2 * GDN_QK_W + GDN_V_W
MIX_WIDTH = SWA_Q_W + GDN_V_W
IN_MAIN_W = SWA_Q_W + 2 * SWA_KV_W + GDN_CONV_CH + GDN_V_W
IN_TAIL_W = 2 * GDN_V_HEADS

D_FF = 7 * D_MODEL // 2
N_EXPERTS = 8
TOP_K = 2

LANES = 128
SUBLANES = 8
VMEM_LIMIT = 56 * 1024 * 1024

COL_K = SWA_Q_W // LANES
COL_V = COL_K + SWA_KV_W // LANES
COL_GDN = COL_V + SWA_KV_W // LANES
COL_Z = COL_GDN + GDN_CONV_CH // LANES


def _params(*sem):
    return pltpu.CompilerParams(dimension_semantics=sem, vmem_limit_bytes=VMEM_LIMIT)


def _rms_kernel(x_ref, g_ref, o_ref):
    x = x_ref[...]
    y = x * lax.rsqrt(jnp.mean(x * x, axis=-1, keepdims=True) + EPS)
    o_ref[...] = (y * g_ref[...]).astype(o_ref.dtype)


def rmsnorm_rows(x, gain, out_dtype, tm=512):
    n, d = x.shape
    return pl.pallas_call(
        _rms_kernel,
        grid=(n // tm,),
        in_specs=[pl.BlockSpec((tm, d), lambda i: (i, 0)), pl.BlockSpec((1, d), lambda i: (0, 0))],
        out_specs=pl.BlockSpec((tm, d), lambda i: (i, 0)),
        out_shape=jax.ShapeDtypeStruct((n, d), out_dtype),
        compiler_params=_params("parallel"),
        name="rmsnorm",
    )(x, gain.reshape(1, d))


def _mm_kernel(a_ref, w_ref, o_ref):
    o_ref[...] = jnp.dot(a_ref[...], w_ref[...], preferred_element_type=F32).astype(o_ref.dtype)


def matmul_cols_outer(a, w, li, n_out, tm, tn, out_dtype=F32):
    m, k = a.shape
    return pl.pallas_call(
        _mm_kernel,
        grid=(n_out // tn, m // tm),
        in_specs=[pl.BlockSpec((tm, k), lambda j, i: (i, 0)),
                  pl.BlockSpec((None, k, tn), lambda j, i: (li, 0, j))],
        out_specs=pl.BlockSpec((tm, tn), lambda j, i: (i, j)),
        out_shape=jax.ShapeDtypeStruct((m, n_out), out_dtype),
        compiler_params=_params("parallel", "parallel"),
        name="in_proj",
    )(a, w)


def _attn_kernel(sink_ref, q_ref, kc_ref, kp_ref, vc_ref, vp_ref, bias_ref, o_ref):
    n = pl.program_id(1)
    kj = lax.broadcasted_iota(jnp.int32, (SWA_BLOCK, SWA_BLOCK), 0)
    qi = lax.broadcasted_iota(jnp.int32, (SWA_BLOCK, SWA_BLOCK), 1)
    in_cur = kj <= qi
    valid = in_cur | (n > 0)
    scale = SWA_HEAD_DIM ** -0.5
    kv_cols = [slice(g * SWA_HEAD_DIM, (g + 1) * SWA_HEAD_DIM) for g in range(SWA_KV_HEADS)]
    kws = [jnp.concatenate([kp_ref[:, c], kc_ref[:, c]], axis=0).astype(BF16) for c in kv_cols]
    vts = [jnp.concatenate([vp_ref[:, c], vc_ref[:, c]], axis=0).T.astype(BF16) for c in kv_cols]
    heads = range(SWA_HEADS)
    qs = [(q_ref[:, h * SWA_HEAD_DIM:(h + 1) * SWA_HEAD_DIM] * scale).astype(BF16) for h in heads]
    boths = [lax.dot_general(kws[h // SWA_GROUP], qs[h], _NT, preferred_element_type=F32) for h in heads]
    logits = [jnp.where(valid, jnp.where(in_cur, b[SWA_BLOCK:], b[:SWA_BLOCK]) + bias_ref[h], NEG_INF)
              for h, b in zip(heads, boths)]
    ms = [jnp.maximum(jnp.max(l, axis=0, keepdims=True), sink_ref[h]) for h, l in zip(heads, logits)]
    es = [jnp.exp(l - m) for l, m in zip(logits, ms)]
    denoms = [jnp.sum(e, axis=0, keepdims=True) + jnp.exp(sink_ref[h] - m) for h, e, m in zip(heads, es, ms)]
    unfolded = [jnp.concatenate([jnp.where(in_cur, 0.0, e), jnp.where(in_cur, e, 0.0)], axis=0).astype(BF16)
                for e in es]
    outs = [jnp.dot(vts[h // SWA_GROUP], u, preferred_element_type=F32) * (1.0 / d)
            for h, u, d in zip(heads, unfolded, denoms)]
    for h0 in range(0, SWA_HEADS, 2):
        both_heads = jnp.concatenate(outs[h0:h0 + 2], axis=0).T
        o_ref[:, h0 * SWA_HEAD_DIM:(h0 + 2) * SWA_HEAD_DIM] = both_heads.astype(o_ref.dtype)


def folded_relative_bias(rel_table):
    kj = jnp.arange(SWA_BLOCK, dtype=jnp.int32)[:, None]
    qi = jnp.arange(SWA_BLOCK, dtype=jnp.int32)[None, :]
    dist = jnp.where(kj <= qi, qi - kj, qi + SWA_BLOCK - kj)
    max_exact = REL_BUCKETS // 2
    d = jnp.maximum(dist, 0)
    log_ratio = jnp.log(jnp.maximum(d, 1).astype(F32) / max_exact) / math.log(REL_MAX_DIST / max_exact)
    large = jnp.minimum(max_exact + (log_ratio * (REL_BUCKETS - max_exact)).astype(jnp.int32), REL_BUCKETS - 1)
    bucket = jnp.where(d < max_exact, d, large)
    onehot = (bucket[None] == jnp.arange(REL_BUCKETS, dtype=jnp.int32)[:, None, None]).astype(F32)
    return jnp.einsum('bh,bjq->hjq', rel_table.astype(F32), onehot, precision=lax.Precision.HIGHEST)


def swa_attention(proj, sinks, bias, batch, seq):
    n = proj.shape[0]
    nb = seq // SWA_BLOCK
    cur = lambda b, i, s: b * nb + i
    prev = lambda b, i, s: jnp.maximum(b * nb + i - 1, 0)
    blk = (SWA_BLOCK, LANES)
    return pl.pallas_call(
        _attn_kernel,
        grid_spec=pltpu.PrefetchScalarGridSpec(
            num_scalar_prefetch=1,
            grid=(batch, nb),
            in_specs=[
                pl.BlockSpec((SWA_BLOCK, SWA_Q_W), lambda b, i, s: (cur(b, i, s), 0)),
                pl.BlockSpec(blk, lambda b, i, s: (cur(b, i, s), COL_K)),
                pl.BlockSpec(blk, lambda b, i, s: (prev(b, i, s), COL_K)),
                pl.BlockSpec(blk, lambda b, i, s: (cur(b, i, s), COL_V)),
                pl.BlockSpec(blk, lambda b, i, s: (prev(b, i, s), COL_V)),
                pl.BlockSpec((SWA_HEADS, SWA_BLOCK, SWA_BLOCK), lambda b, i, s: (0, 0, 0)),
            ],
            out_specs=pl.BlockSpec((SWA_BLOCK, SWA_Q_W), lambda b, i, s: (cur(b, i, s), 0)),
        ),
        out_shape=jax.ShapeDtypeStruct((n, SWA_Q_W), BF16),
        compiler_params=_params("parallel", "parallel"),
        name="swa_attention",
    )(sinks, proj, proj, proj, proj, proj, bias)


def _causal_conv_silu(stage_ref, cur, prev, w):
    tm = cur.shape[0]
    stage_ref[:SUBLANES, :] = prev
    stage_ref[SUBLANES:, :] = cur
    first = SUBLANES - (GDN_CONV - 1)
    acc = stage_ref[first:first + tm, :] * w[0:1]
    for k in range(1, GDN_CONV):
        acc = acc + stage_ref[first + k:first + k + tm, :] * w[k:k + 1]
    return acc * jax.nn.sigmoid(acc)


def _l2norm(y):
    return y * lax.rsqrt(jnp.sum(y * y, axis=-1, keepdims=True) + EPS)


def _gdn_gate_kernel(h_ref, w_ref, alog_ref, dtb_ref, o_ref):
    tm = h_ref.shape[0]
    t = jnp.dot(h_ref[...], w_ref[...], preferred_element_type=F32)
    lane = lax.broadcasted_iota(jnp.int32, t.shape, 1)
    beta = jax.nn.sigmoid(t)
    g = -jnp.exp(alog_ref[...]) * jax.nn.softplus(t + dtb_ref[...])
    g = jnp.where((lane >= GDN_V_HEADS) & (lane < 2 * GDN_V_HEADS), g, 0.0)
    blk = GDN_BLOCK
    r = lax.broadcasted_iota(jnp.int32, (blk, blk), 0)
    s = lax.broadcasted_iota(jnp.int32, (blk, blk), 1)
    tri = jnp.where(s <= r, 1.0, 0.0).astype(BF16)
    sums = []
    for b in range(tm // blk):
        gb = g[b * blk:(b + 1) * blk]
        hi = gb.astype(BF16)
        rest = gb - hi.astype(F32)
        mid = rest.astype(BF16)
        lo = (rest - mid.astype(F32)).astype(BF16)
        sums.append(jnp.dot(tri, hi, preferred_element_type=F32)
                    + (jnp.dot(tri, mid, preferred_element_type=F32) + jnp.dot(tri, lo, preferred_element_type=F32)))
    gcum = jnp.concatenate(sums, axis=0)
    o_ref[...] = jnp.where(lane < GDN_V_HEADS, beta, gcum)


def gdn_gates(h, w_tail, a_log, dt_bias, tm=512):
    n, k = h.shape
    pad = lambda v: jnp.zeros((1, LANES), F32).at[0, GDN_V_HEADS:2 * GDN_V_HEADS].set(v.astype(F32))
    return pl.pallas_call(
        _gdn_gate_kernel,
        grid=(n // tm,),
        in_specs=[pl.BlockSpec((tm, k), lambda i: (i, 0)),
                  pl.BlockSpec((k, LANES), lambda i: (0, 0)),
                  pl.BlockSpec((1, LANES), lambda i: (0, 0)),
                  pl.BlockSpec((1, LANES), lambda i: (0, 0))],
        out_specs=pl.BlockSpec((tm, LANES), lambda i: (i, 0)),
        out_shape=jax.ShapeDtypeStruct((n, LANES), F32),
        compiler_params=_params("parallel"),
        name="gdn_gates",
    )(h, w_tail, pad(a_log), pad(dt_bias))


def _bdot(a, b, dims=(((1,), (0,)), ((), ()))):
    return lax.dot_general(a.astype(BF16), b.astype(BF16), dims, preferred_element_type=F32)


_NT = (((1,), (1,)), ((), ()))
_TN = (((0,), (0,)), ((), ()))


def _unit_lower_inverses(lows, block):
    r_dim = lows[0].shape[0]
    r = lax.broadcasted_iota(jnp.int32, (r_dim, r_dim), 0)
    s = lax.broadcasted_iota(jnp.int32, (r_dim, r_dim), 1)
    level = 31 - lax.clz(r ^ s)
    size = 1
    step = 0
    xs = None
    while size < block:
        sel = level == step
        step += 1
        offs = [jnp.where(sel, low, 0.0) for low in lows]
        if size == 1:
            eye = jnp.where(r == s, 1.0, 0.0).astype(F32)
            xs = [eye - off for off in offs]
        else:
            xo = [_bdot(x, off) for x, off in zip(xs, offs)]
            xs = [x - _bdot(y, x) for x, y in zip(xs, xo)]
        size *= 2
    return xs


def _gdn_kernel(qin_ref, qprev_ref, kin_ref, kprev_ref, vin_ref, vprev_ref, wq_ref, wk_ref, wv_ref,
                z_ref, gb_ref, grow_ref, norm_ref, o_ref,
                state_ref, qstage_ref, kstage_ref, vstage_ref, q_ref, k_ref, v_ref, w_s, u_s, qd_s, kd_s, qk_s):
    hp = pl.program_id(1)
    t = pl.program_id(2)
    tb = qin_ref.shape[0]
    rt = GDN_TILE
    blk = GDN_BLOCK
    pair = GDN_V_HEADS // GDN_QK_HEADS
    d = GDN_HEAD_DIM

    @pl.when(t == 0)
    def _():
        state_ref[...] = jnp.zeros_like(state_ref)

    halo = lambda ref: jnp.where(t == 0, 0.0, ref[...])
    q_ref[...] = _l2norm(_causal_conv_silu(qstage_ref, qin_ref[...], halo(qprev_ref), wq_ref[...]))
    k_ref[...] = _l2norm(_causal_conv_silu(kstage_ref, kin_ref[...], halo(kprev_ref), wk_ref[...]))
    v_all = _causal_conv_silu(vstage_ref, vin_ref[...], halo(vprev_ref), wv_ref[...])
    for hh in range(pair):
        v_ref[hh] = v_all[:, hh * d:(hh + 1) * d]

    lane = lax.broadcasted_iota(jnp.int32, (tb, LANES), 1)
    gb = gb_ref[...]
    r = lax.broadcasted_iota(jnp.int32, (rt, rt), 0)
    s = lax.broadcasted_iota(jnp.int32, (rt, rt), 1)
    same = (r // blk) == (s // blk)
    causal = (s <= r) & same
    strict = (s < r) & same

    g_heads = []
    beta_heads = []
    for hh in range(pair):
        h = pair * hp + hh
        beta_heads.append(jnp.sum(jnp.where(lane == h, gb, 0.0), axis=-1, keepdims=True))
        g_heads.append(jnp.sum(jnp.where(lane == h + GDN_V_HEADS, gb, 0.0), axis=-1, keepdims=True))

    probs = [(hh, ti, slice(ti * rt, (ti + 1) * rt)) for hh in range(pair) for ti in range(tb // rt)]
    qs = {ti: q_ref[ti * rt:(ti + 1) * rt, :] * (d ** -0.5) for ti in range(tb // rt)}
    ks = {ti: k_ref[ti * rt:(ti + 1) * rt, :] for ti in range(tb // rt)}
    betas = [beta_heads[hh][rows] for hh, ti, rows in probs]
    gs = [g_heads[hh][rows] for hh, ti, rows in probs]
    egs = [jnp.exp(g) for g in gs]
    kbs = [ks[ti] * beta for (hh, ti, rows), beta in zip(probs, betas)]
    kqs = [_bdot(jnp.concatenate([kb, qs[ti]], axis=0), ks[ti], _NT) for (hh, ti, rows), kb in zip(probs, kbs)]
    decays = [jnp.where(causal, jnp.exp(jnp.where(causal, g - grow_ref[ti, hh:hh + 1, :], 0.0)), 0.0)
              for (hh, ti, rows), g in zip(probs, gs)]
    lows = [jnp.where(strict, kq[:rt] * decay, 0.0) for kq, decay in zip(kqs, decays)]
    for (hh, ti, rows), kq, decay, g, eg in zip(probs, kqs, decays, gs, egs):
        qk_s[hh, rows, :] = (kq[rt:] * decay).astype(BF16)
        g_last = jnp.concatenate(
            [jnp.broadcast_to(g[(c + 1) * blk - 1:(c + 1) * blk], (blk, 1)) for c in range(rt // blk)], axis=0)
        qd_s[hh, rows, :] = (qs[ti] * eg).astype(BF16)
        kd_s[hh, rows, :] = (ks[ti] * jnp.exp(g_last - g)).astype(BF16)
    rhss = [jnp.concatenate([v_ref[hh, rows, :] * beta, kb * eg], axis=1).astype(BF16)
            for (hh, ti, rows), beta, kb, eg in zip(probs, betas, kbs, egs)]

    tinvs = _unit_lower_inverses(lows, blk)
    uws = [_bdot(tinv, rhs) for tinv, rhs in zip(tinvs, rhss)]
    for (hh, ti, rows), uw in zip(probs, uws):
        u_s[hh, rows, :] = uw[:, :d]
        w_s[hh, rows, :] = uw[:, d:].astype(BF16)

    heads = range(pair)
    for c in range(tb // blk):
        rows = slice(c * blk, (c + 1) * blk)
        off = (c * blk) % rt
        states = [state_ref[hh] for hh in heads]
        ws = [_bdot(jnp.concatenate([w_s[hh, rows, :], qd_s[hh, rows, :]], axis=0), states[hh]) for hh in heads]
        v_new = [u_s[hh, rows, :] - ws[hh][:blk] for hh in heads]
        outs = [ws[hh][blk:] + _bdot(qk_s[hh, rows, off:off + blk], v_new[hh]) for hh in heads]
        for hh in heads:
            g_end = g_heads[hh][(c + 1) * blk - 1:(c + 1) * blk]
            state_ref[hh] = states[hh] * jnp.exp(g_end) + _bdot(kd_s[hh, rows, :], v_new[hh], _TN)
        for hh in heads:
            o = outs[hh]
            o = o * lax.rsqrt(jnp.mean(o * o, axis=-1, keepdims=True) + EPS) * norm_ref[...]
            zc = z_ref[rows, hh * d:(hh + 1) * d]
            o_ref[rows, hh * d:(hh + 1) * d] = (o * (zc * jax.nn.sigmoid(zc))).astype(o_ref.dtype)


def gdn_mixer(proj, conv_w, gates, g_rows, gdn_norm, batch, seq, tb=512):
    n = proj.shape[0]
    nt = seq // tb
    pair = GDN_V_HEADS // GDN_QK_HEADS
    d = GDN_HEAD_DIM
    row = lambda b, hp, t: b * nt + t
    before = lambda b, hp, t: jnp.maximum(row(b, hp, t) * (tb // SUBLANES) - 1, 0)
    wide = (tb, pair * d)
    q_col = COL_GDN
    k_col = COL_GDN + GDN_QK_HEADS
    v_col = (COL_GDN + 2 * GDN_QK_HEADS) // pair
    z_col = COL_Z // pair
    return pl.pallas_call(
        _gdn_kernel,
        grid=(batch, GDN_QK_HEADS, nt),
        in_specs=[
            pl.BlockSpec((tb, d), lambda b, hp, t: (row(b, hp, t), q_col + hp)),
            pl.BlockSpec((SUBLANES, d), lambda b, hp, t: (before(b, hp, t), q_col + hp)),
            pl.BlockSpec((tb, d), lambda b, hp, t: (row(b, hp, t), k_col + hp)),
            pl.BlockSpec((SUBLANES, d), lambda b, hp, t: (before(b, hp, t), k_col + hp)),
            pl.BlockSpec(wide, lambda b, hp, t: (row(b, hp, t), v_col + hp)),
            pl.BlockSpec((SUBLANES, pair * d), lambda b, hp, t: (before(b, hp, t), v_col + hp)),
            pl.BlockSpec((GDN_CONV, d), lambda b, hp, t: (0, hp)),
            pl.BlockSpec((GDN_CONV, d), lambda b, hp, t: (0, GDN_QK_HEADS + hp)),
            pl.BlockSpec((GDN_CONV, pair * d), lambda b, hp, t: (0, 2 * GDN_QK_HEADS // pair + hp)),
            pl.BlockSpec(wide, lambda b, hp, t: (row(b, hp, t), z_col + hp)),
            pl.BlockSpec((tb, LANES), lambda b, hp, t: (row(b, hp, t), 0)),
            pl.BlockSpec((None, tb // GDN_TILE, pair, GDN_TILE), lambda b, hp, t: (hp, row(b, hp, t), 0, 0)),
            pl.BlockSpec((1, LANES), lambda b, hp, t: (0, 0)),
        ],
        out_specs=pl.BlockSpec(wide, lambda b, hp, t: (row(b, hp, t), hp)),
        out_shape=jax.ShapeDtypeStruct((n, GDN_V_W), BF16),
        scratch_shapes=[
            pltpu.VMEM((pair, d, d), F32),
            pltpu.VMEM((SUBLANES + tb, d), F32),
            pltpu.VMEM((SUBLANES + tb, d), F32),
            pltpu.VMEM((SUBLANES + tb, pair * d), F32),
            pltpu.VMEM((tb, d), F32),
            pltpu.VMEM((tb, d), F32),
            pltpu.VMEM((pair, tb, d), F32),
            pltpu.VMEM((pair, tb, d), BF16),
            pltpu.VMEM((pair, tb, d), F32),
            pltpu.VMEM((pair, tb, d), BF16),
            pltpu.VMEM((pair, tb, d), BF16),
            pltpu.VMEM((pair, tb, GDN_TILE), BF16),
        ],
        compiler_params=_params("parallel", "parallel", "arbitrary"),
        name="gated_delta_rule",
    )(proj, proj, proj, proj, proj, proj, conv_w, conv_w, conv_w, proj, gates, g_rows,
      gdn_norm.reshape(1, LANES).astype(F32))


def _rms(x, gain):
    return x * lax.rsqrt(jnp.mean(x * x, axis=-1, keepdims=True) + EPS) * gain


def _out_proj_kernel(a_ref, b_ref, w_ref, x_ref, g_ref, o_ref, h_ref):
    ka = a_ref.shape[1]
    acc = jnp.dot(a_ref[...], w_ref[:ka, :], preferred_element_type=F32)
    acc += jnp.dot(b_ref[...], w_ref[ka:, :], preferred_element_type=F32)
    x = x_ref[...] + acc
    o_ref[...] = x
    h_ref[...] = _rms(x, g_ref[...]).astype(h_ref.dtype)


def out_proj_residual(attn, gdn, w_out, li, x, next_gain, tm=512):
    n, d = x.shape
    ka, kb = attn.shape[1], gdn.shape[1]
    row = lambda w: pl.BlockSpec((tm, w), lambda i: (i, 0))
    return pl.pallas_call(
        _out_proj_kernel,
        grid=(n // tm,),
        in_specs=[row(ka), row(kb),
                  pl.BlockSpec((None, ka + kb, d), lambda i: (li, 0, 0)),
                  row(d),
                  pl.BlockSpec((1, d), lambda i: (0, 0))],
        out_specs=[row(d), row(d)],
        out_shape=[jax.ShapeDtypeStruct((n, d), F32), jax.ShapeDtypeStruct((n, d), BF16)],
        compiler_params=_params("parallel"),
        name="out_proj_residual",
    )(attn, gdn, w_out, x, next_gain.reshape(1, d))


def _swiglu_tile(h, wg, wu):
    g = jnp.dot(h, wg, preferred_element_type=F32)
    u = jnp.dot(h, wu, preferred_element_type=F32)
    return g * jax.nn.sigmoid(g) * u


def _gate_up_kernel(h_ref, wg_ref, wu_ref, o_ref):
    o_ref[...] = _swiglu_tile(h_ref[...], wg_ref[...].astype(BF16), wu_ref[...].astype(BF16)).astype(o_ref.dtype)


def gate_up(h, w_gate, w_up, li, tm=512, tn=1024):
    m, k = h.shape
    f = w_gate.shape[2]
    return pl.pallas_call(
        _gate_up_kernel,
        grid=(f // tn, m // tm),
        in_specs=[
            pl.BlockSpec((tm, k), lambda j, i: (i, 0)),
            pl.BlockSpec((None, k, tn), lambda j, i: (li, 0, j)),
            pl.BlockSpec((None, k, tn), lambda j, i: (li, 0, j)),
        ],
        out_specs=pl.BlockSpec((tm, tn), lambda j, i: (i, j)),
        out_shape=jax.ShapeDtypeStruct((m, f), BF16),
        compiler_params=_params("parallel", "parallel"),
        name="swiglu_gate_up",
    )(h, w_gate, w_up)


def _rows_in_use(meta_ref, o_ref, compute):
    i = pl.program_id(1)
    valid = meta_ref[pl.num_programs(1) + i]

    @pl.when(valid > 0)
    def _():
        o_ref[...] = compute(slice(None))

    @pl.when(valid == 0)
    def _():
        o_ref[...] = jnp.zeros_like(o_ref)


def _gate_up_grouped_kernel(meta_ref, h_ref, wg_ref, wu_ref, o_ref):
    _rows_in_use(meta_ref, o_ref, lambda rows: _swiglu_tile(
        h_ref[rows, :].astype(BF16), wg_ref[...].astype(BF16), wu_ref[...].astype(BF16)).astype(o_ref.dtype))


def gate_up_grouped(meta, h, w_gate, w_up, li, tm, tn=1024):
    m, k = h.shape
    f = w_gate.shape[3]
    nb = m // tm
    row = lambda j, i, meta: (jnp.minimum(i, meta[2 * nb] - 1), 0)
    wmap = lambda j, i, meta: (li, meta[i], 0, j)
    return pl.pallas_call(
        _gate_up_grouped_kernel,
        grid_spec=pltpu.PrefetchScalarGridSpec(
            num_scalar_prefetch=1,
            grid=(f // tn, nb),
            in_specs=[
                pl.BlockSpec((tm, k), row),
                pl.BlockSpec((None, None, k, tn), wmap),
                pl.BlockSpec((None, None, k, tn), wmap),
            ],
            out_specs=pl.BlockSpec((tm, tn), lambda j, i, meta: (i, j)),
        ),
        out_shape=jax.ShapeDtypeStruct((m, f), BF16),
        compiler_params=_params("parallel", "arbitrary"),
        name="moe_gate_up",
    )(meta, h, w_gate, w_up)


def _down_kernel(a_ref, w_ref, x_ref, o_ref):
    o_ref[...] = x_ref[...] + jnp.dot(a_ref[...], w_ref[...].astype(BF16), preferred_element_type=F32)


def down_residual(a, w_down, li, x, tm=512, tn=512):
    n, d = x.shape
    f = a.shape[1]
    return pl.pallas_call(
        _down_kernel,
        grid=(d // tn, n // tm),
        in_specs=[
            pl.BlockSpec((tm, f), lambda j, i: (i, 0)),
            pl.BlockSpec((None, f, tn), lambda j, i: (li, 0, j)),
            pl.BlockSpec((tm, tn), lambda j, i: (i, j)),
        ],
        out_specs=pl.BlockSpec((tm, tn), lambda j, i: (i, j)),
        out_shape=jax.ShapeDtypeStruct((n, d), F32),
        compiler_params=_params("parallel", "parallel"),
        name="swiglu_down_residual",
    )(a, w_down, x)


def _down_grouped_kernel(meta_ref, a_ref, w_ref, o_ref):
    _rows_in_use(meta_ref, o_ref, lambda rows: jnp.dot(
        a_ref[rows, :], w_ref[...].astype(BF16), preferred_element_type=F32))


def down_grouped(meta, a, w_down, li, tm, tn=512):
    m, f = a.shape
    d = w_down.shape[3]
    nb = m // tm
    return pl.pallas_call(
        _down_grouped_kernel,
        grid_spec=pltpu.PrefetchScalarGridSpec(
            num_scalar_prefetch=1,
            grid=(d // tn, nb),
            in_specs=[
                pl.BlockSpec((tm, f), lambda j, i, meta: (jnp.minimum(i, meta[2 * nb] - 1), 0)),
                pl.BlockSpec((None, None, f, tn), lambda j, i, meta: (li, meta[i], 0, j)),
            ],
            out_specs=pl.BlockSpec((tm, tn), lambda j, i, meta: (i, j)),
        ),
        out_shape=jax.ShapeDtypeStruct((m, d), F32),
        compiler_params=_params("parallel", "arbitrary"),
        name="moe_down",
    )(meta, a, w_down)


def _router_kernel(x_ref, g_ref, w_ref, h_ref, idx_ref, gate_ref):
    x = x_ref[...]
    h = x * lax.rsqrt(jnp.mean(x * x, axis=-1, keepdims=True) + EPS) * g_ref[...]
    h_ref[...] = h.astype(h_ref.dtype)
    w = w_ref[...]
    hh, wh = h.astype(BF16), w.astype(BF16)
    hl, wl = (h - hh.astype(F32)).astype(BF16), (w - wh.astype(F32)).astype(BF16)
    logits = jnp.dot(hh, wh, preferred_element_type=F32) + (
        jnp.dot(hh, wl, preferred_element_type=F32) + jnp.dot(hl, wh, preferred_element_type=F32))
    lane = lax.broadcasted_iota(jnp.int32, logits.shape, 1)
    logits = jnp.where(lane < N_EXPERTS, logits, -jnp.inf)
    m1 = jnp.max(logits, axis=-1, keepdims=True)
    i1 = jnp.min(jnp.where(logits == m1, lane, LANES), axis=-1, keepdims=True)
    rest = jnp.where(lane == i1, -jnp.inf, logits)
    m2 = jnp.max(rest, axis=-1, keepdims=True)
    i2 = jnp.min(jnp.where(rest == m2, lane, LANES), axis=-1, keepdims=True)
    e2 = jnp.exp(m2 - m1)
    denom = 1.0 + e2
    idx_ref[...] = jnp.where(lane == 0, i1, jnp.where(lane == 1, i2, 0))
    gate_ref[...] = jnp.where(lane == 0, 1.0 / denom, jnp.where(lane == 1, e2 / denom, 0.0))


def moe_router(x, gain, w_router, tm=256):
    n, d = x.shape
    w = jnp.zeros((d, LANES), F32).at[:, :N_EXPERTS].set(w_router.astype(F32))
    return pl.pallas_call(
        _router_kernel,
        grid=(n // tm,),
        in_specs=[pl.BlockSpec((tm, d), lambda i: (i, 0)),
                  pl.BlockSpec((1, d), lambda i: (0, 0)),
                  pl.BlockSpec((d, LANES), lambda i: (0, 0))],
        out_specs=[pl.BlockSpec((tm, d), lambda i: (i, 0)),
                   pl.BlockSpec((tm, LANES), lambda i: (i, 0)),
                   pl.BlockSpec((tm, LANES), lambda i: (i, 0))],
        out_shape=[jax.ShapeDtypeStruct((n, d), F32),
                   jax.ShapeDtypeStruct((n, LANES), jnp.int32),
                   jax.ShapeDtypeStruct((n, LANES), F32)],
        compiler_params=_params("parallel"),
        name="moe_router",
    )(x, gain.reshape(1, d), w)


def _slot_row(ref, r):
    shift = SUBLANES.bit_length() - 1
    return ref.at[lax.shift_right_logical(r, shift), pl.ds(lax.bitwise_and(r, SUBLANES - 1), 1)]


def _combine_kernel(dest_ref, x_ref, g_ref, ys_ref, o_ref, rows_ref, sems):
    tm = x_ref.shape[0]
    groups = tm // SUBLANES
    half = groups // 2

    def copies(g, sem):
        return [pltpu.make_async_copy(_slot_row(ys_ref, dest_ref[0, (g * SUBLANES + u) * TOP_K + k]),
                                      rows_ref.at[k, g, pl.ds(u, 1)], sem)
                for u in range(SUBLANES) for k in range(TOP_K)]

    def each(lo, hi, sem, action):
        def body(g, c):
            for cp in copies(g, sem):
                action(cp)
            return c
        lax.fori_loop(lo, hi, body, 0)

    parts = ((0, half, sems.at[0]), (half, groups, sems.at[1]))
    for lo, hi, sem in parts:
        each(lo, hi, sem, lambda cp: cp.start())
    for lo, hi, sem in parts:
        each(lo, hi, sem, lambda cp: cp.wait())
        rows = slice(lo * SUBLANES, hi * SUBLANES)
        g = g_ref[rows, :]
        mix = None
        for k in range(TOP_K):
            term = rows_ref[k, lo:hi].reshape((hi - lo) * SUBLANES, -1) * g[:, k:k + 1]
            mix = term if mix is None else mix + term
        o_ref[rows, :] = x_ref[rows, :] + mix


def moe_combine(x, ys, dest, gates, tm=512):
    n, d = x.shape
    cap = ys.shape[0]
    return pl.pallas_call(
        _combine_kernel,
        grid=(n // tm,),
        in_specs=[
            pl.BlockSpec((None, 1, TOP_K * tm), lambda i: (i, 0, 0), memory_space=pltpu.SMEM),
            pl.BlockSpec((tm, d), lambda i: (i, 0)),
            pl.BlockSpec((tm, LANES), lambda i: (i, 0)),
            pl.BlockSpec(memory_space=pl.ANY),
        ],
        out_specs=pl.BlockSpec((tm, d), lambda i: (i, 0)),
        out_shape=jax.ShapeDtypeStruct((n, d), F32),
        scratch_shapes=[pltpu.VMEM((TOP_K, tm // SUBLANES, SUBLANES, d), F32), pltpu.SemaphoreType.DMA((2,))],
        compiler_params=_params("arbitrary"),
        name="moe_combine",
    )(dest.reshape(n // tm, 1, TOP_K * tm), x, gates, ys.reshape(cap // SUBLANES, SUBLANES, d))


def _dispatch_kernel(fill_ref, dest_ref, h_ref, xs_ref, zero_ref, sem):
    i = pl.program_id(0)
    groups_per_tile = h_ref.shape[0]

    slot = functools.partial(_slot_row, xs_ref)

    def run(count, make_copies):
        def start(g, c):
            for cp in make_copies(g):
                cp.start()
            return c

        def wait(g, c):
            for cp in make_copies(g):
                cp.wait()
            return c

        lax.fori_loop(0, count, start, 0)
        lax.fori_loop(0, count, wait, 0)

    @pl.when(i == 0)
    def _():
        zero_ref[...] = jnp.zeros_like(zero_ref)
        runs = fill_ref.shape[0] // 2
        for e in range(runs):
            first = fill_ref[e]
            run(fill_ref[runs + e], lambda a: [pltpu.make_async_copy(zero_ref.at[pl.ds(0, 1)], slot(first + a), sem)])

    def group_copies(g):
        return [pltpu.make_async_copy(h_ref.at[g, pl.ds(u, 1)],
                                      slot(dest_ref[0, (g * SUBLANES + u) * TOP_K + k]), sem)
                for u in range(SUBLANES) for k in range(TOP_K)]

    run(groups_per_tile, group_copies)


def moe_dispatch(h, dest, fill, cap, tm=512):
    n, d = h.shape
    xs = pl.pallas_call(
        _dispatch_kernel,
        grid_spec=pltpu.PrefetchScalarGridSpec(
            num_scalar_prefetch=1,
            grid=(n // tm,),
            in_specs=[
                pl.BlockSpec((None, 1, TOP_K * tm), lambda i, fill: (i, 0, 0), memory_space=pltpu.SMEM),
                pl.BlockSpec((tm // SUBLANES, SUBLANES, d), lambda i, fill: (i, 0, 0)),
            ],
            out_specs=pl.BlockSpec(memory_space=pl.ANY),
            scratch_shapes=[pltpu.VMEM((SUBLANES, d), h.dtype), pltpu.SemaphoreType.DMA(())],
        ),
        out_shape=jax.ShapeDtypeStruct((cap // SUBLANES, SUBLANES, d), h.dtype),
        compiler_params=_params("arbitrary"),
        name="moe_dispatch",
    )(fill, dest.reshape(n // tm, 1, TOP_K * tm), h.reshape(n // SUBLANES, SUBLANES, d))
    return xs.reshape(cap, d)


def moe_block(x, gain, w_router, w_gate, w_up, w_down, li, tm=512):
    n, d = x.shape
    h, idx, gates = moe_router(x, gain, w_router)
    flat_e = idx[:, :TOP_K].reshape(-1)
    na = n * TOP_K
    onehot = (flat_e[None, :] == jnp.arange(N_EXPERTS, dtype=jnp.int32)[:, None]).astype(jnp.int32)
    csum = jnp.cumsum(onehot, axis=1)
    counts = csum[:, -1]
    padded = (counts + tm - 1) // tm * tm
    pad_end = jnp.cumsum(padded)
    pad_start = pad_end - padded
    dest = jnp.sum(onehot * (csum - 1 + pad_start[:, None]), axis=0)
    n_blocks = na // tm + N_EXPERTS
    cap = n_blocks * tm
    fill = jnp.concatenate([pad_start + counts, pad_end[-1:], padded - counts, cap - pad_end[-1:]]).astype(jnp.int32)
    block_e = jnp.minimum(
        jnp.searchsorted(pad_end, jnp.arange(n_blocks, dtype=jnp.int32) * tm, side='right'),
        N_EXPERTS - 1).astype(jnp.int32)
    block_rows = jnp.clip((pad_start + counts)[block_e] - jnp.arange(n_blocks, dtype=jnp.int32) * tm, 0, tm)
    meta = jnp.concatenate([block_e, block_rows, pad_end[-1:] // tm]).astype(jnp.int32)
    xs = moe_dispatch(h, dest, fill, cap)
    act = gate_up_grouped(meta, xs, w_gate, w_up, li, tm)
    ys = down_grouped(meta, act, w_down, li, tm)
    return moe_combine(x, ys, dest, gates)


def _ple_kernel(gain_ref, wg_ref, p_ref, wp_ref, x_ref, next_gain_ref, o_ref, hn_ref):
    x = x_ref[...]
    h = _rms(x, gain_ref[...]).astype(BF16)
    gate = jax.nn.sigmoid(jnp.dot(h, wg_ref[...], preferred_element_type=F32))
    proj = jnp.dot(p_ref[...].astype(BF16), wp_ref[...].astype(BF16), preferred_element_type=F32)
    x = x + proj * gate
    o_ref[...] = x
    hn_ref[...] = _rms(x, next_gain_ref[...]).astype(hn_ref.dtype)


def ple_residual(gain, w_gate, p, w_proj, li, x, next_gain, next_dtype, tm=512):
    n, d = x.shape
    kp = p.shape[2]
    row = lambda w: pl.BlockSpec((tm, w), lambda i: (i, 0))
    vec = pl.BlockSpec((1, d), lambda i: (0, 0))
    return pl.pallas_call(
        _ple_kernel,
        grid=(n // tm,),
        in_specs=[
            vec,
            pl.BlockSpec((None, d, d), lambda i: (li, 0, 0)),
            pl.BlockSpec((None, tm, kp), lambda i: (li, i, 0)),
            pl.BlockSpec((None, kp, d), lambda i: (li, 0, 0)),
            row(d),
            vec,
        ],
        out_specs=[row(d), row(d)],
        out_shape=[jax.ShapeDtypeStruct((n, d), F32), jax.ShapeDtypeStruct((n, d), next_dtype)],
        compiler_params=_params("parallel"),
        name="ple_residual",
    )(gain.reshape(1, d), w_gate, p, w_proj, x, next_gain.reshape(1, d))


def hybrid_mixer_residual(x, h, w_in, li, conv_w, a_log, dt_bias, gdn_norm, sinks, bias, w_out, next_gain,
                          batch, seq):
    n = x.shape[0]
    proj = matmul_cols_outer(h, w_in, li, IN_MAIN_W, tm=512, tn=IN_MAIN_W // 2)
    w_tail = jnp.zeros((D_MODEL, LANES), BF16).at[:, :IN_TAIL_W].set(w_in[li, :, IN_MAIN_W:])
    attn = swa_attention(proj, sinks.astype(F32), bias, batch, seq)
    gates = gdn_gates(h, w_tail, a_log, dt_bias)
    pair = GDN_V_HEADS // GDN_QK_HEADS
    g_rows = gates[:, GDN_V_HEADS:2 * GDN_V_HEADS].T.reshape(GDN_QK_HEADS, pair, n // GDN_TILE, GDN_TILE)
    g_rows = g_rows.transpose(0, 2, 1, 3)
    gdn = gdn_mixer(proj, conv_w, gates, g_rows, gdn_norm, batch, seq)
    return out_proj_residual(attn, gdn, w_out, li, x, next_gain)


def kernel(x, p, w_in, conv_w, a_log, dt_bias, gdn_norm, attn_sinks, rel_bias_table, w_out, norm_mix, norm_ffn, w_dense_gate, w_dense_up, w_dense_down, w_router, w_exp_gate, w_exp_up, w_exp_down, norm_ple, w_ple_gate, w_ple_proj, norm_final):
    batch, seq, d = x.shape
    depth = w_in.shape[0]
    n = batch * seq
    out_dtype = x.dtype
    x = x.reshape(n, d).astype(F32)
    assert WINDOW == SWA_BLOCK
    bias = folded_relative_bias(rel_bias_table)
    p = p.reshape(depth, n, PLE_DIM)
    w_in, w_out, w_ple_gate = w_in.astype(BF16), w_out.astype(BF16), w_ple_gate.astype(BF16)
    h = rmsnorm_rows(x, norm_mix[0], BF16)
    for i in range(depth):
        x, hf = hybrid_mixer_residual(x, h, w_in, i, conv_w[i], a_log[i], dt_bias[i], gdn_norm[i],
                                      attn_sinks[i], bias, w_out, norm_ffn[i], batch, seq)
        if i % 2 == 0:
            act = gate_up(hf, w_dense_gate, w_dense_up, i // 2)
            x = down_residual(act, w_dense_down, i // 2, x)
        else:
            x = moe_block(x, norm_ffn[i], w_router[i // 2], w_exp_gate, w_exp_up, w_exp_down, i // 2)
        last = i == depth - 1
        x, h = ple_residual(norm_ple[i], w_ple_gate, p, w_ple_proj, i, x,
                            norm_final if last else norm_mix[i + 1], out_dtype if last else BF16)
    return h.reshape(batch, seq, d)
```

```python
import functools
import math

import jax
import jax.numpy as jnp
from jax import lax
from jax.experimental import pallas as pl
from jax.experimental.pallas import tpu as pltpu

F32 = jnp.float32
BF16 = jnp.bfloat16

D_MODEL = 2048
PLE_DIM = 256
EPS = 1e-6
NEG_INF = -1e30

SWA_HEADS = 16
SWA_KV_HEADS = 2
SWA_HEAD_DIM = 64
SWA_GROUP = SWA_HEADS // SWA_KV_HEADS
WINDOW = 128
SWA_BLOCK = 128
REL_BUCKETS = 32
REL_MAX_DIST = 128

GDN_QK_HEADS = 4
GDN_V_HEADS = 8
GDN_HEAD_DIM = 128
GDN_CONV = 4
GDN_TILE = 256
GDN_BLOCK = 256

SWA_Q_W = SWA_HEADS * SWA_HEAD_DIM
SWA_KV_W = SWA_KV_HEADS * SWA_HEAD_DIM
GDN_QK_W = GDN_QK_HEADS * GDN_HEAD_DIM
GDN_V_W = GDN_V_HEADS * GDN_HEAD_DIM
GDN_CONV_CH = 2 * GDN_QK_W + GDN_V_W
MIX_WIDTH = SWA_Q_W + GDN_V_W
IN_MAIN_W = SWA_Q_W + 2 * SWA_KV_W + GDN_CONV_CH + GDN_V_W
IN_TAIL_W = 2 * GDN_V_HEADS

D_FF = 7 * D_MODEL // 2
N_EXPERTS = 8
TOP_K = 2

LANES = 128
SUBLANES = 8
VMEM_LIMIT = 56 * 1024 * 1024

COL_K = SWA_Q_W // LANES
COL_V = COL_K + SWA_KV_W // LANES
COL_GDN = COL_V + SWA_KV_W // LANES
COL_Z = COL_GDN + GDN_CONV_CH // LANES


def _params(*sem):
    return pltpu.CompilerParams(dimension_semantics=sem, vmem_limit_bytes=VMEM_LIMIT)


def _rms_kernel(x_ref, g_ref, o_ref):
    x = x_ref[...]
    y = x * lax.rsqrt(jnp.mean(x * x, axis=-1, keepdims=True) + EPS)
    o_ref[...] = (y * g_ref[...]).astype(o_ref.dtype)


def rmsnorm_rows(x, gain, out_dtype, tm=512):
    n, d = x.shape
    return pl.pallas_call(
        _rms_kernel,
        grid=(n // tm,),
        in_specs=[pl.BlockSpec((tm, d), lambda i: (i, 0)), pl.BlockSpec((1, d), lambda i: (0, 0))],
        out_specs=pl.BlockSpec((tm, d), lambda i: (i, 0)),
        out_shape=jax.ShapeDtypeStruct((n, d), out_dtype),
        compiler_params=_params("parallel"),
        name="rmsnorm",
    )(x, gain.reshape(1, d))


def _mm_kernel(a_ref, wt_ref, o_ref):
    o_ref[...] = lax.dot_general(a_ref[...], wt_ref[...].astype(BF16), (((1,), (1,)), ((), ())),
                                 preferred_element_type=F32).astype(o_ref.dtype)


def matmul_cols_outer(a, wt, li, n_out, tm, tn, out_dtype=F32):
    m, k = a.shape
    return pl.pallas_call(
        _mm_kernel,
        grid=(n_out // tn, m // tm),
        in_specs=[pl.BlockSpec((tm, k), lambda j, i: (i, 0)),
                  pl.BlockSpec((None, tn, k), lambda j, i: (li, j, 0))],
        out_specs=pl.BlockSpec((tm, tn), lambda j, i: (i, j)),
        out_shape=jax.ShapeDtypeStruct((m, n_out), out_dtype),
        compiler_params=_params("parallel", "parallel"),
        name="in_proj",
    )(a, wt)


def _attn_kernel(sink_ref, q_ref, kc_ref, kp_ref, vc_ref, vp_ref, bias_ref, o_ref):
    n = pl.program_id(1)
    kj = lax.broadcasted_iota(jnp.int32, (SWA_BLOCK, SWA_BLOCK), 0)
    qi = lax.broadcasted_iota(jnp.int32, (SWA_BLOCK, SWA_BLOCK), 1)
    in_cur = kj <= qi
    valid = in_cur | (n > 0)
    scale = SWA_HEAD_DIM ** -0.5
    kv_cols = [slice(g * SWA_HEAD_DIM, (g + 1) * SWA_HEAD_DIM) for g in range(SWA_KV_HEADS)]
    kws = [jnp.concatenate([kp_ref[:, c], kc_ref[:, c]], axis=0).astype(BF16) for c in kv_cols]
    vts = [jnp.concatenate([vp_ref[:, c], vc_ref[:, c]], axis=0).T.astype(BF16) for c in kv_cols]
    heads = range(SWA_HEADS)
    qs = [(q_ref[:, h * SWA_HEAD_DIM:(h + 1) * SWA_HEAD_DIM] * scale).astype(BF16) for h in heads]
    boths = [lax.dot_general(kws[h // SWA_GROUP], qs[h], _NT, preferred_element_type=F32) for h in heads]
    logits = [jnp.where(valid, jnp.where(in_cur, b[SWA_BLOCK:], b[:SWA_BLOCK]) + bias_ref[h], NEG_INF)
              for h, b in zip(heads, boths)]
    ms = [jnp.maximum(jnp.max(l, axis=0, keepdims=True), sink_ref[h]) for h, l in zip(heads, logits)]
    es = [jnp.exp(l - m) for l, m in zip(logits, ms)]
    denoms = [jnp.sum(e, axis=0, keepdims=True) + jnp.exp(sink_ref[h] - m) for h, e, m in zip(heads, es, ms)]
    unfolded = [jnp.concatenate([jnp.where(in_cur, 0.0, e), jnp.where(in_cur, e, 0.0)], axis=0).astype(BF16)
                for e in es]
    outs = [jnp.dot(vts[h // SWA_GROUP], u, preferred_element_type=F32) * (1.0 / d)
            for h, u, d in zip(heads, unfolded, denoms)]
    for h0 in range(0, SWA_HEADS, 2):
        both_heads = jnp.concatenate(outs[h0:h0 + 2], axis=0).T
        o_ref[:, h0 * SWA_HEAD_DIM:(h0 + 2) * SWA_HEAD_DIM] = both_heads.astype(o_ref.dtype)


def folded_relative_bias(rel_table):
    kj = jnp.arange(SWA_BLOCK, dtype=jnp.int32)[:, None]
    qi = jnp.arange(SWA_BLOCK, dtype=jnp.int32)[None, :]
    dist = jnp.where(kj <= qi, qi - kj, qi + SWA_BLOCK - kj)
    max_exact = REL_BUCKETS // 2
    d = jnp.maximum(dist, 0)
    log_ratio = jnp.log(jnp.maximum(d, 1).astype(F32) / max_exact) / math.log(REL_MAX_DIST / max_exact)
    large = jnp.minimum(max_exact + (log_ratio * (REL_BUCKETS - max_exact)).astype(jnp.int32), REL_BUCKETS - 1)
    bucket = jnp.where(d < max_exact, d, large)
    onehot = (bucket[None] == jnp.arange(REL_BUCKETS, dtype=jnp.int32)[:, None, None]).astype(F32)
    return jnp.einsum('bh,bjq->hjq', rel_table.astype(F32), onehot, precision=lax.Precision.HIGHEST)


def swa_attention(proj, sinks, bias, batch, seq):
    n = proj.shape[0]
    nb = seq // SWA_BLOCK
    cur = lambda b, i, s: b * nb + i
    prev = lambda b, i, s: jnp.maximum(b * nb + i - 1, 0)
    blk = (SWA_BLOCK, LANES)
    return pl.pallas_call(
        _attn_kernel,
        grid_spec=pltpu.PrefetchScalarGridSpec(
            num_scalar_prefetch=1,
            grid=(batch, nb),
            in_specs=[
                pl.BlockSpec((SWA_BLOCK, SWA_Q_W), lambda b, i, s: (cur(b, i, s), 0)),
                pl.BlockSpec(blk, lambda b, i, s: (cur(b, i, s), COL_K)),
                pl.BlockSpec(blk, lambda b, i, s: (prev(b, i, s), COL_K)),
                pl.BlockSpec(blk, lambda b, i, s: (cur(b, i, s), COL_V)),
                pl.BlockSpec(blk, lambda b, i, s: (prev(b, i, s), COL_V)),
                pl.BlockSpec((SWA_HEADS, SWA_BLOCK, SWA_BLOCK), lambda b, i, s: (0, 0, 0)),
            ],
            out_specs=pl.BlockSpec((SWA_BLOCK, SWA_Q_W), lambda b, i, s: (cur(b, i, s), 0)),
        ),
        out_shape=jax.ShapeDtypeStruct((n, SWA_Q_W), BF16),
        compiler_params=_params("parallel", "parallel"),
        name="swa_attention",
    )(sinks, proj, proj, proj, proj, proj, bias)


def _causal_conv_silu(stage_ref, cur, prev, w):
    tm = cur.shape[0]
    stage_ref[:SUBLANES, :] = prev
    stage_ref[SUBLANES:, :] = cur
    first = SUBLANES - (GDN_CONV - 1)
    acc = stage_ref[first:first + tm, :] * w[0:1]
    for k in range(1, GDN_CONV):
        acc = acc + stage_ref[first + k:first + k + tm, :] * w[k:k + 1]
    return acc * jax.nn.sigmoid(acc)


def _l2norm(y):
    return y * lax.rsqrt(jnp.sum(y * y, axis=-1, keepdims=True) + EPS)


def _gdn_gate_kernel(h_ref, w_ref, alog_ref, dtb_ref, o_ref):
    tm = h_ref.shape[0]
    t = jnp.dot(h_ref[...], w_ref[...], preferred_element_type=F32)
    lane = lax.broadcasted_iota(jnp.int32, t.shape, 1)
    beta = jax.nn.sigmoid(t)
    g = -jnp.exp(alog_ref[...]) * jax.nn.softplus(t + dtb_ref[...])
    g = jnp.where((lane >= GDN_V_HEADS) & (lane < 2 * GDN_V_HEADS), g, 0.0)
    blk = GDN_BLOCK
    r = lax.broadcasted_iota(jnp.int32, (blk, blk), 0)
    s = lax.broadcasted_iota(jnp.int32, (blk, blk), 1)
    tri = jnp.where(s <= r, 1.0, 0.0).astype(BF16)
    sums = []
    for b in range(tm // blk):
        gb = g[b * blk:(b + 1) * blk]
        hi = gb.astype(BF16)
        rest = gb - hi.astype(F32)
        mid = rest.astype(BF16)
        lo = (rest - mid.astype(F32)).astype(BF16)
        sums.append(jnp.dot(tri, hi, preferred_element_type=F32)
                    + (jnp.dot(tri, mid, preferred_element_type=F32) + jnp.dot(tri, lo, preferred_element_type=F32)))
    gcum = jnp.concatenate(sums, axis=0)
    o_ref[...] = jnp.where(lane < GDN_V_HEADS, beta, gcum)


def gdn_gates(h, w_tail, a_log, dt_bias, tm=512):
    n, k = h.shape
    pad = lambda v: jnp.zeros((1, LANES), F32).at[0, GDN_V_HEADS:2 * GDN_V_HEADS].set(v.astype(F32))
    return pl.pallas_call(
        _gdn_gate_kernel,
        grid=(n // tm,),
        in_specs=[pl.BlockSpec((tm, k), lambda i: (i, 0)),
                  pl.BlockSpec((k, LANES), lambda i: (0, 0)),
                  pl.BlockSpec((1, LANES), lambda i: (0, 0)),
                  pl.BlockSpec((1, LANES), lambda i: (0, 0))],
        out_specs=pl.BlockSpec((tm, LANES), lambda i: (i, 0)),
        out_shape=jax.ShapeDtypeStruct((n, LANES), F32),
        compiler_params=_params("parallel"),
        name="gdn_gates",
    )(h, w_tail, pad(a_log), pad(dt_bias))


def _bdot(a, b, dims=(((1,), (0,)), ((), ()))):
    return lax.dot_general(a.astype(BF16), b.astype(BF16), dims, preferred_element_type=F32)


_NT = (((1,), (1,)), ((), ()))
_TN = (((0,), (0,)), ((), ()))


def _unit_lower_inverses(lows, block):
    r_dim = lows[0].shape[0]
    r = lax.broadcasted_iota(jnp.int32, (r_dim, r_dim), 0)
    s = lax.broadcasted_iota(jnp.int32, (r_dim, r_dim), 1)
    level = 31 - lax.clz(r ^ s)
    size = 1
    step = 0
    xs = None
    while size < block:
        sel = level == step
        step += 1
        offs = [jnp.where(sel, low, 0.0) for low in lows]
        if size == 1:
            eye = jnp.where(r == s, 1.0, 0.0).astype(F32)
            xs = [eye - off for off in offs]
        else:
            xo = [_bdot(x, off) for x, off in zip(xs, offs)]
            xs = [x - _bdot(y, x) for x, y in zip(xs, xo)]
        size *= 2
    return xs


def _gdn_kernel(qin_ref, qprev_ref, kin_ref, kprev_ref, vin_ref, vprev_ref, wq_ref, wk_ref, wv_ref,
                z_ref, gb_ref, grow_ref, norm_ref, o_ref,
                state_ref, qstage_ref, kstage_ref, vstage_ref, q_ref, k_ref, v_ref, w_s, u_s, qd_s, kd_s, qk_s):
    hp = pl.program_id(1)
    t = pl.program_id(2)
    tb = qin_ref.shape[0]
    rt = GDN_TILE
    blk = GDN_BLOCK
    pair = GDN_V_HEADS // GDN_QK_HEADS
    d = GDN_HEAD_DIM

    @pl.when(t == 0)
    def _():
        state_ref[...] = jnp.zeros_like(state_ref)

    halo = lambda ref: jnp.where(t == 0, 0.0, ref[...])
    q_ref[...] = _l2norm(_causal_conv_silu(qstage_ref, qin_ref[...], halo(qprev_ref), wq_ref[...]))
    k_ref[...] = _l2norm(_causal_conv_silu(kstage_ref, kin_ref[...], halo(kprev_ref), wk_ref[...]))
    v_all = _causal_conv_silu(vstage_ref, vin_ref[...], halo(vprev_ref), wv_ref[...])
    for hh in range(pair):
        v_ref[hh] = v_all[:, hh * d:(hh + 1) * d]

    lane = lax.broadcasted_iota(jnp.int32, (tb, LANES), 1)
    gb = gb_ref[...]
    r = lax.broadcasted_iota(jnp.int32, (rt, rt), 0)
    s = lax.broadcasted_iota(jnp.int32, (rt, rt), 1)
    same = (r // blk) == (s // blk)
    causal = (s <= r) & same
    strict = (s < r) & same

    g_heads = []
    beta_heads = []
    for hh in range(pair):
        h = pair * hp + hh
        beta_heads.append(jnp.sum(jnp.where(lane == h, gb, 0.0), axis=-1, keepdims=True))
        g_heads.append(jnp.sum(jnp.where(lane == h + GDN_V_HEADS, gb, 0.0), axis=-1, keepdims=True))

    probs = [(hh, ti, slice(ti * rt, (ti + 1) * rt)) for hh in range(pair) for ti in range(tb // rt)]
    qs = {ti: q_ref[ti * rt:(ti + 1) * rt, :] * (d ** -0.5) for ti in range(tb // rt)}
    ks = {ti: k_ref[ti * rt:(ti + 1) * rt, :] for ti in range(tb // rt)}
    betas = [beta_heads[hh][rows] for hh, ti, rows in probs]
    gs = [g_heads[hh][rows] for hh, ti, rows in probs]
    egs = [jnp.exp(g) for g in gs]
    kbs = [ks[ti] * beta for (hh, ti, rows), beta in zip(probs, betas)]
    kqs = [_bdot(jnp.concatenate([kb, qs[ti]], axis=0), ks[ti], _NT) for (hh, ti, rows), kb in zip(probs, kbs)]
    decays = [jnp.where(causal, jnp.exp(jnp.where(causal, g - grow_ref[ti, hh:hh + 1, :], 0.0)), 0.0)
              for (hh, ti, rows), g in zip(probs, gs)]
    lows = [jnp.where(strict, kq[:rt] * decay, 0.0) for kq, decay in zip(kqs, decays)]
    for (hh, ti, rows), kq, decay, g, eg in zip(probs, kqs, decays, gs, egs):
        qk_s[hh, rows, :] = (kq[rt:] * decay).astype(BF16)
        g_last = jnp.concatenate(
            [jnp.broadcast_to(g[(c + 1) * blk - 1:(c + 1) * blk], (blk, 1)) for c in range(rt // blk)], axis=0)
        qd_s[hh, rows, :] = (qs[ti] * eg).astype(BF16)
        kd_s[hh, rows, :] = (ks[ti] * jnp.exp(g_last - g)).astype(BF16)
    rhss = [jnp.concatenate([v_ref[hh, rows, :] * beta, kb * eg], axis=1).astype(BF16)
            for (hh, ti, rows), beta, kb, eg in zip(probs, betas, kbs, egs)]

    tinvs = _unit_lower_inverses(lows, blk)
    uws = [_bdot(tinv, rhs) for tinv, rhs in zip(tinvs, rhss)]
    for (hh, ti, rows), uw in zip(probs, uws):
        u_s[hh, rows, :] = uw[:, :d]
        w_s[hh, rows, :] = uw[:, d:].astype(BF16)

    heads = range(pair)
    for c in range(tb // blk):
        rows = slice(c * blk, (c + 1) * blk)
        off = (c * blk) % rt
        states = [state_ref[hh] for hh in heads]
        ws = [_bdot(jnp.concatenate([w_s[hh, rows, :], qd_s[hh, rows, :]], axis=0), states[hh]) for hh in heads]
        v_new = [u_s[hh, rows, :] - ws[hh][:blk] for hh in heads]
        outs = [ws[hh][blk:] + _bdot(qk_s[hh, rows, off:off + blk], v_new[hh]) for hh in heads]
        for hh in heads:
            g_end = g_heads[hh][(c + 1) * blk - 1:(c + 1) * blk]
            state_ref[hh] = states[hh] * jnp.exp(g_end) + _bdot(kd_s[hh, rows, :], v_new[hh], _TN)
        for hh in heads:
            o = outs[hh]
            o = o * lax.rsqrt(jnp.mean(o * o, axis=-1, keepdims=True) + EPS) * norm_ref[...]
            zc = z_ref[rows, hh * d:(hh + 1) * d]
            o_ref[rows, hh * d:(hh + 1) * d] = (o * (zc * jax.nn.sigmoid(zc))).astype(o_ref.dtype)


def gdn_mixer(proj, conv_w, gates, g_rows, gdn_norm, batch, seq, tb=512):
    n = proj.shape[0]
    nt = seq // tb
    pair = GDN_V_HEADS // GDN_QK_HEADS
    d = GDN_HEAD_DIM
    row = lambda b, hp, t: b * nt + t
    before = lambda b, hp, t: jnp.maximum(row(b, hp, t) * (tb // SUBLANES) - 1, 0)
    wide = (tb, pair * d)
    q_col = COL_GDN
    k_col = COL_GDN + GDN_QK_HEADS
    v_col = (COL_GDN + 2 * GDN_QK_HEADS) // pair
    z_col = COL_Z // pair
    return pl.pallas_call(
        _gdn_kernel,
        grid=(batch, GDN_QK_HEADS, nt),
        in_specs=[
            pl.BlockSpec((tb, d), lambda b, hp, t: (row(b, hp, t), q_col + hp)),
            pl.BlockSpec((SUBLANES, d), lambda b, hp, t: (before(b, hp, t), q_col + hp)),
            pl.BlockSpec((tb, d), lambda b, hp, t: (row(b, hp, t), k_col + hp)),
            pl.BlockSpec((SUBLANES, d), lambda b, hp, t: (before(b, hp, t), k_col + hp)),
            pl.BlockSpec(wide, lambda b, hp, t: (row(b, hp, t), v_col + hp)),
            pl.BlockSpec((SUBLANES, pair * d), lambda b, hp, t: (before(b, hp, t), v_col + hp)),
            pl.BlockSpec((GDN_CONV, d), lambda b, hp, t: (0, hp)),
            pl.BlockSpec((GDN_CONV, d), lambda b, hp, t: (0, GDN_QK_HEADS + hp)),
            pl.BlockSpec((GDN_CONV, pair * d), lambda b, hp, t: (0, 2 * GDN_QK_HEADS // pair + hp)),
            pl.BlockSpec(wide, lambda b, hp, t: (row(b, hp, t), z_col + hp)),
            pl.BlockSpec((tb, LANES), lambda b, hp, t: (row(b, hp, t), 0)),
            pl.BlockSpec((None, tb // GDN_TILE, pair, GDN_TILE), lambda b, hp, t: (hp, row(b, hp, t), 0, 0)),
            pl.BlockSpec((1, LANES), lambda b, hp, t: (0, 0)),
        ],
        out_specs=pl.BlockSpec(wide, lambda b, hp, t: (row(b, hp, t), hp)),
        out_shape=jax.ShapeDtypeStruct((n, GDN_V_W), BF16),
        scratch_shapes=[
            pltpu.VMEM((pair, d, d), F32),
            pltpu.VMEM((SUBLANES + tb, d), F32),
            pltpu.VMEM((SUBLANES + tb, d), F32),
            pltpu.VMEM((SUBLANES + tb, pair * d), F32),
            pltpu.VMEM((tb, d), F32),
            pltpu.VMEM((tb, d), F32),
            pltpu.VMEM((pair, tb, d), F32),
            pltpu.VMEM((pair, tb, d), BF16),
            pltpu.VMEM((pair, tb, d), F32),
            pltpu.VMEM((pair, tb, d), BF16),
            pltpu.VMEM((pair, tb, d), BF16),
            pltpu.VMEM((pair, tb, GDN_TILE), BF16),
        ],
        compiler_params=_params("parallel", "parallel", "arbitrary"),
        name="gated_delta_rule",
    )(proj, proj, proj, proj, proj, proj, conv_w, conv_w, conv_w, proj, gates, g_rows,
      gdn_norm.reshape(1, LANES).astype(F32))


def _rms(x, gain):
    return x * lax.rsqrt(jnp.mean(x * x, axis=-1, keepdims=True) + EPS) * gain


def _out_proj_kernel(a_ref, b_ref, w_ref, x_ref, g_ref, o_ref, h_ref):
    ka = a_ref.shape[1]
    acc = jnp.dot(a_ref[...], w_ref[:ka, :], preferred_element_type=F32)
    acc += jnp.dot(b_ref[...], w_ref[ka:, :], preferred_element_type=F32)
    x = x_ref[...] + acc
    o_ref[...] = x
    h_ref[...] = _rms(x, g_ref[...]).astype(h_ref.dtype)


def out_proj_residual(attn, gdn, w_out, li, x, next_gain, tm=512):
    n, d = x.shape
    ka, kb = attn.shape[1], gdn.shape[1]
    row = lambda w: pl.BlockSpec((tm, w), lambda i: (i, 0))
    return pl.pallas_call(
        _out_proj_kernel,
        grid=(n // tm,),
        in_specs=[row(ka), row(kb),
                  pl.BlockSpec((None, ka + kb, d), lambda i: (li, 0, 0)),
                  row(d),
                  pl.BlockSpec((1, d), lambda i: (0, 0))],
        out_specs=[row(d), row(d)],
        out_shape=[jax.ShapeDtypeStruct((n, d), F32), jax.ShapeDtypeStruct((n, d), BF16)],
        compiler_params=_params("parallel"),
        name="out_proj_residual",
    )(attn, gdn, w_out, x, next_gain.reshape(1, d))


def _swiglu_tile(h, wg, wu):
    g = jnp.dot(h, wg, preferred_element_type=F32)
    u = jnp.dot(h, wu, preferred_element_type=F32)
    return g * jax.nn.sigmoid(g) * u


def _gate_up_kernel(h_ref, wg_ref, wu_ref, o_ref):
    o_ref[...] = _swiglu_tile(h_ref[...], wg_ref[...].astype(BF16), wu_ref[...].astype(BF16)).astype(o_ref.dtype)


def gate_up(h, w_gate, w_up, li, tm=512, tn=1024):
    m, k = h.shape
    f = w_gate.shape[2]
    return pl.pallas_call(
        _gate_up_kernel,
        grid=(f // tn, m // tm),
        in_specs=[
            pl.BlockSpec((tm, k), lambda j, i: (i, 0)),
            pl.BlockSpec((None, k, tn), lambda j, i: (li, 0, j)),
            pl.BlockSpec((None, k, tn), lambda j, i: (li, 0, j)),
        ],
        out_specs=pl.BlockSpec((tm, tn), lambda j, i: (i, j)),
        out_shape=jax.ShapeDtypeStruct((m, f), BF16),
        compiler_params=_params("parallel", "parallel"),
        name="swiglu_gate_up",
    )(h, w_gate, w_up)


def _rows_in_use(meta_ref, o_ref, compute):
    i = pl.program_id(1)
    valid = meta_ref[pl.num_programs(1) + i]

    @pl.when(valid > 0)
    def _():
        o_ref[...] = compute(slice(None))

    @pl.when(valid == 0)
    def _():
        o_ref[...] = jnp.zeros_like(o_ref)


def _gate_up_grouped_kernel(meta_ref, h_ref, wg_ref, wu_ref, o_ref):
    _rows_in_use(meta_ref, o_ref, lambda rows: _swiglu_tile(
        h_ref[rows, :].astype(BF16), wg_ref[...].astype(BF16), wu_ref[...].astype(BF16)).astype(o_ref.dtype))


def gate_up_grouped(meta, h, w_gate, w_up, li, tm, tn=1024):
    m, k = h.shape
    f = w_gate.shape[3]
    nb = m // tm
    row = lambda j, i, meta: (jnp.minimum(i, meta[2 * nb] - 1), 0)
    wmap = lambda j, i, meta: (li, meta[i], 0, j)
    return pl.pallas_call(
        _gate_up_grouped_kernel,
        grid_spec=pltpu.PrefetchScalarGridSpec(
            num_scalar_prefetch=1,
            grid=(f // tn, nb),
            in_specs=[
                pl.BlockSpec((tm, k), row),
                pl.BlockSpec((None, None, k, tn), wmap),
                pl.BlockSpec((None, None, k, tn), wmap),
            ],
            out_specs=pl.BlockSpec((tm, tn), lambda j, i, meta: (i, j)),
        ),
        out_shape=jax.ShapeDtypeStruct((m, f), BF16),
        compiler_params=_params("parallel", "arbitrary"),
        name="moe_gate_up",
    )(meta, h, w_gate, w_up)


def _down_kernel(a_ref, w_ref, x_ref, o_ref):
    o_ref[...] = x_ref[...] + jnp.dot(a_ref[...], w_ref[...].astype(BF16), preferred_element_type=F32)


def down_residual(a, w_down, li, x, tm=512, tn=512):
    n, d = x.shape
    f = a.shape[1]
    return pl.pallas_call(
        _down_kernel,
        grid=(d // tn, n // tm),
        in_specs=[
            pl.BlockSpec((tm, f), lambda j, i: (i, 0)),
            pl.BlockSpec((None, f, tn), lambda j, i: (li, 0, j)),
            pl.BlockSpec((tm, tn), lambda j, i: (i, j)),
        ],
        out_specs=pl.BlockSpec((tm, tn), lambda j, i: (i, j)),
        out_shape=jax.ShapeDtypeStruct((n, d), F32),
        compiler_params=_params("parallel", "parallel"),
        name="swiglu_down_residual",
    )(a, w_down, x)


def _down_grouped_kernel(meta_ref, a_ref, w_ref, o_ref):
    _rows_in_use(meta_ref, o_ref, lambda rows: jnp.dot(
        a_ref[rows, :], w_ref[...].astype(BF16), preferred_element_type=F32))


def down_grouped(meta, a, w_down, li, tm, tn=512):
    m, f = a.shape
    d = w_down.shape[3]
    nb = m // tm
    return pl.pallas_call(
        _down_grouped_kernel,
        grid_spec=pltpu.PrefetchScalarGridSpec(
            num_scalar_prefetch=1,
            grid=(d // tn, nb),
            in_specs=[
                pl.BlockSpec((tm, f), lambda j, i, meta: (jnp.minimum(i, meta[2 * nb] - 1), 0)),
                pl.BlockSpec((None, None, f, tn), lambda j, i, meta: (li, meta[i], 0, j)),
            ],
            out_specs=pl.BlockSpec((tm, tn), lambda j, i, meta: (i, j)),
        ),
        out_shape=jax.ShapeDtypeStruct((m, d), F32),
        compiler_params=_params("parallel", "arbitrary"),
        name="moe_down",
    )(meta, a, w_down)


def _router_kernel(x_ref, g_ref, w_ref, h_ref, idx_ref, gate_ref):
    x = x_ref[...]
    h = x * lax.rsqrt(jnp.mean(x * x, axis=-1, keepdims=True) + EPS) * g_ref[...]
    h_ref[...] = h.astype(h_ref.dtype)
    w = w_ref[...]
    hh, wh = h.astype(BF16), w.astype(BF16)
    hl, wl = (h - hh.astype(F32)).astype(BF16), (w - wh.astype(F32)).astype(BF16)
    logits = jnp.dot(hh, wh, preferred_element_type=F32) + (
        jnp.dot(hh, wl, preferred_element_type=F32) + jnp.dot(hl, wh, preferred_element_type=F32))
    lane = lax.broadcasted_iota(jnp.int32, logits.shape, 1)
    logits = jnp.where(lane < N_EXPERTS, logits, -jnp.inf)
    m1 = jnp.max(logits, axis=-1, keepdims=True)
    i1 = jnp.min(jnp.where(logits == m1, lane, LANES), axis=-1, keepdims=True)
    rest = jnp.where(lane == i1, -jnp.inf, logits)
    m2 = jnp.max(rest, axis=-1, keepdims=True)
    i2 = jnp.min(jnp.where(rest == m2, lane, LANES), axis=-1, keepdims=True)
    e2 = jnp.exp(m2 - m1)
    denom = 1.0 + e2
    idx_ref[...] = jnp.where(lane == 0, i1, jnp.where(lane == 1, i2, 0))
    gate_ref[...] = jnp.where(lane == 0, 1.0 / denom, jnp.where(lane == 1, e2 / denom, 0.0))


def moe_router(x, gain, w_router, tm=256):
    n, d = x.shape
    w = jnp.zeros((d, LANES), F32).at[:, :N_EXPERTS].set(w_router.astype(F32))
    return pl.pallas_call(
        _router_kernel,
        grid=(n // tm,),
        in_specs=[pl.BlockSpec((tm, d), lambda i: (i, 0)),
                  pl.BlockSpec((1, d), lambda i: (0, 0)),
                  pl.BlockSpec((d, LANES), lambda i: (0, 0))],
        out_specs=[pl.BlockSpec((tm, d), lambda i: (i, 0)),
                   pl.BlockSpec((tm, LANES), lambda i: (i, 0)),
                   pl.BlockSpec((tm, LANES), lambda i: (i, 0))],
        out_shape=[jax.ShapeDtypeStruct((n, d), F32),
                   jax.ShapeDtypeStruct((n, LANES), jnp.int32),
                   jax.ShapeDtypeStruct((n, LANES), F32)],
        compiler_params=_params("parallel"),
        name="moe_router",
    )(x, gain.reshape(1, d), w)


def _slot_row(ref, r):
    shift = SUBLANES.bit_length() - 1
    return ref.at[lax.shift_right_logical(r, shift), pl.ds(lax.bitwise_and(r, SUBLANES - 1), 1)]


def _combine_kernel(dest_ref, x_ref, g_ref, ys_ref, o_ref, rows_ref, sems):
    tm = x_ref.shape[0]
    groups = tm // SUBLANES
    half = groups // 2

    def copies(g, sem):
        return [pltpu.make_async_copy(_slot_row(ys_ref, dest_ref[0, (g * SUBLANES + u) * TOP_K + k]),
                                      rows_ref.at[k, g, pl.ds(u, 1)], sem)
                for u in range(SUBLANES) for k in range(TOP_K)]

    def each(lo, hi, sem, action):
        def body(g, c):
            for cp in copies(g, sem):
                action(cp)
            return c
        lax.fori_loop(lo, hi, body, 0)

    parts = ((0, half, sems.at[0]), (half, groups, sems.at[1]))
    for lo, hi, sem in parts:
        each(lo, hi, sem, lambda cp: cp.start())
    for lo, hi, sem in parts:
        each(lo, hi, sem, lambda cp: cp.wait())
        rows = slice(lo * SUBLANES, hi * SUBLANES)
        g = g_ref[rows, :]
        mix = None
        for k in range(TOP_K):
            term = rows_ref[k, lo:hi].reshape((hi - lo) * SUBLANES, -1) * g[:, k:k + 1]
            mix = term if mix is None else mix + term
        o_ref[rows, :] = x_ref[rows, :] + mix


def moe_combine(x, ys, dest, gates, tm=512):
    n, d = x.shape
    cap = ys.shape[0]
    return pl.pallas_call(
        _combine_kernel,
        grid=(n // tm,),
        in_specs=[
            pl.BlockSpec((None, 1, TOP_K * tm), lambda i: (i, 0, 0), memory_space=pltpu.SMEM),
            pl.BlockSpec((tm, d), lambda i: (i, 0)),
            pl.BlockSpec((tm, LANES), lambda i: (i, 0)),
            pl.BlockSpec(memory_space=pl.ANY),
        ],
        out_specs=pl.BlockSpec((tm, d), lambda i: (i, 0)),
        out_shape=jax.ShapeDtypeStruct((n, d), F32),
        scratch_shapes=[pltpu.VMEM((TOP_K, tm // SUBLANES, SUBLANES, d), F32), pltpu.SemaphoreType.DMA((2,))],
        compiler_params=_params("arbitrary"),
        name="moe_combine",
    )(dest.reshape(n // tm, 1, TOP_K * tm), x, gates, ys.reshape(cap // SUBLANES, SUBLANES, d))


def _dispatch_kernel(fill_ref, dest_ref, h_ref, xs_ref, zero_ref, sem):
    i = pl.program_id(0)
    groups_per_tile = h_ref.shape[0]

    slot = functools.partial(_slot_row, xs_ref)

    def run(count, make_copies):
        def start(g, c):
            for cp in make_copies(g):
                cp.start()
            return c

        def wait(g, c):
            for cp in make_copies(g):
                cp.wait()
            return c

        lax.fori_loop(0, count, start, 0)
        lax.fori_loop(0, count, wait, 0)

    @pl.when(i == 0)
    def _():
        zero_ref[...] = jnp.zeros_like(zero_ref)
        runs = fill_ref.shape[0] // 2
        for e in range(runs):
            first = fill_ref[e]
            run(fill_ref[runs + e], lambda a: [pltpu.make_async_copy(zero_ref.at[pl.ds(0, 1)], slot(first + a), sem)])

    def group_copies(g):
        return [pltpu.make_async_copy(h_ref.at[g, pl.ds(u, 1)],
                                      slot(dest_ref[0, (g * SUBLANES + u) * TOP_K + k]), sem)
                for u in range(SUBLANES) for k in range(TOP_K)]

    run(groups_per_tile, group_copies)


def moe_dispatch(h, dest, fill, cap, tm=512):
    n, d = h.shape
    xs = pl.pallas_call(
        _dispatch_kernel,
        grid_spec=pltpu.PrefetchScalarGridSpec(
            num_scalar_prefetch=1,
            grid=(n // tm,),
            in_specs=[
                pl.BlockSpec((None, 1, TOP_K * tm), lambda i, fill: (i, 0, 0), memory_space=pltpu.SMEM),
                pl.BlockSpec((tm // SUBLANES, SUBLANES, d), lambda i, fill: (i, 0, 0)),
            ],
            out_specs=pl.BlockSpec(memory_space=pl.ANY),
            scratch_shapes=[pltpu.VMEM((SUBLANES, d), h.dtype), pltpu.SemaphoreType.DMA(())],
        ),
        out_shape=jax.ShapeDtypeStruct((cap // SUBLANES, SUBLANES, d), h.dtype),
        compiler_params=_params("arbitrary"),
        name="moe_dispatch",
    )(fill, dest.reshape(n // tm, 1, TOP_K * tm), h.reshape(n // SUBLANES, SUBLANES, d))
    return xs.reshape(cap, d)


def moe_block(x, gain, w_router, w_gate, w_up, w_down, li, tm=512):
    n, d = x.shape
    h, idx, gates = moe_router(x, gain, w_router)
    flat_e = idx[:, :TOP_K].reshape(-1)
    na = n * TOP_K
    onehot = (flat_e[None, :] == jnp.arange(N_EXPERTS, dtype=jnp.int32)[:, None]).astype(jnp.int32)
    csum = jnp.cumsum(onehot, axis=1)
    counts = csum[:, -1]
    padded = (counts + tm - 1) // tm * tm
    pad_end = jnp.cumsum(padded)
    pad_start = pad_end - padded
    dest = jnp.sum(onehot * (csum - 1 + pad_start[:, None]), axis=0)
    n_blocks = na // tm + N_EXPERTS
    cap = n_blocks * tm
    fill = jnp.concatenate([pad_start + counts, pad_end[-1:], padded - counts, cap - pad_end[-1:]]).astype(jnp.int32)
    block_e = jnp.minimum(
        jnp.searchsorted(pad_end, jnp.arange(n_blocks, dtype=jnp.int32) * tm, side='right'),
        N_EXPERTS - 1).astype(jnp.int32)
    block_rows = jnp.clip((pad_start + counts)[block_e] - jnp.arange(n_blocks, dtype=jnp.int32) * tm, 0, tm)
    meta = jnp.concatenate([block_e, block_rows, pad_end[-1:] // tm]).astype(jnp.int32)
    xs = moe_dispatch(h, dest, fill, cap)
    act = gate_up_grouped(meta, xs, w_gate, w_up, li, tm)
    ys = down_grouped(meta, act, w_down, li, tm)
    return moe_combine(x, ys, dest, gates)


def _ple_kernel(gain_ref, wg_ref, p_ref, wp_ref, x_ref, next_gain_ref, o_ref, hn_ref):
    x = x_ref[...]
    h = _rms(x, gain_ref[...]).astype(BF16)
    gate = jax.nn.sigmoid(jnp.dot(h, wg_ref[...], preferred_element_type=F32))
    proj = jnp.dot(p_ref[...].astype(BF16), wp_ref[...].astype(BF16), preferred_element_type=F32)
    x = x + proj * gate
    o_ref[...] = x
    hn_ref[...] = _rms(x, next_gain_ref[...]).astype(hn_ref.dtype)


def ple_residual(gain, w_gate, p, w_proj, li, x, next_gain, next_dtype, tm=512):
    n, d = x.shape
    kp = p.shape[2]
    row = lambda w: pl.BlockSpec((tm, w), lambda i: (i, 0))
    vec = pl.BlockSpec((1, d), lambda i: (0, 0))
    return pl.pallas_call(
        _ple_kernel,
        grid=(n // tm,),
        in_specs=[
            vec,
            pl.BlockSpec((None, d, d), lambda i: (li, 0, 0)),
            pl.BlockSpec((None, tm, kp), lambda i: (li, i, 0)),
            pl.BlockSpec((None, kp, d), lambda i: (li, 0, 0)),
            row(d),
            vec,
        ],
        out_specs=[row(d), row(d)],
        out_shape=[jax.ShapeDtypeStruct((n, d), F32), jax.ShapeDtypeStruct((n, d), next_dtype)],
        compiler_params=_params("parallel"),
        name="ple_residual",
    )(gain.reshape(1, d), w_gate, p, w_proj, x, next_gain.reshape(1, d))


def hybrid_mixer_residual(x, h, w_in, w_tail, li, conv_w, a_log, dt_bias, gdn_norm, sinks, bias, w_out, next_gain,
                          batch, seq):
    n = x.shape[0]
    proj = matmul_cols_outer(h, w_in, li, IN_MAIN_W, tm=512, tn=IN_MAIN_W // 2)
    attn = swa_attention(proj, sinks.astype(F32), bias, batch, seq)
    gates = gdn_gates(h, w_tail, a_log, dt_bias)
    pair = GDN_V_HEADS // GDN_QK_HEADS
    g_rows = gates[:, GDN_V_HEADS:2 * GDN_V_HEADS].T.reshape(GDN_QK_HEADS, pair, n // GDN_TILE, GDN_TILE)
    g_rows = g_rows.transpose(0, 2, 1, 3)
    gdn = gdn_mixer(proj, conv_w, gates, g_rows, gdn_norm, batch, seq)
    return out_proj_residual(attn, gdn, w_out, li, x, next_gain)


def kernel(x, p, w_in, conv_w, a_log, dt_bias, gdn_norm, attn_sinks, rel_bias_table, w_out, norm_mix, norm_ffn, w_dense_gate, w_dense_up, w_dense_down, w_router, w_exp_gate, w_exp_up, w_exp_down, norm_ple, w_ple_gate, w_ple_proj, norm_final):
    batch, seq, d = x.shape
    depth = w_in.shape[0]
    n = batch * seq
    out_dtype = x.dtype
    x = x.reshape(n, d).astype(F32)
    assert WINDOW == SWA_BLOCK
    bias = folded_relative_bias(rel_bias_table)
    p = p.reshape(depth, n, PLE_DIM)
    w_out, w_ple_gate = w_out.astype(BF16), w_ple_gate.astype(BF16)
    w_tails = jnp.zeros((depth, D_MODEL, LANES), BF16).at[:, :, :IN_TAIL_W].set(w_in[:, :, IN_MAIN_W:].astype(BF16))
    w_in = jnp.swapaxes(w_in, 1, 2)
    h = rmsnorm_rows(x, norm_mix[0], BF16)
    for i in range(depth):
        x, hf = hybrid_mixer_residual(x, h, w_in, w_tails[i], i, conv_w[i], a_log[i], dt_bias[i], gdn_norm[i],
                                      attn_sinks[i], bias, w_out, norm_ffn[i], batch, seq)
        if i % 2 == 0:
            act = gate_up(hf, w_dense_gate, w_dense_up, i // 2)
            x = down_residual(act, w_dense_down, i // 2, x)
        else:
            x = moe_block(x, norm_ffn[i], w_router[i // 2], w_exp_gate, w_exp_up, w_exp_down, i // 2)
        last = i == depth - 1
        x, h = ple_residual(norm_ple[i], w_ple_gate, p, w_ple_proj, i, x,
                            norm_final if last else norm_mix[i + 1], out_dtype if last else BF16)
    return h.reshape(batch, seq, d)
```

```python
import functools
import math

import jax
import jax.numpy as jnp
from jax import lax
from jax.experimental import pallas as pl
from jax.experimental.pallas import tpu as pltpu

F32 = jnp.float32
BF16 = jnp.bfloat16

D_MODEL = 2048
PLE_DIM = 256
EPS = 1e-6
NEG_INF = -1e30

SWA_HEADS = 16
SWA_KV_HEADS = 2
SWA_HEAD_DIM = 64
SWA_GROUP = SWA_HEADS // SWA_KV_HEADS
WINDOW = 128
SWA_BLOCK = 128
REL_BUCKETS = 32
REL_MAX_DIST = 128

GDN_QK_HEADS = 4
GDN_V_HEADS = 8
GDN_HEAD_DIM = 128
GDN_CONV = 4
GDN_TILE = 256
GDN_BLOCK = 256

SWA_Q_W = SWA_HEADS * SWA_HEAD_DIM
SWA_KV_W = SWA_KV_HEADS * SWA_HEAD_DIM
GDN_QK_W = GDN_QK_HEADS * GDN_HEAD_DIM
GDN_V_W = GDN_V_HEADS * GDN_HEAD_DIM
GDN_CONV_CH = 2 * GDN_QK_W + GDN_V_W
MIX_WIDTH = SWA_Q_W + GDN_V_W
IN_MAIN_W = SWA_Q_W + 2 * SWA_KV_W + GDN_CONV_CH + GDN_V_W
IN_TAIL_W = 2 * GDN_V_HEADS

D_FF = 7 * D_MODEL // 2
N_EXPERTS = 8
TOP_K = 2

LANES = 128
SUBLANES = 8
VMEM_LIMIT = 56 * 1024 * 1024

COL_K = SWA_Q_W // LANES
COL_V = COL_K + SWA_KV_W // LANES
COL_GDN = COL_V + SWA_KV_W // LANES
COL_Z = COL_GDN + GDN_CONV_CH // LANES


def _params(*sem):
    return pltpu.CompilerParams(dimension_semantics=sem, vmem_limit_bytes=VMEM_LIMIT)


def _rms_kernel(x_ref, g_ref, o_ref):
    x = x_ref[...]
    y = x * lax.rsqrt(jnp.mean(x * x, axis=-1, keepdims=True) + EPS)
    o_ref[...] = (y * g_ref[...]).astype(o_ref.dtype)


def rmsnorm_rows(x, gain, out_dtype, tm=512):
    n, d = x.shape
    return pl.pallas_call(
        _rms_kernel,
        grid=(n // tm,),
        in_specs=[pl.BlockSpec((tm, d), lambda i: (i, 0)), pl.BlockSpec((1, d), lambda i: (0, 0))],
        out_specs=pl.BlockSpec((tm, d), lambda i: (i, 0)),
        out_shape=jax.ShapeDtypeStruct((n, d), out_dtype),
        compiler_params=_params("parallel"),
        name="rmsnorm",
    )(x, gain.reshape(1, d))


def _mm_kernel(a_ref, wt_ref, o_ref):
    o_ref[...] = lax.dot_general(a_ref[...], wt_ref[...].astype(BF16), (((1,), (1,)), ((), ())),
                                 preferred_element_type=F32).astype(o_ref.dtype)


def matmul_cols_outer(a, wt, li, n_out, tm, tn, out_dtype=F32):
    m, k = a.shape
    return pl.pallas_call(
        _mm_kernel,
        grid=(n_out // tn, m // tm),
        in_specs=[pl.BlockSpec((tm, k), lambda j, i: (i, 0)),
                  pl.BlockSpec((None, tn, k), lambda j, i: (li, j, 0))],
        out_specs=pl.BlockSpec((tm, tn), lambda j, i: (i, j)),
        out_shape=jax.ShapeDtypeStruct((m, n_out), out_dtype),
        compiler_params=_params("parallel", "parallel"),
        name="in_proj",
    )(a, wt)


def _attn_kernel(sink_ref, q_ref, kc_ref, kp_ref, vc_ref, vp_ref, bias_ref, o_ref):
    n = pl.program_id(1)
    kj = lax.broadcasted_iota(jnp.int32, (SWA_BLOCK, SWA_BLOCK), 0)
    qi = lax.broadcasted_iota(jnp.int32, (SWA_BLOCK, SWA_BLOCK), 1)
    in_cur = kj <= qi
    valid = in_cur | (n > 0)
    scale = SWA_HEAD_DIM ** -0.5
    kv_cols = [slice(g * SWA_HEAD_DIM, (g + 1) * SWA_HEAD_DIM) for g in range(SWA_KV_HEADS)]
    kws = [jnp.concatenate([kp_ref[:, c], kc_ref[:, c]], axis=0).astype(BF16) for c in kv_cols]
    vts = [jnp.concatenate([vp_ref[:, c], vc_ref[:, c]], axis=0).T.astype(BF16) for c in kv_cols]
    heads = range(SWA_HEADS)
    qs = [(q_ref[:, h * SWA_HEAD_DIM:(h + 1) * SWA_HEAD_DIM] * scale).astype(BF16) for h in heads]
    boths = [lax.dot_general(kws[h // SWA_GROUP], qs[h], _NT, preferred_element_type=F32) for h in heads]
    logits = [jnp.where(valid, jnp.where(in_cur, b[SWA_BLOCK:], b[:SWA_BLOCK]) + bias_ref[h], NEG_INF)
              for h, b in zip(heads, boths)]
    ms = [jnp.maximum(jnp.max(l, axis=0, keepdims=True), sink_ref[h]) for h, l in zip(heads, logits)]
    es = [jnp.exp(l - m) for l, m in zip(logits, ms)]
    denoms = [jnp.sum(e, axis=0, keepdims=True) + jnp.exp(sink_ref[h] - m) for h, e, m in zip(heads, es, ms)]
    unfolded = [jnp.concatenate([jnp.where(in_cur, 0.0, e), jnp.where(in_cur, e, 0.0)], axis=0).astype(BF16)
                for e in es]
    outs = [jnp.dot(vts[h // SWA_GROUP], u, preferred_element_type=F32) * (1.0 / d)
            for h, u, d in zip(heads, unfolded, denoms)]
    for h0 in range(0, SWA_HEADS, 2):
        both_heads = jnp.concatenate(outs[h0:h0 + 2], axis=0).T
        o_ref[:, h0 * SWA_HEAD_DIM:(h0 + 2) * SWA_HEAD_DIM] = both_heads.astype(o_ref.dtype)


def folded_relative_bias(rel_table):
    kj = jnp.arange(SWA_BLOCK, dtype=jnp.int32)[:, None]
    qi = jnp.arange(SWA_BLOCK, dtype=jnp.int32)[None, :]
    dist = jnp.where(kj <= qi, qi - kj, qi + SWA_BLOCK - kj)
    max_exact = REL_BUCKETS // 2
    d = jnp.maximum(dist, 0)
    log_ratio = jnp.log(jnp.maximum(d, 1).astype(F32) / max_exact) / math.log(REL_MAX_DIST / max_exact)
    large = jnp.minimum(max_exact + (log_ratio * (REL_BUCKETS - max_exact)).astype(jnp.int32), REL_BUCKETS - 1)
    bucket = jnp.where(d < max_exact, d, large)
    onehot = (bucket[None] == jnp.arange(REL_BUCKETS, dtype=jnp.int32)[:, None, None]).astype(F32)
    return jnp.einsum('bh,bjq->hjq', rel_table.astype(F32), onehot, precision=lax.Precision.HIGHEST)


def swa_attention(proj, sinks, bias, batch, seq):
    n = proj.shape[0]
    nb = seq // SWA_BLOCK
    cur = lambda b, i, s: b * nb + i
    prev = lambda b, i, s: jnp.maximum(b * nb + i - 1, 0)
    blk = (SWA_BLOCK, LANES)
    return pl.pallas_call(
        _attn_kernel,
        grid_spec=pltpu.PrefetchScalarGridSpec(
            num_scalar_prefetch=1,
            grid=(batch, nb),
            in_specs=[
                pl.BlockSpec((SWA_BLOCK, SWA_Q_W), lambda b, i, s: (cur(b, i, s), 0)),
                pl.BlockSpec(blk, lambda b, i, s: (cur(b, i, s), COL_K)),
                pl.BlockSpec(blk, lambda b, i, s: (prev(b, i, s), COL_K)),
                pl.BlockSpec(blk, lambda b, i, s: (cur(b, i, s), COL_V)),
                pl.BlockSpec(blk, lambda b, i, s: (prev(b, i, s), COL_V)),
                pl.BlockSpec((SWA_HEADS, SWA_BLOCK, SWA_BLOCK), lambda b, i, s: (0, 0, 0)),
            ],
            out_specs=pl.BlockSpec((SWA_BLOCK, SWA_Q_W), lambda b, i, s: (cur(b, i, s), 0)),
        ),
        out_shape=jax.ShapeDtypeStruct((n, SWA_Q_W), BF16),
        compiler_params=_params("parallel", "parallel"),
        name="swa_attention",
    )(sinks, proj, proj, proj, proj, proj, bias)


def _causal_conv_silu(stage_ref, cur, prev, w):
    tm = cur.shape[0]
    stage_ref[:SUBLANES, :] = prev
    stage_ref[SUBLANES:, :] = cur
    first = SUBLANES - (GDN_CONV - 1)
    acc = stage_ref[first:first + tm, :] * w[0:1]
    for k in range(1, GDN_CONV):
        acc = acc + stage_ref[first + k:first + k + tm, :] * w[k:k + 1]
    return acc * jax.nn.sigmoid(acc)


def _l2norm(y):
    return y * lax.rsqrt(jnp.sum(y * y, axis=-1, keepdims=True) + EPS)


def _gdn_gate_kernel(h_ref, w_ref, alog_ref, dtb_ref, o_ref):
    tm = h_ref.shape[0]
    t = jnp.dot(h_ref[...], w_ref[...], preferred_element_type=F32)
    lane = lax.broadcasted_iota(jnp.int32, t.shape, 1)
    beta = jax.nn.sigmoid(t)
    g = -jnp.exp(alog_ref[...]) * jax.nn.softplus(t + dtb_ref[...])
    g = jnp.where((lane >= GDN_V_HEADS) & (lane < 2 * GDN_V_HEADS), g, 0.0)
    blk = GDN_BLOCK
    r = lax.broadcasted_iota(jnp.int32, (blk, blk), 0)
    s = lax.broadcasted_iota(jnp.int32, (blk, blk), 1)
    tri = jnp.where(s <= r, 1.0, 0.0).astype(BF16)
    sums = []
    for b in range(tm // blk):
        gb = g[b * blk:(b + 1) * blk]
        hi = gb.astype(BF16)
        rest = gb - hi.astype(F32)
        mid = rest.astype(BF16)
        lo = (rest - mid.astype(F32)).astype(BF16)
        sums.append(jnp.dot(tri, hi, preferred_element_type=F32)
                    + (jnp.dot(tri, mid, preferred_element_type=F32) + jnp.dot(tri, lo, preferred_element_type=F32)))
    gcum = jnp.concatenate(sums, axis=0)
    o_ref[...] = jnp.where(lane < GDN_V_HEADS, beta, gcum)


def gdn_gates(h, w_tail, a_log, dt_bias, tm=512):
    n, k = h.shape
    pad = lambda v: jnp.zeros((1, LANES), F32).at[0, GDN_V_HEADS:2 * GDN_V_HEADS].set(v.astype(F32))
    return pl.pallas_call(
        _gdn_gate_kernel,
        grid=(n // tm,),
        in_specs=[pl.BlockSpec((tm, k), lambda i: (i, 0)),
                  pl.BlockSpec((k, LANES), lambda i: (0, 0)),
                  pl.BlockSpec((1, LANES), lambda i: (0, 0)),
                  pl.BlockSpec((1, LANES), lambda i: (0, 0))],
        out_specs=pl.BlockSpec((tm, LANES), lambda i: (i, 0)),
        out_shape=jax.ShapeDtypeStruct((n, LANES), F32),
        compiler_params=_params("parallel"),
        name="gdn_gates",
    )(h, w_tail, pad(a_log), pad(dt_bias))


def _bdot(a, b, dims=(((1,), (0,)), ((), ()))):
    return lax.dot_general(a.astype(BF16), b.astype(BF16), dims, preferred_element_type=F32)


_NT = (((1,), (1,)), ((), ()))
_TN = (((0,), (0,)), ((), ()))


def _unit_lower_inverses(lows, block):
    r_dim = lows[0].shape[0]
    r = lax.broadcasted_iota(jnp.int32, (r_dim, r_dim), 0)
    s = lax.broadcasted_iota(jnp.int32, (r_dim, r_dim), 1)
    level = 31 - lax.clz(r ^ s)
    size = 1
    step = 0
    xs = None
    while size < block:
        sel = level == step
        step += 1
        offs = [jnp.where(sel, low, 0.0) for low in lows]
        if size == 1:
            eye = jnp.where(r == s, 1.0, 0.0).astype(F32)
            xs = [eye - off for off in offs]
        else:
            xo = [_bdot(x, off) for x, off in zip(xs, offs)]
            xs = [x - _bdot(y, x) for x, y in zip(xs, xo)]
        size *= 2
    return xs


def _gdn_kernel(qin_ref, qprev_ref, kin_ref, kprev_ref, vin_ref, vprev_ref, wq_ref, wk_ref, wv_ref,
                z_ref, gb_ref, grow_ref, norm_ref, o_ref,
                state_ref, qstage_ref, kstage_ref, vstage_ref, q_ref, k_ref, v_ref, w_s, u_s, qd_s, kd_s, qk_s):
    hp = pl.program_id(1)
    t = pl.program_id(2)
    tb = qin_ref.shape[0]
    rt = GDN_TILE
    blk = GDN_BLOCK
    pair = GDN_V_HEADS // GDN_QK_HEADS
    d = GDN_HEAD_DIM

    @pl.when(t == 0)
    def _():
        state_ref[...] = jnp.zeros_like(state_ref)

    halo = lambda ref: jnp.where(t == 0, 0.0, ref[...])
    q_ref[...] = _l2norm(_causal_conv_silu(qstage_ref, qin_ref[...], halo(qprev_ref), wq_ref[...]))
    k_ref[...] = _l2norm(_causal_conv_silu(kstage_ref, kin_ref[...], halo(kprev_ref), wk_ref[...]))
    v_all = _causal_conv_silu(vstage_ref, vin_ref[...], halo(vprev_ref), wv_ref[...])
    for hh in range(pair):
        v_ref[hh] = v_all[:, hh * d:(hh + 1) * d]

    lane = lax.broadcasted_iota(jnp.int32, (tb, LANES), 1)
    gb = gb_ref[...]
    r = lax.broadcasted_iota(jnp.int32, (rt, rt), 0)
    s = lax.broadcasted_iota(jnp.int32, (rt, rt), 1)
    same = (r // blk) == (s // blk)
    causal = (s <= r) & same
    strict = (s < r) & same

    g_heads = []
    beta_heads = []
    for hh in range(pair):
        h = pair * hp + hh
        beta_heads.append(jnp.sum(jnp.where(lane == h, gb, 0.0), axis=-1, keepdims=True))
        g_heads.append(jnp.sum(jnp.where(lane == h + GDN_V_HEADS, gb, 0.0), axis=-1, keepdims=True))

    probs = [(hh, ti, slice(ti * rt, (ti + 1) * rt)) for hh in range(pair) for ti in range(tb // rt)]
    qs = {ti: q_ref[ti * rt:(ti + 1) * rt, :] * (d ** -0.5) for ti in range(tb // rt)}
    ks = {ti: k_ref[ti * rt:(ti + 1) * rt, :] for ti in range(tb // rt)}
    betas = [beta_heads[hh][rows] for hh, ti, rows in probs]
    gs = [g_heads[hh][rows] for hh, ti, rows in probs]
    egs = [jnp.exp(g) for g in gs]
    kbs = [ks[ti] * beta for (hh, ti, rows), beta in zip(probs, betas)]
    kqs = [_bdot(jnp.concatenate([kb, qs[ti]], axis=0), ks[ti], _NT) for (hh, ti, rows), kb in zip(probs, kbs)]
    decays = [jnp.where(causal, jnp.exp(jnp.where(causal, g - grow_ref[ti, hh:hh + 1, :], 0.0)), 0.0)
              for (hh, ti, rows), g in zip(probs, gs)]
    lows = [jnp.where(strict, kq[:rt] * decay, 0.0) for kq, decay in zip(kqs, decays)]
    for (hh, ti, rows), kq, decay, g, eg in zip(probs, kqs, decays, gs, egs):
        qk_s[hh, rows, :] = (kq[rt:] * decay).astype(BF16)
        g_last = jnp.concatenate(
            [jnp.broadcast_to(g[(c + 1) * blk - 1:(c + 1) * blk], (blk, 1)) for c in range(rt // blk)], axis=0)
        qd_s[hh, rows, :] = (qs[ti] * eg).astype(BF16)
        kd_s[hh, rows, :] = (ks[ti] * jnp.exp(g_last - g)).astype(BF16)
    rhss = [jnp.concatenate([v_ref[hh, rows, :] * beta, kb * eg], axis=1).astype(BF16)
            for (hh, ti, rows), beta, kb, eg in zip(probs, betas, kbs, egs)]

    tinvs = _unit_lower_inverses(lows, blk)
    uws = [_bdot(tinv, rhs) for tinv, rhs in zip(tinvs, rhss)]
    for (hh, ti, rows), uw in zip(probs, uws):
        u_s[hh, rows, :] = uw[:, :d]
        w_s[hh, rows, :] = uw[:, d:].astype(BF16)

    heads = range(pair)
    for c in range(tb // blk):
        rows = slice(c * blk, (c + 1) * blk)
        off = (c * blk) % rt
        states = [state_ref[hh] for hh in heads]
        ws = [_bdot(jnp.concatenate([w_s[hh, rows, :], qd_s[hh, rows, :]], axis=0), states[hh]) for hh in heads]
        v_new = [u_s[hh, rows, :] - ws[hh][:blk] for hh in heads]
        outs = [ws[hh][blk:] + _bdot(qk_s[hh, rows, off:off + blk], v_new[hh]) for hh in heads]
        for hh in heads:
            g_end = g_heads[hh][(c + 1) * blk - 1:(c + 1) * blk]
            state_ref[hh] = states[hh] * jnp.exp(g_end) + _bdot(kd_s[hh, rows, :], v_new[hh], _TN)
        for hh in heads:
            o = outs[hh]
            o = o * lax.rsqrt(jnp.mean(o * o, axis=-1, keepdims=True) + EPS) * norm_ref[...]
            zc = z_ref[rows, hh * d:(hh + 1) * d]
            o_ref[rows, hh * d:(hh + 1) * d] = (o * (zc * jax.nn.sigmoid(zc))).astype(o_ref.dtype)


def gdn_mixer(proj, conv_w, gates, g_rows, gdn_norm, batch, seq, tb=512):
    n = proj.shape[0]
    nt = seq // tb
    pair = GDN_V_HEADS // GDN_QK_HEADS
    d = GDN_HEAD_DIM
    row = lambda b, hp, t: b * nt + t
    before = lambda b, hp, t: jnp.maximum(row(b, hp, t) * (tb // SUBLANES) - 1, 0)
    wide = (tb, pair * d)
    q_col = COL_GDN
    k_col = COL_GDN + GDN_QK_HEADS
    v_col = (COL_GDN + 2 * GDN_QK_HEADS) // pair
    z_col = COL_Z // pair
    return pl.pallas_call(
        _gdn_kernel,
        grid=(batch, GDN_QK_HEADS, nt),
        in_specs=[
            pl.BlockSpec((tb, d), lambda b, hp, t: (row(b, hp, t), q_col + hp)),
            pl.BlockSpec((SUBLANES, d), lambda b, hp, t: (before(b, hp, t), q_col + hp)),
            pl.BlockSpec((tb, d), lambda b, hp, t: (row(b, hp, t), k_col + hp)),
            pl.BlockSpec((SUBLANES, d), lambda b, hp, t: (before(b, hp, t), k_col + hp)),
            pl.BlockSpec(wide, lambda b, hp, t: (row(b, hp, t), v_col + hp)),
            pl.BlockSpec((SUBLANES, pair * d), lambda b, hp, t: (before(b, hp, t), v_col + hp)),
            pl.BlockSpec((GDN_CONV, d), lambda b, hp, t: (0, hp)),
            pl.BlockSpec((GDN_CONV, d), lambda b, hp, t: (0, GDN_QK_HEADS + hp)),
            pl.BlockSpec((GDN_CONV, pair * d), lambda b, hp, t: (0, 2 * GDN_QK_HEADS // pair + hp)),
            pl.BlockSpec(wide, lambda b, hp, t: (row(b, hp, t), z_col + hp)),
            pl.BlockSpec((tb, LANES), lambda b, hp, t: (row(b, hp, t), 0)),
            pl.BlockSpec((None, tb // GDN_TILE, pair, GDN_TILE), lambda b, hp, t: (hp, row(b, hp, t), 0, 0)),
            pl.BlockSpec((1, LANES), lambda b, hp, t: (0, 0)),
        ],
        out_specs=pl.BlockSpec(wide, lambda b, hp, t: (row(b, hp, t), hp)),
        out_shape=jax.ShapeDtypeStruct((n, GDN_V_W), BF16),
        scratch_shapes=[
            pltpu.VMEM((pair, d, d), F32),
            pltpu.VMEM((SUBLANES + tb, d), F32),
            pltpu.VMEM((SUBLANES + tb, d), F32),
            pltpu.VMEM((SUBLANES + tb, pair * d), F32),
            pltpu.VMEM((tb, d), F32),
            pltpu.VMEM((tb, d), F32),
            pltpu.VMEM((pair, tb, d), F32),
            pltpu.VMEM((pair, tb, d), BF16),
            pltpu.VMEM((pair, tb, d), F32),
            pltpu.VMEM((pair, tb, d), BF16),
            pltpu.VMEM((pair, tb, d), BF16),
            pltpu.VMEM((pair, tb, GDN_TILE), BF16),
        ],
        compiler_params=_params("parallel", "parallel", "arbitrary"),
        name="gated_delta_rule",
    )(proj, proj, proj, proj, proj, proj, conv_w, conv_w, conv_w, proj, gates, g_rows,
      gdn_norm.reshape(1, LANES).astype(F32))


def _rms(x, gain):
    return x * lax.rsqrt(jnp.mean(x * x, axis=-1, keepdims=True) + EPS) * gain


def _out_proj_kernel(a_ref, b_ref, w_ref, x_ref, g_ref, o_ref, h_ref):
    ka = a_ref.shape[1]
    acc = jnp.dot(a_ref[...], w_ref[:ka, :], preferred_element_type=F32)
    acc += jnp.dot(b_ref[...], w_ref[ka:, :], preferred_element_type=F32)
    x = x_ref[...] + acc
    o_ref[...] = x
    h_ref[...] = _rms(x, g_ref[...]).astype(h_ref.dtype)


def out_proj_residual(attn, gdn, w_out, li, x, next_gain, tm=512):
    n, d = x.shape
    ka, kb = attn.shape[1], gdn.shape[1]
    row = lambda w: pl.BlockSpec((tm, w), lambda i: (i, 0))
    return pl.pallas_call(
        _out_proj_kernel,
        grid=(n // tm,),
        in_specs=[row(ka), row(kb),
                  pl.BlockSpec((None, ka + kb, d), lambda i: (li, 0, 0)),
                  row(d),
                  pl.BlockSpec((1, d), lambda i: (0, 0))],
        out_specs=[row(d), row(d)],
        out_shape=[jax.ShapeDtypeStruct((n, d), F32), jax.ShapeDtypeStruct((n, d), BF16)],
        compiler_params=_params("parallel"),
        name="out_proj_residual",
    )(attn, gdn, w_out, x, next_gain.reshape(1, d))


def _swiglu_tile(h, wg, wu):
    g = jnp.dot(h, wg, preferred_element_type=F32)
    u = jnp.dot(h, wu, preferred_element_type=F32)
    return g * jax.nn.sigmoid(g) * u


def _gate_up_kernel(h_ref, wg_ref, wu_ref, o_ref):
    o_ref[...] = _swiglu_tile(h_ref[...], wg_ref[...].astype(BF16), wu_ref[...].astype(BF16)).astype(o_ref.dtype)


def gate_up(h, w_gate, w_up, li, tm=512, tn=1024):
    m, k = h.shape
    f = w_gate.shape[2]
    return pl.pallas_call(
        _gate_up_kernel,
        grid=(f // tn, m // tm),
        in_specs=[
            pl.BlockSpec((tm, k), lambda j, i: (i, 0)),
            pl.BlockSpec((None, k, tn), lambda j, i: (li, 0, j)),
            pl.BlockSpec((None, k, tn), lambda j, i: (li, 0, j)),
        ],
        out_specs=pl.BlockSpec((tm, tn), lambda j, i: (i, j)),
        out_shape=jax.ShapeDtypeStruct((m, f), BF16),
        compiler_params=_params("parallel", "parallel"),
        name="swiglu_gate_up",
    )(h, w_gate, w_up)


def _rows_in_use(meta_ref, o_ref, compute):
    i = pl.program_id(1)
    valid = meta_ref[pl.num_programs(1) + i]

    @pl.when(valid > 0)
    def _():
        o_ref[...] = compute(slice(None))

    @pl.when(valid == 0)
    def _():
        o_ref[...] = jnp.zeros_like(o_ref)


def _gate_up_grouped_kernel(meta_ref, h_ref, wg_ref, wu_ref, o_ref):
    _rows_in_use(meta_ref, o_ref, lambda rows: _swiglu_tile(
        h_ref[rows, :].astype(BF16), wg_ref[...].astype(BF16), wu_ref[...].astype(BF16)).astype(o_ref.dtype))


def gate_up_grouped(meta, h, w_gate, w_up, li, tm, tn=1024):
    m, k = h.shape
    f = w_gate.shape[3]
    nb = m // tm
    row = lambda j, i, meta: (jnp.minimum(i, meta[2 * nb] - 1), 0)
    wmap = lambda j, i, meta: (li, meta[i], 0, j)
    return pl.pallas_call(
        _gate_up_grouped_kernel,
        grid_spec=pltpu.PrefetchScalarGridSpec(
            num_scalar_prefetch=1,
            grid=(f // tn, nb),
            in_specs=[
                pl.BlockSpec((tm, k), row),
                pl.BlockSpec((None, None, k, tn), wmap),
                pl.BlockSpec((None, None, k, tn), wmap),
            ],
            out_specs=pl.BlockSpec((tm, tn), lambda j, i, meta: (i, j)),
        ),
        out_shape=jax.ShapeDtypeStruct((m, f), BF16),
        compiler_params=_params("parallel", "arbitrary"),
        name="moe_gate_up",
    )(meta, h, w_gate, w_up)


def _down_kernel(a_ref, w_ref, x_ref, o_ref):
    o_ref[...] = x_ref[...] + jnp.dot(a_ref[...], w_ref[...].astype(BF16), preferred_element_type=F32)


def down_residual(a, w_down, li, x, tm=512, tn=512):
    n, d = x.shape
    f = a.shape[1]
    return pl.pallas_call(
        _down_kernel,
        grid=(d // tn, n // tm),
        in_specs=[
            pl.BlockSpec((tm, f), lambda j, i: (i, 0)),
            pl.BlockSpec((None, f, tn), lambda j, i: (li, 0, j)),
            pl.BlockSpec((tm, tn), lambda j, i: (i, j)),
        ],
        out_specs=pl.BlockSpec((tm, tn), lambda j, i: (i, j)),
        out_shape=jax.ShapeDtypeStruct((n, d), F32),
        compiler_params=_params("parallel", "parallel"),
        name="swiglu_down_residual",
    )(a, w_down, x)


def _down_grouped_kernel(meta_ref, a_ref, w_ref, o_ref):
    _rows_in_use(meta_ref, o_ref, lambda rows: jnp.dot(
        a_ref[rows, :], w_ref[...].astype(BF16), preferred_element_type=F32))


def down_grouped(meta, a, w_down, li, tm, tn=512):
    m, f = a.shape
    d = w_down.shape[3]
    nb = m // tm
    return pl.pallas_call(
        _down_grouped_kernel,
        grid_spec=pltpu.PrefetchScalarGridSpec(
            num_scalar_prefetch=1,
            grid=(d // tn, nb),
            in_specs=[
                pl.BlockSpec((tm, f), lambda j, i, meta: (jnp.minimum(i, meta[2 * nb] - 1), 0)),
                pl.BlockSpec((None, None, f, tn), lambda j, i, meta: (li, meta[i], 0, j)),
            ],
            out_specs=pl.BlockSpec((tm, tn), lambda j, i, meta: (i, j)),
        ),
        out_shape=jax.ShapeDtypeStruct((m, d), F32),
        compiler_params=_params("parallel", "arbitrary"),
        name="moe_down",
    )(meta, a, w_down)


def _router_kernel(x_ref, g_ref, w_ref, h_ref, idx_ref, gate_ref):
    x = x_ref[...]
    h = x * lax.rsqrt(jnp.mean(x * x, axis=-1, keepdims=True) + EPS) * g_ref[...]
    h_ref[...] = h.astype(h_ref.dtype)
    w = w_ref[...]
    hh, wh = h.astype(BF16), w.astype(BF16)
    hl, wl = (h - hh.astype(F32)).astype(BF16), (w - wh.astype(F32)).astype(BF16)
    logits = jnp.dot(hh, wh, preferred_element_type=F32) + (
        jnp.dot(hh, wl, preferred_element_type=F32) + jnp.dot(hl, wh, preferred_element_type=F32))
    lane = lax.broadcasted_iota(jnp.int32, logits.shape, 1)
    logits = jnp.where(lane < N_EXPERTS, logits, -jnp.inf)
    m1 = jnp.max(logits, axis=-1, keepdims=True)
    i1 = jnp.min(jnp.where(logits == m1, lane, LANES), axis=-1, keepdims=True)
    rest = jnp.where(lane == i1, -jnp.inf, logits)
    m2 = jnp.max(rest, axis=-1, keepdims=True)
    i2 = jnp.min(jnp.where(rest == m2, lane, LANES), axis=-1, keepdims=True)
    e2 = jnp.exp(m2 - m1)
    denom = 1.0 + e2
    idx_ref[...] = jnp.where(lane == 0, i1, jnp.where(lane == 1, i2, 0))
    gate_ref[...] = jnp.where(lane == 0, 1.0 / denom, jnp.where(lane == 1, e2 / denom, 0.0))


def moe_router(x, gain, w_router, tm=256):
    n, d = x.shape
    w = jnp.zeros((d, LANES), F32).at[:, :N_EXPERTS].set(w_router.astype(F32))
    return pl.pallas_call(
        _router_kernel,
        grid=(n // tm,),
        in_specs=[pl.BlockSpec((tm, d), lambda i: (i, 0)),
                  pl.BlockSpec((1, d), lambda i: (0, 0)),
                  pl.BlockSpec((d, LANES), lambda i: (0, 0))],
        out_specs=[pl.BlockSpec((tm, d), lambda i: (i, 0)),
                   pl.BlockSpec((tm, LANES), lambda i: (i, 0)),
                   pl.BlockSpec((tm, LANES), lambda i: (i, 0))],
        out_shape=[jax.ShapeDtypeStruct((n, d), F32),
                   jax.ShapeDtypeStruct((n, LANES), jnp.int32),
                   jax.ShapeDtypeStruct((n, LANES), F32)],
        compiler_params=_params("parallel"),
        name="moe_router",
    )(x, gain.reshape(1, d), w)


def _slot_row(ref, r):
    shift = SUBLANES.bit_length() - 1
    return ref.at[lax.shift_right_logical(r, shift), pl.ds(lax.bitwise_and(r, SUBLANES - 1), 1)]


def _combine_kernel(dest_ref, x_ref, g_ref, ys_ref, o_ref, rows_ref, sems):
    tm = x_ref.shape[0]
    groups = tm // SUBLANES
    half = groups // 2

    def copies(g, sem):
        return [pltpu.make_async_copy(_slot_row(ys_ref, dest_ref[0, (g * SUBLANES + u) * TOP_K + k]),
                                      rows_ref.at[k, g, pl.ds(u, 1)], sem)
                for u in range(SUBLANES) for k in range(TOP_K)]

    def each(lo, hi, sem, action):
        def body(g, c):
            for j, cp in enumerate(copies(g, sem)):
                action(j, cp)
            return c
        lax.fori_loop(lo, hi, body, 0)

    parts = ((0, half, sems.at[0]), (half, groups, sems.at[1]))
    for lo, hi, sem in parts:
        each(lo, hi, sem, lambda j, cp: cp.start(priority=j % 2))
    for lo, hi, sem in parts:
        each(lo, hi, sem, lambda j, cp: cp.wait())
        rows = slice(lo * SUBLANES, hi * SUBLANES)
        g = g_ref[rows, :]
        mix = None
        for k in range(TOP_K):
            term = rows_ref[k, lo:hi].reshape((hi - lo) * SUBLANES, -1) * g[:, k:k + 1]
            mix = term if mix is None else mix + term
        o_ref[rows, :] = x_ref[rows, :] + mix


def moe_combine(x, ys, dest, gates, tm=512):
    n, d = x.shape
    cap = ys.shape[0]
    return pl.pallas_call(
        _combine_kernel,
        grid=(n // tm,),
        in_specs=[
            pl.BlockSpec((None, 1, TOP_K * tm), lambda i: (i, 0, 0), memory_space=pltpu.SMEM),
            pl.BlockSpec((tm, d), lambda i: (i, 0)),
            pl.BlockSpec((tm, LANES), lambda i: (i, 0)),
            pl.BlockSpec(memory_space=pl.ANY),
        ],
        out_specs=pl.BlockSpec((tm, d), lambda i: (i, 0)),
        out_shape=jax.ShapeDtypeStruct((n, d), F32),
        scratch_shapes=[pltpu.VMEM((TOP_K, tm // SUBLANES, SUBLANES, d), F32), pltpu.SemaphoreType.DMA((2,))],
        compiler_params=_params("arbitrary"),
        name="moe_combine",
    )(dest.reshape(n // tm, 1, TOP_K * tm), x, gates, ys.reshape(cap // SUBLANES, SUBLANES, d))


def _dispatch_kernel(fill_ref, dest_ref, h_ref, xs_ref, zero_ref, sem):
    i = pl.program_id(0)
    groups_per_tile = h_ref.shape[0]

    slot = functools.partial(_slot_row, xs_ref)

    def run(count, make_copies):
        def start(g, c):
            for j, cp in enumerate(make_copies(g)):
                cp.start(priority=j % 2)
            return c

        def wait(g, c):
            for cp in make_copies(g):
                cp.wait()
            return c

        lax.fori_loop(0, count, start, 0)
        lax.fori_loop(0, count, wait, 0)

    @pl.when(i == 0)
    def _():
        zero_ref[...] = jnp.zeros_like(zero_ref)
        runs = fill_ref.shape[0] // 2
        for e in range(runs):
            first = fill_ref[e]
            run(fill_ref[runs + e], lambda a: [pltpu.make_async_copy(zero_ref.at[pl.ds(0, 1)], slot(first + a), sem)])

    def group_copies(g):
        return [pltpu.make_async_copy(h_ref.at[g, pl.ds(u, 1)],
                                      slot(dest_ref[0, (g * SUBLANES + u) * TOP_K + k]), sem)
                for u in range(SUBLANES) for k in range(TOP_K)]

    run(groups_per_tile, group_copies)


def moe_dispatch(h, dest, fill, cap, tm=512):
    n, d = h.shape
    xs = pl.pallas_call(
        _dispatch_kernel,
        grid_spec=pltpu.PrefetchScalarGridSpec(
            num_scalar_prefetch=1,
            grid=(n // tm,),
            in_specs=[
                pl.BlockSpec((None, 1, TOP_K * tm), lambda i, fill: (i, 0, 0), memory_space=pltpu.SMEM),
                pl.BlockSpec((tm // SUBLANES, SUBLANES, d), lambda i, fill: (i, 0, 0)),
            ],
            out_specs=pl.BlockSpec(memory_space=pl.ANY),
            scratch_shapes=[pltpu.VMEM((SUBLANES, d), h.dtype), pltpu.SemaphoreType.DMA(())],
        ),
        out_shape=jax.ShapeDtypeStruct((cap // SUBLANES, SUBLANES, d), h.dtype),
        compiler_params=_params("arbitrary"),
        name="moe_dispatch",
    )(fill, dest.reshape(n // tm, 1, TOP_K * tm), h.reshape(n // SUBLANES, SUBLANES, d))
    return xs.reshape(cap, d)


def moe_block(x, gain, w_router, w_gate, w_up, w_down, li, tm=512):
    n, d = x.shape
    h, idx, gates = moe_router(x, gain, w_router)
    flat_e = idx[:, :TOP_K].reshape(-1)
    na = n * TOP_K
    onehot = (flat_e[None, :] == jnp.arange(N_EXPERTS, dtype=jnp.int32)[:, None]).astype(jnp.int32)
    csum = jnp.cumsum(onehot, axis=1)
    counts = csum[:, -1]
    padded = (counts + tm - 1) // tm * tm
    pad_end = jnp.cumsum(padded)
    pad_start = pad_end - padded
    dest = jnp.sum(onehot * (csum - 1 + pad_start[:, None]), axis=0)
    n_blocks = na // tm + N_EXPERTS
    cap = n_blocks * tm
    fill = jnp.concatenate([pad_start + counts, pad_end[-1:], padded - counts, cap - pad_end[-1:]]).astype(jnp.int32)
    block_e = jnp.minimum(
        jnp.searchsorted(pad_end, jnp.arange(n_blocks, dtype=jnp.int32) * tm, side='right'),
        N_EXPERTS - 1).astype(jnp.int32)
    block_rows = jnp.clip((pad_start + counts)[block_e] - jnp.arange(n_blocks, dtype=jnp.int32) * tm, 0, tm)
    meta = jnp.concatenate([block_e, block_rows, pad_end[-1:] // tm]).astype(jnp.int32)
    xs = moe_dispatch(h, dest, fill, cap)
    act = gate_up_grouped(meta, xs, w_gate, w_up, li, tm)
    ys = down_grouped(meta, act, w_down, li, tm)
    return moe_combine(x, ys, dest, gates)


def _ple_kernel(gain_ref, wg_ref, p_ref, wp_ref, x_ref, next_gain_ref, o_ref, hn_ref):
    x = x_ref[...]
    h = _rms(x, gain_ref[...]).astype(BF16)
    gate = jax.nn.sigmoid(jnp.dot(h, wg_ref[...], preferred_element_type=F32))
    proj = jnp.dot(p_ref[...].astype(BF16), wp_ref[...].astype(BF16), preferred_element_type=F32)
    x = x + proj * gate
    o_ref[...] = x
    hn_ref[...] = _rms(x, next_gain_ref[...]).astype(hn_ref.dtype)


def ple_residual(gain, w_gate, p, w_proj, li, x, next_gain, next_dtype, tm=512):
    n, d = x.shape
    kp = p.shape[2]
    row = lambda w: pl.BlockSpec((tm, w), lambda i: (i, 0))
    vec = pl.BlockSpec((1, d), lambda i: (0, 0))
    return pl.pallas_call(
        _ple_kernel,
        grid=(n // tm,),
        in_specs=[
            vec,
            pl.BlockSpec((None, d, d), lambda i: (li, 0, 0)),
            pl.BlockSpec((None, tm, kp), lambda i: (li, i, 0)),
            pl.BlockSpec((None, kp, d), lambda i: (li, 0, 0)),
            row(d),
            vec,
        ],
        out_specs=[row(d), row(d)],
        out_shape=[jax.ShapeDtypeStruct((n, d), F32), jax.ShapeDtypeStruct((n, d), next_dtype)],
        compiler_params=_params("parallel"),
        name="ple_residual",
    )(gain.reshape(1, d), w_gate, p, w_proj, x, next_gain.reshape(1, d))


def hybrid_mixer_residual(x, h, w_in, w_tail, li, conv_w, a_log, dt_bias, gdn_norm, sinks, bias, w_out, next_gain,
                          batch, seq):
    n = x.shape[0]
    proj = matmul_cols_outer(h, w_in, li, IN_MAIN_W, tm=512, tn=IN_MAIN_W // 2)
    attn = swa_attention(proj, sinks.astype(F32), bias, batch, seq)
    gates = gdn_gates(h, w_tail, a_log, dt_bias)
    pair = GDN_V_HEADS // GDN_QK_HEADS
    g_rows = gates[:, GDN_V_HEADS:2 * GDN_V_HEADS].T.reshape(GDN_QK_HEADS, pair, n // GDN_TILE, GDN_TILE)
    g_rows = g_rows.transpose(0, 2, 1, 3)
    gdn = gdn_mixer(proj, conv_w, gates, g_rows, gdn_norm, batch, seq)
    return out_proj_residual(attn, gdn, w_out, li, x, next_gain)


def kernel(x, p, w_in, conv_w, a_log, dt_bias, gdn_norm, attn_sinks, rel_bias_table, w_out, norm_mix, norm_ffn, w_dense_gate, w_dense_up, w_dense_down, w_router, w_exp_gate, w_exp_up, w_exp_down, norm_ple, w_ple_gate, w_ple_proj, norm_final):
    batch, seq, d = x.shape
    depth = w_in.shape[0]
    n = batch * seq
    out_dtype = x.dtype
    x = x.reshape(n, d).astype(F32)
    assert WINDOW == SWA_BLOCK
    bias = folded_relative_bias(rel_bias_table)
    p = p.reshape(depth, n, PLE_DIM)
    w_out, w_ple_gate = w_out.astype(BF16), w_ple_gate.astype(BF16)
    w_tails = jnp.zeros((depth, D_MODEL, LANES), BF16).at[:, :, :IN_TAIL_W].set(w_in[:, :, IN_MAIN_W:].astype(BF16))
    w_in = jnp.swapaxes(w_in, 1, 2)
    h = rmsnorm_rows(x, norm_mix[0], BF16)
    for i in range(depth):
        x, hf = hybrid_mixer_residual(x, h, w_in, w_tails[i], i, conv_w[i], a_log[i], dt_bias[i], gdn_norm[i],
                                      attn_sinks[i], bias, w_out, norm_ffn[i], batch, seq)
        if i % 2 == 0:
            act = gate_up(hf, w_dense_gate, w_dense_up, i // 2)
            x = down_residual(act, w_dense_down, i // 2, x)
        else:
            x = moe_block(x, norm_ffn[i], w_router[i // 2], w_exp_gate, w_exp_up, w_exp_down, i // 2)
        last = i == depth - 1
        x, h = ple_residual(norm_ple[i], w_ple_gate, p, w_ple_proj, i, x,
                            norm_final if last else norm_mix[i + 1], out_dtype if last else BF16)
    return h.reshape(batch, seq, d)
```
